```python
import math
import jax, jax.numpy as jnp
from jax import lax
import numpy as np

D_MODEL = 1024
BATCH = 4
SEQ = 8192
DEPTH = 1

GLA_HEADS = 4
GLA_DK = D_MODEL // 16
GLA_DV = D_MODEL // 8
GLA_LOWRANK = 16
GLA_TAU = 16.0
GLA_CHUNK = 64
MOBA_HEADS = 4
MOBA_DH = 128
MOBA_BLOCK = 256
MOBA_TOPK = 3
MOBA_QCHUNK = 32
NUM_BUCKETS = 32
MAX_DISTANCE = 128
MAX_EXACT = NUM_BUCKETS // 2
D_FF = 2816
FFN_RES = 0.5
EPS = 1e-6
N_MOD = 9
GLA_QK_W = GLA_HEADS * GLA_DK
GLA_V_W = GLA_HEADS * GLA_DV
MOBA_W = MOBA_HEADS * MOBA_DH
IN_SIZES = (GLA_QK_W, GLA_QK_W, GLA_V_W, GLA_LOWRANK, GLA_V_W, MOBA_W, MOBA_W, MOBA_W, D_MODEL, D_MODEL)
IN_WIDTH = sum(IN_SIZES)

kernel_name = "hybrid_gla_moba_macaron_adaln"


def rmsnorm(x, g):
    xf = x.astype(jnp.float32)
    y = xf * lax.rsqrt(jnp.mean(xf * xf, axis=-1, keepdims=True) + EPS) * g.astype(jnp.float32)
    return y.astype(x.dtype)


def modulate(x, shift, scale):
    return x * (1.0 + scale[:, None, :]) + shift[:, None, :]


def swiglu(x, w_gate, w_up, w_down):
    return (jax.nn.silu(x @ w_gate) * (x @ w_up)) @ w_down


def rel_bucket(dist):
    n = jnp.maximum(dist, 0)
    nf = jnp.maximum(n, 1).astype(jnp.float32)
    large = MAX_EXACT + (jnp.log(nf / MAX_EXACT) / math.log(MAX_DISTANCE / MAX_EXACT)
                         * (NUM_BUCKETS - MAX_EXACT)).astype(jnp.int32)
    large = jnp.minimum(large, NUM_BUCKETS - 1)
    return jnp.where(n < MAX_EXACT, n, large)


def gla_chunked(q, k, v, log_a):
    B, H, S, dk = q.shape
    dv = v.shape[-1]
    nc = S // GLA_CHUNK

    def to_chunks(t):
        return jnp.moveaxis(t.astype(jnp.float32).reshape(B, H, nc, GLA_CHUNK, t.shape[-1]), 2, 0)

    causal = jnp.tril(jnp.ones((GLA_CHUNK, GLA_CHUNK), dtype=bool))

    def step(state, inp):
        qc, kc, vc, gc = inp
        b = jnp.cumsum(gc, axis=2)
        o_inter = jnp.einsum('bhcd,bhde->bhce', qc * jnp.exp(b), state)
        diff = b[:, :, :, None, :] - b[:, :, None, :, :]
        decay = jnp.exp(jnp.where(causal[:, :, None], diff, -jnp.inf))
        attn = jnp.einsum('bhid,bhijd,bhjd->bhij', qc, decay, kc)
        o = o_inter + jnp.einsum('bhij,bhje->bhie', attn, vc)
        b_last = b[:, :, -1:, :]
        state = (jnp.exp(b_last[:, :, 0, :])[..., None] * state
                 + jnp.einsum('bhcd,bhce->bhde', kc * jnp.exp(b_last - b), vc))
        return state, o

    state0 = jnp.zeros((B, H, dk, dv), jnp.float32)
    _, o = lax.scan(step, state0, (to_chunks(q), to_chunks(k), to_chunks(v), to_chunks(log_a)))
    return jnp.moveaxis(o, 0, 2).reshape(B, H, S, dv)


def moba_attention(q, k, v, rel_bias):
    B, H, S, dh = q.shape
    nb = -(-S // MOBA_BLOCK)
    s_pad = nb * MOBA_BLOCK
    pad = ((0, 0), (0, 0), (0, s_pad - S), (0, 0))
    q = q.astype(jnp.float32)
    k = jnp.pad(k.astype(jnp.float32), pad)
    v = jnp.pad(v.astype(jnp.float32), pad)
    k_blocks = k.reshape(B, H, nb, MOBA_BLOCK, dh)
    v_blocks = v.reshape(B, H, nb, MOBA_BLOCK, dh)
    k_mean = jnp.mean(k_blocks, axis=3)
    topk = min(MOBA_TOPK, nb)
    scale = dh ** -0.5
    head_ix = jnp.arange(H)[None, :, None, None, None]
    gather = jax.vmap(jax.vmap(lambda blocks, ix: blocks[ix]))

    def chunk(ci):
        start = ci * MOBA_QCHUNK
        q_c = lax.dynamic_slice_in_dim(q, start, MOBA_QCHUNK, axis=2)
        q_pos = start + jnp.arange(MOBA_QCHUNK)
        cur = start // MOBA_BLOCK
        gate = jnp.einsum('bhqd,bhnd->bhqn', q_c, k_mean)
        gate = jnp.where(jnp.arange(nb) < cur, gate, -jnp.inf)
        _, sel = lax.top_k(gate, topk)
        sel_valid = jnp.arange(topk) < cur
        k_sel = gather(k_blocks, sel)
        v_sel = gather(v_blocks, sel)
        sel_pos = sel[..., None] * MOBA_BLOCK + jnp.arange(MOBA_BLOCK)
        sel_bias = rel_bias[head_ix, rel_bucket(q_pos[:, None, None] - sel_pos)]
        s_sel = jnp.einsum('bhqd,bhqrkd->bhqrk', q_c, k_sel) * scale + sel_bias
        s_sel = jnp.where(sel_valid[:, None], s_sel, -jnp.inf)
        own_start = cur * MOBA_BLOCK
        k_own = lax.dynamic_slice_in_dim(k, own_start, MOBA_BLOCK, axis=2)
        v_own = lax.dynamic_slice_in_dim(v, own_start, MOBA_BLOCK, axis=2)
        rel = q_pos[:, None] - (own_start + jnp.arange(MOBA_BLOCK))[None, :]
        s_own = jnp.einsum('bhqd,bhkd->bhqk', q_c, k_own) * scale + rel_bias[:, rel_bucket(rel)]
        s_own = jnp.where(rel >= 0, s_own, -jnp.inf)
        n_sel = topk * MOBA_BLOCK
        logits = jnp.concatenate([s_sel.reshape(B, H, MOBA_QCHUNK, n_sel), s_own], axis=-1)
        p = jax.nn.softmax(logits, axis=-1)
        p_sel = p[..., :n_sel].reshape(B, H, MOBA_QCHUNK, topk, MOBA_BLOCK)
        p_own = p[..., n_sel:]
        return (jnp.einsum('bhqrk,bhqrkd->bhqd', p_sel, v_sel)
                + jnp.einsum('bhqk,bhkd->bhqd', p_own, v_own))

    out = lax.map(chunk, jnp.arange(S // MOBA_QCHUNK))
    return jnp.transpose(out, (1, 2, 0, 3, 4)).reshape(B, H, S, dh)


def token_mixing(u, w_in, w_gla_lr, b_gla_lr, gla_norm, rel_bias, w_br_gla, w_br_moba, w_out):
    B, S, _ = u.shape
    proj = u @ w_in
    offsets = np.cumsum(IN_SIZES)[:-1].tolist()
    gq, gk, gv, glr, gog, mq, mk, mv, ga, gb = jnp.split(proj, offsets, axis=-1)

    def heads(t, n):
        return t.reshape(B, S, n, -1).transpose(0, 2, 1, 3)

    log_a = jax.nn.log_sigmoid((glr @ w_gla_lr + b_gla_lr).astype(jnp.float32)) / GLA_TAU
    o_a = gla_chunked(heads(gq, GLA_HEADS) * (GLA_DK ** -0.5), heads(gk, GLA_HEADS),
                      heads(gv, GLA_HEADS), heads(log_a, GLA_HEADS))
    o_a = rmsnorm(o_a.transpose(0, 2, 1, 3), gla_norm)
    o_a = o_a * jax.nn.silu(gog.reshape(B, S, GLA_HEADS, GLA_DV).astype(jnp.float32))
    y_a = o_a.reshape(B, S, GLA_V_W).astype(u.dtype) @ w_br_gla

    o_b = moba_attention(heads(mq, MOBA_HEADS), heads(mk, MOBA_HEADS), heads(mv, MOBA_HEADS), rel_bias)
    y_b = o_b.transpose(0, 2, 1, 3).reshape(B, S, MOBA_W).astype(u.dtype) @ w_br_moba

    merged = jax.nn.sigmoid(ga) * y_a + jax.nn.sigmoid(gb) * y_b
    return merged @ w_out


def setup_inputs(seed: int = 0) -> dict:
    key = jax.random.key(seed)
    ks = jax.random.split(key, 24)
    L, D, F = DEPTH, D_MODEL, D_FF
    nrm = lambda k, shape, fan_in: jax.random.normal(k, shape, jnp.float32) * (fan_in ** -0.5)
    gain = lambda k, shape: 1.0 + 0.05 * jax.random.normal(k, shape, jnp.float32)
    return {
        "x": jax.random.normal(ks[0], (BATCH, SEQ, D), jnp.float32),
        "c": jax.random.normal(ks[1], (BATCH, D), jnp.float32),
        "w_ada": nrm(ks[2], (L, D, N_MOD * D), D) * 0.5,
        "b_ada": 0.01 * jax.random.normal(ks[3], (L, N_MOD * D), jnp.float32),
        "norm_ff1": gain(ks[4], (L, D)),
        "w_ff1_gate": nrm(ks[5], (L, D, F), D),
        "w_ff1_up": nrm(ks[6], (L, D, F), D),
        "w_ff1_down": nrm(ks[7], (L, F, D), F),
        "norm_mix": gain(ks[8], (L, D)),
        "w_in": nrm(ks[9], (L, D, IN_WIDTH), D),
        "w_gla_lr": nrm(ks[10], (L, GLA_LOWRANK, GLA_QK_W), GLA_LOWRANK),
        "b_gla_lr": 0.1 * jax.random.normal(ks[11], (L, GLA_QK_W), jnp.float32),
        "gla_norm": gain(ks[12], (L, GLA_DV)),
        "rel_bias": 0.5 * jax.random.normal(ks[13], (MOBA_HEADS, NUM_BUCKETS), jnp.float32),
        "w_br_gla": nrm(ks[14], (L, GLA_V_W, D), GLA_V_W),
        "w_br_moba": nrm(ks[15], (L, MOBA_W, D), MOBA_W),
        "w_out": nrm(ks[16], (L, D, D), D),
        "norm_ff2": gain(ks[17], (L, D)),
        "w_ff2_gate": nrm(ks[18], (L, D, F), D),
        "w_ff2_up": nrm(ks[19], (L, D, F), D),
        "w_ff2_down": nrm(ks[20], (L, F, D), F),
        "norm_final": gain(ks[21], (D,)),
    }


def reference(x, c, w_ada, b_ada, norm_ff1, w_ff1_gate, w_ff1_up, w_ff1_down, norm_mix, w_in,
              w_gla_lr, b_gla_lr, gla_norm, rel_bias, w_br_gla, w_br_moba, w_out, norm_ff2,
              w_ff2_gate, w_ff2_up, w_ff2_down, norm_final):
    h = x
    c_act = jax.nn.silu(c)
    for l in range(DEPTH):
        mod = c_act @ w_ada[l] + b_ada[l]
        sh1, sc1, g1, sh2, sc2, g2, sh3, sc3, g3 = jnp.split(mod, N_MOD, axis=-1)
        u = modulate(rmsnorm(h, norm_ff1[l]), sh1, sc1)
        h = h + FFN_RES * g1[:, None, :] * swiglu(u, w_ff1_gate[l], w_ff1_up[l], w_ff1_down[l])
        u = modulate(rmsnorm(h, norm_mix[l]), sh2, sc2)
        h = h + g2[:, None, :] * token_mixing(u, w_in[l], w_gla_lr[l], b_gla_lr[l], gla_norm[l], rel_bias,
                                              w_br_gla[l], w_br_moba[l], w_out[l])
        u = modulate(rmsnorm(h, norm_ff2[l]), sh3, sc3)
        h = h + FFN_RES * g3[:, None, :] * swiglu(u, w_ff2_gate[l], w_ff2_up[l], w_ff2_down[l])
    return rmsnorm(h, norm_final)
```

```python
import functools
import math

import numpy as np
import jax
import jax.numpy as jnp
from jax import lax
from jax.experimental import pallas as pl
from jax.experimental.pallas import tpu as pltpu

F32 = jnp.float32
BF16 = jnp.bfloat16

EPS = 1e-6
FFN_RES = 0.5
GLA_HEADS = 4
GLA_TAU = 16.0
GLA_LOWRANK = 16
GLA_CHUNK = 128
GLA_STEP = 1024
MOBA_HEADS = 4
MOBA_DH = 128
MOBA_BLOCK = 256
MOBA_TOPK = 3
NUM_BUCKETS = 32
MAX_DISTANCE = 128
MAX_EXACT = NUM_BUCKETS // 2
NEG = -1e30
TOKEN_TILE = 512
FF_CHUNK = 256
VMEM_LIMIT = 56 * 1024 * 1024


def _dot(a, b):
    return jnp.dot(a, b, preferred_element_type=F32)


def _dot_nt(a, b):
    return lax.dot_general(a, b, (((1,), (1,)), ((), ())), preferred_element_type=F32)


def _dot_tn(a, b):
    return lax.dot_general(a, b, (((0,), (0,)), ((), ())), preferred_element_type=F32)


def _sigmoid(x):
    return 1.0 / (1.0 + jnp.exp(-x))


def _rmsnorm(x, w):
    return x * lax.rsqrt(jnp.mean(x * x, axis=-1, keepdims=True) + EPS) * w


def _norm_mod(x, w, mod):
    return _rmsnorm(x, w) * (1.0 + mod[1:2, :]) + mod[0:1, :]


def _params(sem):
    return pltpu.CompilerParams(dimension_semantics=sem, vmem_limit_bytes=VMEM_LIMIT)


def _const_spec(shape):
    nd = len(shape)
    return pl.BlockSpec(shape, lambda *_: (0,) * nd, pipeline_mode=pl.Buffered(1))


def _adaln_kernel(c_ref, w_ref, b_ref, o_ref):
    c = c_ref[...]
    ca = (c * _sigmoid(c)).astype(BF16)
    o_ref[...] = _dot(ca, w_ref[...].astype(BF16)) + b_ref[...]


def _adaln(c, w_ada, b_ada):
    bsz, d = c.shape
    n = w_ada.shape[1]
    return pl.pallas_call(
        _adaln_kernel,
        grid=(n // d,),
        in_specs=[pl.BlockSpec((bsz, d), lambda j: (0, 0)),
                  pl.BlockSpec((d, d), lambda j: (0, j)),
                  pl.BlockSpec((1, d), lambda j: (0, j))],
        out_specs=pl.BlockSpec((bsz, d), lambda j: (0, j)),
        out_shape=jax.ShapeDtypeStruct((bsz, n), F32),
        compiler_params=_params(("arbitrary",)),
        name="adaln",
    )(c, w_ada, b_ada.reshape(1, n))


def _ffn_kernel(x_ref, mod_ref, nw_ref, wg_ref, wu_ref, wd_ref, *rest, final_norm):
    if final_norm:
        nf_ref, o_ref, u_scr, a_scr = rest
    else:
        o_ref, u_scr, a_scr = rest
    mod = mod_ref[0, 0]
    u_scr[...] = _norm_mod(x_ref[...], nw_ref[...], mod).astype(BF16)
    d_ff = wg_ref.shape[1]
    for c in range(d_ff // FF_CHUNK):
        sl = slice(c * FF_CHUNK, (c + 1) * FF_CHUNK)
        g = _dot(u_scr[...], wg_ref[:, sl])
        up = _dot(u_scr[...], wu_ref[:, sl])
        a_scr[:, sl] = (g * _sigmoid(g) * up).astype(BF16)
    y = _dot(a_scr[...], wd_ref[...])
    out = x_ref[...] + (FFN_RES * mod[2:3, :]) * y
    if final_norm:
        out = _rmsnorm(out, nf_ref[...])
    o_ref[...] = out


def _ffn(h, mod4, layer, norm_w, wg, wu, wd, tiles_per_batch, norm_final=None):
    t, d = h.shape
    d_ff = wg.shape[1]
    tm = TOKEN_TILE
    in_specs = [pl.BlockSpec((tm, d), lambda i: (i, 0)),
                pl.BlockSpec((1, 1, 3, d), lambda i: (i // tiles_per_batch, layer, 0, 0)),
                _const_spec((1, d)), _const_spec((d, d_ff)), _const_spec((d, d_ff)), _const_spec((d_ff, d))]
    args = [h, mod4, norm_w, wg, wu, wd]
    if norm_final is not None:
        in_specs.append(_const_spec((1, d)))
        args.append(norm_final)
    return pl.pallas_call(
        functools.partial(_ffn_kernel, final_norm=norm_final is not None),
        grid=(t // tm,),
        in_specs=in_specs,
        out_specs=pl.BlockSpec((tm, d), lambda i: (i, 0)),
        out_shape=jax.ShapeDtypeStruct((t, d), F32),
        scratch_shapes=[pltpu.VMEM((tm, d), BF16), pltpu.VMEM((tm, d_ff), BF16)],
        compiler_params=_params(("arbitrary",)),
        name="ffn_final" if norm_final is not None else "ffn",
    )(*args)


_P_GQ, _P_GK, _P_GV, _P_GOG, _P_MQ, _P_MK, _P_LR, _P_END = 0, 256, 512, 1024, 1536, 2048, 2560, 2688


def _inproj_kernel(h_ref, mod_ref, nw_ref, wp_ref, wvt_ref, wlr_ref, blr_ref,
                   gq_ref, gk_ref, gv_ref, gog_ref, loga_ref, mq_ref, mk_ref, mvt_ref, kmean_ref, u_scr):
    u_scr[...] = _norm_mod(h_ref[...], nw_ref[...], mod_ref[0, 0]).astype(BF16)

    def proj(lo, hi):
        return _dot(u_scr[...], wp_ref[:, lo:hi])

    gq_ref[...] = proj(_P_GQ, _P_GK) * 0.125
    gk_ref[...] = proj(_P_GK, _P_GV)
    gv_ref[...] = proj(_P_GV, _P_GOG).astype(BF16)
    gog_ref[...] = proj(_P_GOG, _P_MQ)
    mq_ref[...] = proj(_P_MQ, _P_MK).astype(BF16)
    mk = proj(_P_MK, _P_LR)
    mk_ref[...] = mk.astype(BF16)
    nblk = mk.shape[0] // MOBA_BLOCK
    kmean_ref[0] = jnp.mean(mk.reshape(nblk, MOBA_BLOCK, mk.shape[1]), axis=1)
    glr = proj(_P_LR, _P_END).astype(BF16)
    z = _dot(glr, wlr_ref[...]) + blr_ref[...]
    loga_ref[...] = (jnp.minimum(z, 0.0) - jnp.log1p(jnp.exp(-jnp.abs(z)))) * (1.0 / GLA_TAU)
    mvt = _dot_nt(wvt_ref[...], u_scr[...])
    for h in range(MOBA_HEADS):
        for j in range(nblk):
            mvt_ref[0, h, j] = mvt[h * MOBA_DH:(h + 1) * MOBA_DH,
                                   j * MOBA_BLOCK:(j + 1) * MOBA_BLOCK].astype(BF16)


def _inproj(h, mod4, norm_w, wp, wvt, wlr, blr, bsz, seq):
    t, d = h.shape
    tm = TOKEN_TILE
    tiles_per_batch = seq // tm
    nblk = tm // MOBA_BLOCK
    nb = seq // MOBA_BLOCK
    row = lambda i: (i, 0)
    out_shape = [jax.ShapeDtypeStruct((t, 256), F32), jax.ShapeDtypeStruct((t, 256), F32),
                 jax.ShapeDtypeStruct((t, 512), BF16), jax.ShapeDtypeStruct((t, 512), F32),
                 jax.ShapeDtypeStruct((t, 256), F32), jax.ShapeDtypeStruct((t, 512), BF16),
                 jax.ShapeDtypeStruct((t, 512), BF16),
                 jax.ShapeDtypeStruct((bsz, MOBA_HEADS, nb, MOBA_DH, MOBA_BLOCK), BF16),
                 jax.ShapeDtypeStruct((t // tm, nblk, 512), F32)]
    out_specs = [pl.BlockSpec((tm, 256), row), pl.BlockSpec((tm, 256), row),
                 pl.BlockSpec((tm, 512), row), pl.BlockSpec((tm, 512), row),
                 pl.BlockSpec((tm, 256), row), pl.BlockSpec((tm, 512), row),
                 pl.BlockSpec((tm, 512), row),
                 pl.BlockSpec((1, MOBA_HEADS, nblk, MOBA_DH, MOBA_BLOCK),
                              lambda i: (i // tiles_per_batch, 0, i % tiles_per_batch, 0, 0)),
                 pl.BlockSpec((1, nblk, 512), lambda i: (i, 0, 0))]
    return pl.pallas_call(
        _inproj_kernel,
        grid=(t // tm,),
        in_specs=[pl.BlockSpec((tm, d), row),
                  pl.BlockSpec((1, 1, 3, d), lambda i: (i // tiles_per_batch, 1, 0, 0)),
                  _const_spec((1, d)), _const_spec(wp.shape), _const_spec(wvt.shape),
                  _const_spec(wlr.shape), _const_spec(blr.shape)],
        out_specs=out_specs,
        out_shape=out_shape,
        scratch_shapes=[pltpu.VMEM((tm, d), BF16)],
        compiler_params=_params(("arbitrary",)),
        name="inproj",
    )(h, mod4, norm_w, wp, wvt, wlr, blr)


def _gla_kernel(q_ref, k_ref, v_ref, og_ref, la_ref, nw_ref, o_ref, st_scr):
    L = GLA_CHUNK
    hq = q_ref.shape[1] // GLA_HEADS
    hv = v_ref.shape[1] // GLA_HEADS

    @pl.when(pl.program_id(1) == 0)
    def _():
        st_scr[...] = jnp.zeros_like(st_scr)

    ti = lax.broadcasted_iota(jnp.int32, (L, L), 0)
    tj = lax.broadcasted_iota(jnp.int32, (L, L), 1)
    tri = (tj <= ti).astype(BF16)
    xor = ti ^ tj
    trow = lax.broadcasted_iota(jnp.int32, (L, q_ref.shape[1]), 0)
    levels = [1 << p for p in range(int(math.log2(L)))]

    def chunk(c, carry):
        r0 = pl.multiple_of(c * L, L)
        q = q_ref[pl.ds(r0, L), :]
        k = k_ref[pl.ds(r0, L), :]
        g = la_ref[pl.ds(r0, L), :]
        v = v_ref[pl.ds(r0, L), :]
        g_hi = g.astype(BF16)
        g_lo = (g - g_hi.astype(F32)).astype(BF16)
        b = _dot(tri, g_hi) + _dot(tri, g_lo)
        b_last = b[L - 1:L, :]
        q_in = (q * jnp.exp(b)).astype(BF16)
        k_out = (k * jnp.exp(b_last - b)).astype(BF16)
        dec = jnp.exp(b_last)

        qs, ks = [q.astype(BF16)], [k.astype(BF16)]
        bs = b
        for s in levels:
            bit = (trow & s) != 0
            prev_end = pltpu.roll(bs, s, 0)
            qs.append(jnp.where(bit, q * jnp.exp(b - prev_end), 0.0).astype(BF16))
            ks.append(jnp.where(bit, 0.0, k * jnp.exp(bs - b)).astype(BF16))
            if 2 * s < L:
                bs = jnp.where(bit, bs, pltpu.roll(bs, L - s, 0))

        for h in range(GLA_HEADS):
            ql, vl = slice(h * hq, (h + 1) * hq), slice(h * hv, (h + 1) * hv)
            att = jnp.where(xor == 0, _dot_nt(qs[0][:, ql], ks[0][:, ql]), 0.0)
            for li, s in enumerate(levels):
                a_s = _dot_nt(qs[li + 1][:, ql], ks[li + 1][:, ql])
                att = jnp.where((xor >= s) & (xor < 2 * s) & (tj < ti), a_s, att)
            st = st_scr[h]
            vh = v[:, vl]
            o = _dot_nt(q_in[:, ql], st.astype(BF16)) + _dot(att.astype(BF16), vh)
            st_scr[h] = st * dec[:, ql] + _dot_tn(vh, k_out[:, ql])
            o = _rmsnorm(o, nw_ref[...])
            gate = og_ref[pl.ds(r0, L), vl]
            o_ref[pl.ds(r0, L), vl] = (o * (gate * _sigmoid(gate))).astype(BF16)
        return carry

    lax.fori_loop(0, q_ref.shape[0] // L, chunk, 0)


def _gla(gq, gk, gv, gog, loga, gla_norm, bsz, seq):
    t = gq.shape[0]
    step = min(GLA_STEP, seq)
    spb = seq // step
    row = lambda b, i: (b * spb + i, 0)
    hv = gv.shape[1] // GLA_HEADS
    hq = gq.shape[1] // GLA_HEADS
    return pl.pallas_call(
        _gla_kernel,
        grid=(bsz, spb),
        in_specs=[pl.BlockSpec((step, gq.shape[1]), row), pl.BlockSpec((step, gk.shape[1]), row),
                  pl.BlockSpec((step, gv.shape[1]), row), pl.BlockSpec((step, gog.shape[1]), row),
                  pl.BlockSpec((step, loga.shape[1]), row),
                  pl.BlockSpec((1, hv), lambda b, i: (0, 0))],
        out_specs=pl.BlockSpec((step, gv.shape[1]), row),
        out_shape=jax.ShapeDtypeStruct((t, gv.shape[1]), BF16),
        scratch_shapes=[pltpu.VMEM((GLA_HEADS, hv, hq), F32)],
        compiler_params=_params(("arbitrary", "arbitrary")),
        name="gla",
    )(gq, gk, gv, gog, loga, gla_norm)


def _rel_buckets(max_dist):
    n = np.arange(max_dist)
    nf = np.maximum(n, 1).astype(np.float64)
    large = MAX_EXACT + (np.log(nf / MAX_EXACT) / math.log(MAX_DISTANCE / MAX_EXACT)
                         * (NUM_BUCKETS - MAX_EXACT)).astype(np.int64)
    return np.where(n < MAX_EXACT, n, np.minimum(large, NUM_BUCKETS - 1))


def _bias_kernel(rb_ref, o_ref):
    h = pl.program_id(0)
    blk = MOBA_BLOCK
    buckets = _rel_buckets(2 * blk)
    assert (np.diff(buckets) >= 0).all()
    starts = {b: int(np.argmax(buckets == b)) for b in range(NUM_BUCKETS) if (buckets == b).any()}
    ki = lax.broadcasted_iota(jnp.int32, (blk, blk), 0)
    qi = lax.broadcasted_iota(jnp.int32, (blk, blk), 1)
    far = rb_ref[h, NUM_BUCKETS - 1]
    inv_scale = float(MOBA_DH) ** 0.5
    for tbl, base in ((0, 0), (1, blk)):
        dist = qi - ki + base
        val = jnp.full((blk, blk), rb_ref[h, 0], F32)
        for b in sorted(starts):
            if b > 0:
                val = jnp.where(dist >= starts[b], rb_ref[h, b], val)
        val = (val - far) * inv_scale
        if tbl == 0:
            val = jnp.where(dist >= 0, val, NEG)
        o_ref[0, tbl] = val


def _bias_tables(rel_bias):
    nh = rel_bias.shape[0]
    return pl.pallas_call(
        _bias_kernel,
        grid=(nh,),
        in_specs=[pl.BlockSpec(memory_space=pltpu.SMEM)],
        out_specs=pl.BlockSpec((1, 2, MOBA_BLOCK, MOBA_BLOCK), lambda h: (h, 0, 0, 0)),
        out_shape=jax.ShapeDtypeStruct((nh, 2, MOBA_BLOCK, MOBA_BLOCK), F32),
        compiler_params=_params(("arbitrary",)),
        name="moba_bias",
    )(rel_bias)


def _moba_kernel(q_ref, k_ref, vt_ref, km_ref, bias_ref, o_ref, off_scr, m_scr, l_scr, acc_scr):
    n = pl.program_id(1)
    blk = MOBA_BLOCK
    nb = km_ref.shape[1]
    scale = float(MOBA_DH) ** -0.5
    heads = range(MOBA_HEADS)
    hsl = [slice(h * MOBA_DH, (h + 1) * MOBA_DH) for h in heads]

    brow = lax.broadcasted_iota(jnp.int32, (nb, blk), 0).astype(F32)
    for h in heads:
        gate = _dot_nt(km_ref[0, :, hsl[h]].astype(BF16), q_ref[:, hsl[h]])
        gate = jnp.where(brow < n.astype(F32), gate, NEG)
        off = jnp.full((nb, blk), NEG, F32)
        for _ in range(MOBA_TOPK):
            best = jnp.max(gate, axis=0, keepdims=True)
            first = jnp.min(jnp.where(gate == best, brow, float(nb)), axis=0, keepdims=True)
            pick = (brow == first) & (best > NEG)
            off = jnp.where(pick, 0.0, off)
            gate = jnp.where(pick, NEG, gate)
        off_scr[h] = off
        m_scr[h] = jnp.full((1, blk), NEG, F32)
        l_scr[h] = jnp.zeros((1, blk), F32)
        acc_scr[h] = jnp.zeros((MOBA_DH, blk), F32)

    def attend(j, table, masked):
        r0 = pl.multiple_of(j * blk, blk)
        for h in heads:
            s = _dot_nt(k_ref[pl.ds(r0, blk), hsl[h]], q_ref[:, hsl[h]])
            if table is not None:
                s = s + bias_ref[h, table]
            if masked:
                s = s + off_scr[h, pl.ds(j, 1), :]
            m_old = m_scr[h]
            m_new = jnp.maximum(m_old, jnp.max(s, axis=0, keepdims=True))
            p = jnp.exp((s - m_new) * scale)
            alpha = jnp.exp((m_old - m_new) * scale)
            m_scr[h] = m_new
            l_scr[h] = alpha * l_scr[h] + jnp.sum(p, axis=0, keepdims=True)
            acc_scr[h] = alpha * acc_scr[h] + _dot(vt_ref[0, h, j], p.astype(BF16))

    def far_body(j, carry):
        attend(j, None, True)
        return carry

    lax.fori_loop(0, jnp.maximum(n - 1, 0), far_body, 0)

    @pl.when(n >= 1)
    def _():
        attend(n - 1, 1, True)

    attend(n, 0, False)

    for h in heads:
        o_ref[:, hsl[h]] = (acc_scr[h] / l_scr[h]).T.astype(BF16)


def _moba(mq, mk, mvt, kmean, bias, bsz, seq):
    t, w = mq.shape
    blk = MOBA_BLOCK
    nb = seq // blk
    return pl.pallas_call(
        _moba_kernel,
        grid=(bsz, nb),
        in_specs=[pl.BlockSpec((blk, w), lambda b, n: (b * nb + n, 0)),
                  pl.BlockSpec((seq, w), lambda b, n: (b, 0)),
                  pl.BlockSpec((1, MOBA_HEADS, nb, MOBA_DH, blk), lambda b, n: (b, 0, 0, 0, 0)),
                  pl.BlockSpec((1, nb, w), lambda b, n: (b, 0, 0)),
                  _const_spec(bias.shape)],
        out_specs=pl.BlockSpec((blk, w), lambda b, n: (b * nb + n, 0)),
        out_shape=jax.ShapeDtypeStruct((t, w), BF16),
        scratch_shapes=[pltpu.VMEM((MOBA_HEADS, nb, blk), F32),
                        pltpu.VMEM((MOBA_HEADS, 1, blk), F32),
                        pltpu.VMEM((MOBA_HEADS, 1, blk), F32),
                        pltpu.VMEM((MOBA_HEADS, MOBA_DH, blk), F32)],
        compiler_params=_params(("arbitrary", "arbitrary")),
        name="moba",
    )(mq, mk, mvt, kmean, bias)


def _merge_kernel(h_ref, mod_ref, nw_ref, oa_ref, ob_ref, wga_ref, wgb_ref, wa_ref, wb_ref, wo_ref,
                  o_ref, u_scr):
    mod = mod_ref[0, 0]
    u_scr[...] = _norm_mod(h_ref[...], nw_ref[...], mod).astype(BF16)
    ya = _dot(oa_ref[...], wa_ref[...])
    yb = _dot(ob_ref[...], wb_ref[...])
    merged = _sigmoid(_dot(u_scr[...], wga_ref[...])) * ya + _sigmoid(_dot(u_scr[...], wgb_ref[...])) * yb
    o_ref[...] = h_ref[...] + mod[2:3, :] * _dot(merged.astype(BF16), wo_ref[...])


def _merge(h, mod4, norm_w, oa, ob, wga, wgb, wa, wb, wo, tiles_per_batch):
    t, d = h.shape
    tm = TOKEN_TILE
    row = lambda i: (i, 0)
    return pl.pallas_call(
        _merge_kernel,
        grid=(t // tm,),
        in_specs=[pl.BlockSpec((tm, d), row),
                  pl.BlockSpec((1, 1, 3, d), lambda i: (i // tiles_per_batch, 1, 0, 0)),
                  _const_spec((1, d)),
                  pl.BlockSpec((tm, oa.shape[1]), row), pl.BlockSpec((tm, ob.shape[1]), row),
                  _const_spec(wga.shape), _const_spec(wgb.shape), _const_spec(wa.shape),
                  _const_spec(wb.shape), _const_spec(wo.shape)],
        out_specs=pl.BlockSpec((tm, d), row),
        out_shape=jax.ShapeDtypeStruct((t, d), F32),
        scratch_shapes=[pltpu.VMEM((tm, d), BF16)],
        compiler_params=_params(("arbitrary",)),
        name="merge",
    )(h, mod4, norm_w, oa, ob, wga, wgb, wa, wb, wo)


def kernel(x, c, w_ada, b_ada, norm_ff1, w_ff1_gate, w_ff1_up, w_ff1_down, norm_mix, w_in, w_gla_lr,
           b_gla_lr, gla_norm, rel_bias, w_br_gla, w_br_moba, w_out, norm_ff2, w_ff2_gate, w_ff2_up,
           w_ff2_down, norm_final):
    bsz, seq, d = x.shape
    depth = w_ada.shape[0]
    t = bsz * seq
    tiles_per_batch = seq // TOKEN_TILE
    qk_w = GLA_HEADS * (d // 16)
    v_w = GLA_HEADS * (d // 8)
    m_w = MOBA_HEADS * MOBA_DH
    offs = np.cumsum([0, qk_w, qk_w, v_w, GLA_LOWRANK, v_w, m_w, m_w, m_w, d, d])
    bias = _bias_tables(rel_bias)
    h = x.reshape(t, d)
    c_act_in = c
    for l in range(depth):
        mod4 = _adaln(c_act_in, w_ada[l], b_ada[l]).reshape(bsz, 3, 3, d)
        cast = lambda w: w.astype(BF16)
        h = _ffn(h, mod4, 0, norm_ff1[l].reshape(1, d), cast(w_ff1_gate[l]), cast(w_ff1_up[l]),
                 cast(w_ff1_down[l]), tiles_per_batch)
        wi = w_in[l]
        seg = lambda i: wi[:, offs[i]:offs[i + 1]]
        lr_pad = jnp.pad(seg(3), ((0, 0), (0, 128 - GLA_LOWRANK)))
        wp = cast(jnp.concatenate([seg(0), seg(1), seg(2), seg(4), seg(5), seg(6), lr_pad], axis=1))
        wvt = cast(seg(7).T)
        wlr = cast(jnp.pad(w_gla_lr[l], ((0, 128 - GLA_LOWRANK), (0, 0))))
        gq, gk, gv, gog, loga, mq, mk, mvt, kmean = _inproj(
            h, mod4, norm_mix[l].reshape(1, d), wp, wvt, wlr, b_gla_lr[l].reshape(1, qk_w), bsz, seq)
        o_a = _gla(gq, gk, gv, gog, loga, gla_norm[l].reshape(1, -1), bsz, seq)
        o_b = _moba(mq, mk, mvt, kmean.reshape(bsz, seq // MOBA_BLOCK, m_w), bias, bsz, seq)
        h = _merge(h, mod4, norm_mix[l].reshape(1, d), o_a, o_b, cast(seg(8)), cast(seg(9)),
                   cast(w_br_gla[l]), cast(w_br_moba[l]), cast(w_out[l]), tiles_per_batch)
        last = l == depth - 1
        h = _ffn(h, mod4, 2, norm_ff2[l].reshape(1, d), cast(w_ff2_gate[l]), cast(w_ff2_up[l]),
                 cast(w_ff2_down[l]), tiles_per_batch, norm_final.reshape(1, d) if last else None)
    return h.reshape(bsz, seq, d)
```

```python
import functools
import math

import numpy as np
import jax
import jax.numpy as jnp
from jax import lax
from jax.experimental import pallas as pl
from jax.experimental.pallas import tpu as pltpu

F32 = jnp.float32
BF16 = jnp.bfloat16

EPS = 1e-6
FFN_RES = 0.5
GLA_HEADS = 4
GLA_TAU = 16.0
GLA_LOWRANK = 16
GLA_CHUNK = 128
GLA_STEP = 1024
MOBA_HEADS = 4
MOBA_DH = 128
MOBA_BLOCK = 256
MOBA_TOPK = 3
NUM_BUCKETS = 32
MAX_DISTANCE = 128
MAX_EXACT = NUM_BUCKETS // 2
NEG = -1e30
LOG2E = 1.4426950408889634
MOBA_QSCALE = LOG2E * MOBA_DH ** -0.5
TOKEN_TILE = 512
FF_CHUNK = 256
VMEM_LIMIT = 56 * 1024 * 1024


def _dot(a, b):
    return jnp.dot(a, b, preferred_element_type=F32)


def _dot_nt(a, b):
    return lax.dot_general(a, b, (((1,), (1,)), ((), ())), preferred_element_type=F32)


def _dot_tn(a, b):
    return lax.dot_general(a, b, (((0,), (0,)), ((), ())), preferred_element_type=F32)


def _sigmoid(x):
    return 1.0 / (1.0 + jnp.exp(-x))


def _rmsnorm(x, w):
    return x * lax.rsqrt(jnp.mean(x * x, axis=-1, keepdims=True) + EPS) * w


def _norm_mod(x, w, mod):
    return _rmsnorm(x, w) * (1.0 + mod[1:2, :]) + mod[0:1, :]


def _params(sem):
    return pltpu.CompilerParams(dimension_semantics=sem, vmem_limit_bytes=VMEM_LIMIT)


def _const_spec(shape):
    nd = len(shape)
    return pl.BlockSpec(shape, lambda *_: (0,) * nd, pipeline_mode=pl.Buffered(1))


def _adaln_kernel(c_ref, w_ref, b_ref, o_ref):
    c = c_ref[...]
    ca = (c * _sigmoid(c)).astype(BF16)
    o_ref[...] = _dot(ca, w_ref[...].astype(BF16)) + b_ref[...]


def _adaln(c, w_ada, b_ada):
    bsz, d = c.shape
    n = w_ada.shape[1]
    return pl.pallas_call(
        _adaln_kernel,
        grid=(n // d,),
        in_specs=[pl.BlockSpec((bsz, d), lambda j: (0, 0)),
                  pl.BlockSpec((d, d), lambda j: (0, j)),
                  pl.BlockSpec((1, d), lambda j: (0, j))],
        out_specs=pl.BlockSpec((bsz, d), lambda j: (0, j)),
        out_shape=jax.ShapeDtypeStruct((bsz, n), F32),
        compiler_params=_params(("arbitrary",)),
        name="adaln",
    )(c, w_ada, b_ada.reshape(1, n))


def _ffn_kernel(x_ref, mod_ref, nw_ref, wg_ref, wu_ref, wd_ref, *rest, final_norm):
    if final_norm:
        nf_ref, o_ref, u_scr, a_scr = rest
    else:
        o_ref, u_scr, a_scr = rest
    mod = mod_ref[0, 0]
    u_scr[...] = _norm_mod(x_ref[...], nw_ref[...], mod).astype(BF16)
    d_ff = wg_ref.shape[1]
    for c in range(d_ff // FF_CHUNK):
        sl = slice(c * FF_CHUNK, (c + 1) * FF_CHUNK)
        g = _dot(u_scr[...], wg_ref[:, sl])
        up = _dot(u_scr[...], wu_ref[:, sl])
        a_scr[:, sl] = (g * _sigmoid(g) * up).astype(BF16)
    y = _dot(a_scr[...], wd_ref[...])
    out = x_ref[...] + (FFN_RES * mod[2:3, :]) * y
    if final_norm:
        out = _rmsnorm(out, nf_ref[...])
    o_ref[...] = out


def _ffn(h, mod4, layer, norm_w, wg, wu, wd, tiles_per_batch, norm_final=None):
    t, d = h.shape
    d_ff = wg.shape[1]
    tm = TOKEN_TILE
    in_specs = [pl.BlockSpec((tm, d), lambda i: (i, 0)),
                pl.BlockSpec((1, 1, 3, d), lambda i: (i // tiles_per_batch, layer, 0, 0)),
                _const_spec((1, d)), _const_spec((d, d_ff)), _const_spec((d, d_ff)), _const_spec((d_ff, d))]
    args = [h, mod4, norm_w, wg, wu, wd]
    if norm_final is not None:
        in_specs.append(_const_spec((1, d)))
        args.append(norm_final)
    return pl.pallas_call(
        functools.partial(_ffn_kernel, final_norm=norm_final is not None),
        grid=(t // tm,),
        in_specs=in_specs,
        out_specs=pl.BlockSpec((tm, d), lambda i: (i, 0)),
        out_shape=jax.ShapeDtypeStruct((t, d), F32),
        scratch_shapes=[pltpu.VMEM((tm, d), BF16), pltpu.VMEM((tm, d_ff), BF16)],
        compiler_params=_params(("arbitrary",)),
        name="ffn_final" if norm_final is not None else "ffn",
    )(*args)


_P_GQ, _P_GK, _P_GV, _P_GOG, _P_MQ, _P_MK, _P_LR, _P_END = 0, 256, 512, 1024, 1536, 2048, 2560, 2688


def _inproj_kernel(h_ref, mod_ref, nw_ref, wp_ref, wvt_ref, wlr_ref, blr_ref,
                   gq_ref, gk_ref, gv_ref, gog_ref, loga_ref, mq_ref, mk_ref, mvt_ref, kmean_ref, u_scr):
    u_scr[...] = _norm_mod(h_ref[...], nw_ref[...], mod_ref[0, 0]).astype(BF16)

    def proj(lo, hi):
        return _dot(u_scr[...], wp_ref[:, lo:hi])

    gq_ref[...] = proj(_P_GQ, _P_GK) * 0.125
    gk_ref[...] = proj(_P_GK, _P_GV)
    gv_ref[...] = proj(_P_GV, _P_GOG).astype(BF16)
    gog_ref[...] = proj(_P_GOG, _P_MQ)
    mq_ref[...] = (proj(_P_MQ, _P_MK) * MOBA_QSCALE).astype(BF16)
    mk = proj(_P_MK, _P_LR)
    mk_ref[...] = mk.astype(BF16)
    nblk = mk.shape[0] // MOBA_BLOCK
    kmean_ref[0] = jnp.mean(mk.reshape(nblk, MOBA_BLOCK, mk.shape[1]), axis=1)
    glr = proj(_P_LR, _P_END).astype(BF16)
    z = _dot(glr, wlr_ref[...]) + blr_ref[...]
    loga_ref[...] = (jnp.minimum(z, 0.0) - jnp.log1p(jnp.exp(-jnp.abs(z)))) * (1.0 / GLA_TAU)
    mvt = _dot_nt(wvt_ref[...], u_scr[...])
    for h in range(MOBA_HEADS):
        for j in range(nblk):
            mvt_ref[0, h, j] = mvt[h * MOBA_DH:(h + 1) * MOBA_DH,
                                   j * MOBA_BLOCK:(j + 1) * MOBA_BLOCK].astype(BF16)


def _inproj(h, mod4, norm_w, wp, wvt, wlr, blr, bsz, seq):
    t, d = h.shape
    tm = TOKEN_TILE
    tiles_per_batch = seq // tm
    nblk = tm // MOBA_BLOCK
    nb = seq // MOBA_BLOCK
    row = lambda i: (i, 0)
    out_shape = [jax.ShapeDtypeStruct((t, 256), F32), jax.ShapeDtypeStruct((t, 256), F32),
                 jax.ShapeDtypeStruct((t, 512), BF16), jax.ShapeDtypeStruct((t, 512), F32),
                 jax.ShapeDtypeStruct((t, 256), F32), jax.ShapeDtypeStruct((t, 512), BF16),
                 jax.ShapeDtypeStruct((t, 512), BF16),
                 jax.ShapeDtypeStruct((bsz, MOBA_HEADS, nb, MOBA_DH, MOBA_BLOCK), BF16),
                 jax.ShapeDtypeStruct((t // tm, nblk, 512), F32)]
    out_specs = [pl.BlockSpec((tm, 256), row), pl.BlockSpec((tm, 256), row),
                 pl.BlockSpec((tm, 512), row), pl.BlockSpec((tm, 512), row),
                 pl.BlockSpec((tm, 256), row), pl.BlockSpec((tm, 512), row),
                 pl.BlockSpec((tm, 512), row),
                 pl.BlockSpec((1, MOBA_HEADS, nblk, MOBA_DH, MOBA_BLOCK),
                              lambda i: (i // tiles_per_batch, 0, i % tiles_per_batch, 0, 0)),
                 pl.BlockSpec((1, nblk, 512), lambda i: (i, 0, 0))]
    return pl.pallas_call(
        _inproj_kernel,
        grid=(t // tm,),
        in_specs=[pl.BlockSpec((tm, d), row),
                  pl.BlockSpec((1, 1, 3, d), lambda i: (i // tiles_per_batch, 1, 0, 0)),
                  _const_spec((1, d)), _const_spec(wp.shape), _const_spec(wvt.shape),
                  _const_spec(wlr.shape), _const_spec(blr.shape)],
        out_specs=out_specs,
        out_shape=out_shape,
        scratch_shapes=[pltpu.VMEM((tm, d), BF16)],
        compiler_params=_params(("arbitrary",)),
        name="inproj",
    )(h, mod4, norm_w, wp, wvt, wlr, blr)


def _gla_kernel(q_ref, k_ref, v_ref, og_ref, la_ref, nw_ref, o_ref, st_scr):
    L = GLA_CHUNK
    hq = q_ref.shape[1] // GLA_HEADS
    hv = v_ref.shape[1] // GLA_HEADS

    @pl.when(pl.program_id(1) == 0)
    def _():
        st_scr[...] = jnp.zeros_like(st_scr)

    ti = lax.broadcasted_iota(jnp.int32, (L, L), 0)
    tj = lax.broadcasted_iota(jnp.int32, (L, L), 1)
    tri = (tj <= ti).astype(BF16)
    xor = ti ^ tj
    trow = lax.broadcasted_iota(jnp.int32, (L, q_ref.shape[1]), 0)
    levels = [1 << p for p in range(int(math.log2(L)))]

    def chunk(c, carry):
        r0 = pl.multiple_of(c * L, L)
        q = q_ref[pl.ds(r0, L), :]
        k = k_ref[pl.ds(r0, L), :]
        g = la_ref[pl.ds(r0, L), :]
        v = v_ref[pl.ds(r0, L), :]
        g_hi = g.astype(BF16)
        g_lo = (g - g_hi.astype(F32)).astype(BF16)
        b = _dot(tri, g_hi) + _dot(tri, g_lo)
        b_last = b[L - 1:L, :]
        q_in = (q * jnp.exp(b)).astype(BF16)
        k_out = (k * jnp.exp(b_last - b)).astype(BF16)
        dec = jnp.exp(b_last)

        qs, ks = [q.astype(BF16)], [k.astype(BF16)]
        bs = b
        for s in levels:
            bit = (trow & s) != 0
            prev_end = pltpu.roll(bs, s, 0)
            qs.append(jnp.where(bit, q * jnp.exp(b - prev_end), 0.0).astype(BF16))
            ks.append(jnp.where(bit, 0.0, k * jnp.exp(bs - b)).astype(BF16))
            if 2 * s < L:
                bs = jnp.where(bit, bs, pltpu.roll(bs, L - s, 0))

        for h in range(GLA_HEADS):
            ql, vl = slice(h * hq, (h + 1) * hq), slice(h * hv, (h + 1) * hv)
            att = jnp.where(xor == 0, _dot_nt(qs[0][:, ql], ks[0][:, ql]), 0.0)
            for li, s in enumerate(levels):
                a_s = _dot_nt(qs[li + 1][:, ql], ks[li + 1][:, ql])
                att = jnp.where((xor >= s) & (xor < 2 * s) & (tj < ti), a_s, att)
            st = st_scr[h]
            vh = v[:, vl]
            o = _dot_nt(q_in[:, ql], st.astype(BF16)) + _dot(att.astype(BF16), vh)
            st_scr[h] = st * dec[:, ql] + _dot_tn(vh, k_out[:, ql])
            o = _rmsnorm(o, nw_ref[...])
            gate = og_ref[pl.ds(r0, L), vl]
            o_ref[pl.ds(r0, L), vl] = (o * (gate * _sigmoid(gate))).astype(BF16)
        return carry

    lax.fori_loop(0, q_ref.shape[0] // L, chunk, 0)


def _gla(gq, gk, gv, gog, loga, gla_norm, bsz, seq):
    t = gq.shape[0]
    step = min(GLA_STEP, seq)
    spb = seq // step
    row = lambda b, i: (b * spb + i, 0)
    hv = gv.shape[1] // GLA_HEADS
    hq = gq.shape[1] // GLA_HEADS
    return pl.pallas_call(
        _gla_kernel,
        grid=(bsz, spb),
        in_specs=[pl.BlockSpec((step, gq.shape[1]), row), pl.BlockSpec((step, gk.shape[1]), row),
                  pl.BlockSpec((step, gv.shape[1]), row), pl.BlockSpec((step, gog.shape[1]), row),
                  pl.BlockSpec((step, loga.shape[1]), row),
                  pl.BlockSpec((1, hv), lambda b, i: (0, 0))],
        out_specs=pl.BlockSpec((step, gv.shape[1]), row),
        out_shape=jax.ShapeDtypeStruct((t, gv.shape[1]), BF16),
        scratch_shapes=[pltpu.VMEM((GLA_HEADS, hv, hq), F32)],
        compiler_params=_params(("arbitrary", "arbitrary")),
        name="gla",
    )(gq, gk, gv, gog, loga, gla_norm)


def _rel_buckets(max_dist):
    n = np.arange(max_dist)
    nf = np.maximum(n, 1).astype(np.float64)
    large = MAX_EXACT + (np.log(nf / MAX_EXACT) / math.log(MAX_DISTANCE / MAX_EXACT)
                         * (NUM_BUCKETS - MAX_EXACT)).astype(np.int64)
    return np.where(n < MAX_EXACT, n, np.minimum(large, NUM_BUCKETS - 1))


_TBL_OWN, _TBL_PREV, _TBL_FAR = 0, 1, 2


def _bias_kernel(rb_ref, o_ref):
    h = pl.program_id(0)
    blk = MOBA_BLOCK
    buckets = _rel_buckets(2 * blk)
    assert (np.diff(buckets) >= 0).all()
    starts = {b: int(np.argmax(buckets == b)) for b in range(NUM_BUCKETS) if (buckets == b).any()}
    ki = lax.broadcasted_iota(jnp.int32, (blk, blk), 0)
    qi = lax.broadcasted_iota(jnp.int32, (blk, blk), 1)
    far = rb_ref[h, NUM_BUCKETS - 1]
    for tbl, base in ((_TBL_OWN, 0), (_TBL_PREV, blk)):
        dist = qi - ki + base
        val = jnp.full((blk, blk), rb_ref[h, 0], F32)
        for b in sorted(starts):
            if b > 0:
                val = jnp.where(dist >= starts[b], rb_ref[h, b], val)
        val = (val - far) * LOG2E
        if tbl == _TBL_OWN:
            val = jnp.where(dist >= 0, val, NEG)
        o_ref[0, tbl] = val
    o_ref[0, _TBL_FAR] = jnp.zeros((blk, blk), F32)


def _bias_tables(rel_bias):
    nh = rel_bias.shape[0]
    return pl.pallas_call(
        _bias_kernel,
        grid=(nh,),
        in_specs=[pl.BlockSpec(memory_space=pltpu.SMEM)],
        out_specs=pl.BlockSpec((1, 3, MOBA_BLOCK, MOBA_BLOCK), lambda h: (h, 0, 0, 0)),
        out_shape=jax.ShapeDtypeStruct((nh, 3, MOBA_BLOCK, MOBA_BLOCK), F32),
        compiler_params=_params(("arbitrary",)),
        name="moba_bias",
    )(rel_bias)


def _moba_kernel(q_ref, k_ref, vt_ref, km_ref, bias_ref, o_ref,
                 off_scr, m_scr, l_scr, al_scr, acc_scr, p_scr):
    n = pl.program_id(1)
    blk = MOBA_BLOCK
    nb = km_ref.shape[1]
    heads = range(MOBA_HEADS)
    hsl = [slice(h * MOBA_DH, (h + 1) * MOBA_DH) for h in heads]

    brow = lax.broadcasted_iota(jnp.int32, (nb, blk), 0).astype(F32)
    for h in heads:
        gate = _dot_nt(km_ref[0, :, hsl[h]].astype(BF16), q_ref[:, hsl[h]])
        gate = jnp.where(brow < n.astype(F32), gate, NEG)
        off = jnp.full((nb, blk), NEG, F32)
        for _ in range(MOBA_TOPK):
            best = jnp.max(gate, axis=0, keepdims=True)
            first = jnp.min(jnp.where(gate == best, brow, float(nb)), axis=0, keepdims=True)
            pick = (brow == first) & (best > NEG)
            off = jnp.where(pick, 0.0, off)
            gate = jnp.where(pick, NEG, gate)
        off_scr[h] = off
        m_scr[h] = jnp.full((1, blk), NEG, F32)
        l_scr[h] = jnp.zeros((1, blk), F32)
        al_scr[h] = jnp.ones((1, blk), F32)
        acc_scr[h] = jnp.zeros((MOBA_DH, blk), F32)
        p_scr[h] = jnp.zeros((blk, blk), BF16)

    def flush(h, j):
        acc_scr[h] = al_scr[h] * acc_scr[h] + _dot(vt_ref[0, h, j], p_scr[h])

    def step(j, j_prev, table, masked):
        r0 = pl.multiple_of(j * blk, blk)
        s = []
        for h in heads:
            s.append(_dot_nt(k_ref[pl.ds(r0, blk), hsl[h]], q_ref[:, hsl[h]]))
            flush(h, j_prev)
        for h in heads:
            sh = s[h] + bias_ref[h, table]
            top = jnp.max(sh, axis=0, keepdims=True)
            m_old = m_scr[h]
            if masked:
                off = off_scr[h, pl.ds(j, 1), :]
                m_new = jnp.maximum(m_old, top + off)
                m_sub = jnp.where(off < 0.0, -NEG, m_new)
            else:
                m_new = jnp.maximum(m_old, top)
                m_sub = m_new
            p = jnp.exp2(sh - m_sub)
            alpha = jnp.exp2(m_old - m_new)
            m_scr[h] = m_new
            al_scr[h] = alpha
            l_scr[h] = alpha * l_scr[h] + jnp.sum(p, axis=0, keepdims=True)
            p_scr[h] = p.astype(BF16)

    def past_body(j, carry):
        step(j, jnp.maximum(j - 1, 0), jnp.where(j == n - 1, _TBL_PREV, _TBL_FAR), True)
        return carry

    lax.fori_loop(0, n, past_body, 0)
    step(n, jnp.maximum(n - 1, 0), _TBL_OWN, False)
    for h in heads:
        flush(h, n)
        o_ref[:, hsl[h]] = (acc_scr[h] / l_scr[h]).T.astype(BF16)


def _moba(mq, mk, mvt, kmean, bias, bsz, seq):
    t, w = mq.shape
    blk = MOBA_BLOCK
    nb = seq // blk
    return pl.pallas_call(
        _moba_kernel,
        grid=(bsz, nb),
        in_specs=[pl.BlockSpec((blk, w), lambda b, n: (b * nb + n, 0)),
                  pl.BlockSpec((seq, w), lambda b, n: (b, 0)),
                  pl.BlockSpec((1, MOBA_HEADS, nb, MOBA_DH, blk), lambda b, n: (b, 0, 0, 0, 0)),
                  pl.BlockSpec((1, nb, w), lambda b, n: (b, 0, 0)),
                  _const_spec(bias.shape)],
        out_specs=pl.BlockSpec((blk, w), lambda b, n: (b * nb + n, 0)),
        out_shape=jax.ShapeDtypeStruct((t, w), BF16),
        scratch_shapes=[pltpu.VMEM((MOBA_HEADS, nb, blk), F32),
                        pltpu.VMEM((MOBA_HEADS, 1, blk), F32),
                        pltpu.VMEM((MOBA_HEADS, 1, blk), F32),
                        pltpu.VMEM((MOBA_HEADS, 1, blk), F32),
                        pltpu.VMEM((MOBA_HEADS, MOBA_DH, blk), F32),
                        pltpu.VMEM((MOBA_HEADS, blk, blk), BF16)],
        compiler_params=_params(("arbitrary", "arbitrary")),
        name="moba",
    )(mq, mk, mvt, kmean, bias)


def _merge_kernel(h_ref, mod_ref, nw_ref, oa_ref, ob_ref, wga_ref, wgb_ref, wa_ref, wb_ref, wo_ref,
                  o_ref, u_scr):
    mod = mod_ref[0, 0]
    u_scr[...] = _norm_mod(h_ref[...], nw_ref[...], mod).astype(BF16)
    ya = _dot(oa_ref[...], wa_ref[...])
    yb = _dot(ob_ref[...], wb_ref[...])
    merged = _sigmoid(_dot(u_scr[...], wga_ref[...])) * ya + _sigmoid(_dot(u_scr[...], wgb_ref[...])) * yb
    o_ref[...] = h_ref[...] + mod[2:3, :] * _dot(merged.astype(BF16), wo_ref[...])


def _merge(h, mod4, norm_w, oa, ob, wga, wgb, wa, wb, wo, tiles_per_batch):
    t, d = h.shape
    tm = TOKEN_TILE
    row = lambda i: (i, 0)
    return pl.pallas_call(
        _merge_kernel,
        grid=(t // tm,),
        in_specs=[pl.BlockSpec((tm, d), row),
                  pl.BlockSpec((1, 1, 3, d), lambda i: (i // tiles_per_batch, 1, 0, 0)),
                  _const_spec((1, d)),
                  pl.BlockSpec((tm, oa.shape[1]), row), pl.BlockSpec((tm, ob.shape[1]), row),
                  _const_spec(wga.shape), _const_spec(wgb.shape), _const_spec(wa.shape),
                  _const_spec(wb.shape), _const_spec(wo.shape)],
        out_specs=pl.BlockSpec((tm, d), row),
        out_shape=jax.ShapeDtypeStruct((t, d), F32),
        scratch_shapes=[pltpu.VMEM((tm, d), BF16)],
        compiler_params=_params(("arbitrary",)),
        name="merge",
    )(h, mod4, norm_w, oa, ob, wga, wgb, wa, wb, wo)


def kernel(x, c, w_ada, b_ada, norm_ff1, w_ff1_gate, w_ff1_up, w_ff1_down, norm_mix, w_in, w_gla_lr,
           b_gla_lr, gla_norm, rel_bias, w_br_gla, w_br_moba, w_out, norm_ff2, w_ff2_gate, w_ff2_up,
           w_ff2_down, norm_final):
    bsz, seq, d = x.shape
    depth = w_ada.shape[0]
    t = bsz * seq
    tiles_per_batch = seq // TOKEN_TILE
    qk_w = GLA_HEADS * (d // 16)
    v_w = GLA_HEADS * (d // 8)
    m_w = MOBA_HEADS * MOBA_DH
    offs = np.cumsum([0, qk_w, qk_w, v_w, GLA_LOWRANK, v_w, m_w, m_w, m_w, d, d])
    bias = _bias_tables(rel_bias)
    h = x.reshape(t, d)
    c_act_in = c
    for l in range(depth):
        mod4 = _adaln(c_act_in, w_ada[l], b_ada[l]).reshape(bsz, 3, 3, d)
        cast = lambda w: w.astype(BF16)
        h = _ffn(h, mod4, 0, norm_ff1[l].reshape(1, d), cast(w_ff1_gate[l]), cast(w_ff1_up[l]),
                 cast(w_ff1_down[l]), tiles_per_batch)
        wi = w_in[l]
        seg = lambda i: wi[:, offs[i]:offs[i + 1]]
        lr_pad = jnp.pad(seg(3), ((0, 0), (0, 128 - GLA_LOWRANK)))
        wp = cast(jnp.concatenate([seg(0), seg(1), seg(2), seg(4), seg(5), seg(6), lr_pad], axis=1))
        wvt = cast(seg(7).T)
        wlr = cast(jnp.pad(w_gla_lr[l], ((0, 128 - GLA_LOWRANK), (0, 0))))
        gq, gk, gv, gog, loga, mq, mk, mvt, kmean = _inproj(
            h, mod4, norm_mix[l].reshape(1, d), wp, wvt, wlr, b_gla_lr[l].reshape(1, qk_w), bsz, seq)
        o_a = _gla(gq, gk, gv, gog, loga, gla_norm[l].reshape(1, -1), bsz, seq)
        o_b = _moba(mq, mk, mvt, kmean.reshape(bsz, seq // MOBA_BLOCK, m_w), bias, bsz, seq)
        h = _merge(h, mod4, norm_mix[l].reshape(1, d), o_a, o_b, cast(seg(8)), cast(seg(9)),
                   cast(w_br_gla[l]), cast(w_br_moba[l]), cast(w_out[l]), tiles_per_batch)
        last = l == depth - 1
        h = _ffn(h, mod4, 2, norm_ff2[l].reshape(1, d), cast(w_ff2_gate[l]), cast(w_ff2_up[l]),
                 cast(w_ff2_down[l]), tiles_per_batch, norm_final.reshape(1, d) if last else None)
    return h.reshape(bsz, seq, d)
```

```python
import functools
import math

import numpy as np
import jax
import jax.numpy as jnp
from jax import lax
from jax.experimental import pallas as pl
from jax.experimental.pallas import tpu as pltpu

F32 = jnp.float32
BF16 = jnp.bfloat16

EPS = 1e-6
FFN_RES = 0.5
GLA_HEADS = 4
GLA_TAU = 16.0
GLA_LOWRANK = 16
GLA_CHUNK = 128
GLA_STEP = 1024
MOBA_HEADS = 4
MOBA_DH = 128
MOBA_BLOCK = 256
MOBA_TOPK = 3
NUM_BUCKETS = 32
MAX_DISTANCE = 128
MAX_EXACT = NUM_BUCKETS // 2
NEG = -1e30
LOG2E = 1.4426950408889634
MOBA_QSCALE = LOG2E * MOBA_DH ** -0.5
TOKEN_TILE = 512
FF_CHUNK = 256
VMEM_LIMIT = 56 * 1024 * 1024


def _dot(a, b):
    return jnp.dot(a, b, preferred_element_type=F32)


def _dot_nt(a, b):
    return lax.dot_general(a, b, (((1,), (1,)), ((), ())), preferred_element_type=F32)


def _dot_tn(a, b):
    return lax.dot_general(a, b, (((0,), (0,)), ((), ())), preferred_element_type=F32)


def _sigmoid(x):
    return 1.0 / (1.0 + jnp.exp(-x))


def _rmsnorm(x, w):
    return x * lax.rsqrt(jnp.mean(x * x, axis=-1, keepdims=True) + EPS) * w


def _norm_mod(x, w, mod):
    return _rmsnorm(x, w) * (1.0 + mod[1:2, :]) + mod[0:1, :]


def _params(sem):
    return pltpu.CompilerParams(dimension_semantics=sem, vmem_limit_bytes=VMEM_LIMIT)


def _const_spec(shape):
    nd = len(shape)
    return pl.BlockSpec(shape, lambda *_: (0,) * nd, pipeline_mode=pl.Buffered(1))


def _adaln_kernel(c_ref, w_ref, b_ref, o_ref):
    c = c_ref[...]
    ca = (c * _sigmoid(c)).astype(BF16)
    o_ref[...] = _dot(ca, w_ref[...].astype(BF16)) + b_ref[...]


def _adaln(c, w_ada, b_ada):
    bsz, d = c.shape
    n = w_ada.shape[1]
    return pl.pallas_call(
        _adaln_kernel,
        grid=(n // d,),
        in_specs=[pl.BlockSpec((bsz, d), lambda j: (0, 0)),
                  pl.BlockSpec((d, d), lambda j: (0, j)),
                  pl.BlockSpec((1, d), lambda j: (0, j))],
        out_specs=pl.BlockSpec((bsz, d), lambda j: (0, j)),
        out_shape=jax.ShapeDtypeStruct((bsz, n), F32),
        compiler_params=_params(("arbitrary",)),
        name="adaln",
    )(c, w_ada, b_ada.reshape(1, n))


def _ffn_kernel(x_ref, mod_ref, nw_ref, wg_ref, wu_ref, wd_ref, *rest, final_norm):
    if final_norm:
        nf_ref, o_ref, u_scr, a_scr = rest
    else:
        o_ref, u_scr, a_scr = rest
    mod = mod_ref[0, 0]
    u_scr[...] = _norm_mod(x_ref[...], nw_ref[...], mod).astype(BF16)
    d_ff = wg_ref.shape[1]
    for c in range(d_ff // FF_CHUNK):
        sl = slice(c * FF_CHUNK, (c + 1) * FF_CHUNK)
        g = _dot(u_scr[...], wg_ref[:, sl])
        up = _dot(u_scr[...], wu_ref[:, sl])
        a_scr[:, sl] = (g * _sigmoid(g) * up).astype(BF16)
    y = _dot(a_scr[...], wd_ref[...])
    out = x_ref[...] + (FFN_RES * mod[2:3, :]) * y
    if final_norm:
        out = _rmsnorm(out, nf_ref[...])
    o_ref[...] = out


def _ffn(h, mod4, layer, norm_w, wg, wu, wd, tiles_per_batch, norm_final=None):
    t, d = h.shape
    d_ff = wg.shape[1]
    tm = TOKEN_TILE
    in_specs = [pl.BlockSpec((tm, d), lambda i: (i, 0)),
                pl.BlockSpec((1, 1, 3, d), lambda i: (i // tiles_per_batch, layer, 0, 0)),
                _const_spec((1, d)), _const_spec((d, d_ff)), _const_spec((d, d_ff)), _const_spec((d_ff, d))]
    args = [h, mod4, norm_w, wg, wu, wd]
    if norm_final is not None:
        in_specs.append(_const_spec((1, d)))
        args.append(norm_final)
    return pl.pallas_call(
        functools.partial(_ffn_kernel, final_norm=norm_final is not None),
        grid=(t // tm,),
        in_specs=in_specs,
        out_specs=pl.BlockSpec((tm, d), lambda i: (i, 0)),
        out_shape=jax.ShapeDtypeStruct((t, d), F32),
        scratch_shapes=[pltpu.VMEM((tm, d), BF16), pltpu.VMEM((tm, d_ff), BF16)],
        compiler_params=_params(("arbitrary",)),
        name="ffn_final" if norm_final is not None else "ffn",
    )(*args)


_P_GQ, _P_GK, _P_GV, _P_GOG, _P_MQ, _P_MK, _P_LR, _P_END = 0, 256, 512, 1024, 1536, 2048, 2560, 2688


def _inproj_kernel(h_ref, mod_ref, nw_ref, wp_ref, wvt_ref, wlr_ref, blr_ref,
                   gq_ref, gk_ref, gv_ref, gog_ref, loga_ref, mq_ref, mk_ref, mvt_ref, kmean_ref, u_scr):
    u_scr[...] = _norm_mod(h_ref[...], nw_ref[...], mod_ref[0, 0]).astype(BF16)

    def proj(lo, hi):
        return _dot(u_scr[...], wp_ref[:, lo:hi])

    gq_ref[...] = proj(_P_GQ, _P_GK) * 0.125
    gk_ref[...] = proj(_P_GK, _P_GV)
    gv_ref[...] = proj(_P_GV, _P_GOG).astype(BF16)
    gog_ref[...] = proj(_P_GOG, _P_MQ)
    mq_ref[...] = (proj(_P_MQ, _P_MK) * MOBA_QSCALE).astype(BF16)
    mk = proj(_P_MK, _P_LR)
    mk_ref[...] = mk.astype(BF16)
    nblk = mk.shape[0] // MOBA_BLOCK
    kmean_ref[0] = jnp.mean(mk.reshape(nblk, MOBA_BLOCK, mk.shape[1]), axis=1)
    glr = proj(_P_LR, _P_END).astype(BF16)
    z = _dot(glr, wlr_ref[...]) + blr_ref[...]
    g = (jnp.minimum(z, 0.0) - jnp.log1p(jnp.exp(-jnp.abs(z)))) * (LOG2E / GLA_TAU)
    g_hi = g.astype(BF16)
    loga_ref[:, :g.shape[1]] = g_hi
    loga_ref[:, g.shape[1]:] = (g - g_hi.astype(F32)).astype(BF16)
    mvt = _dot_nt(wvt_ref[...], u_scr[...])
    for h in range(MOBA_HEADS):
        for j in range(nblk):
            mvt_ref[0, h, j] = mvt[h * MOBA_DH:(h + 1) * MOBA_DH,
                                   j * MOBA_BLOCK:(j + 1) * MOBA_BLOCK].astype(BF16)


def _inproj(h, mod4, norm_w, wp, wvt, wlr, blr, bsz, seq):
    t, d = h.shape
    tm = TOKEN_TILE
    tiles_per_batch = seq // tm
    nblk = tm // MOBA_BLOCK
    nb = seq // MOBA_BLOCK
    row = lambda i: (i, 0)
    out_shape = [jax.ShapeDtypeStruct((t, 256), F32), jax.ShapeDtypeStruct((t, 256), F32),
                 jax.ShapeDtypeStruct((t, 512), BF16), jax.ShapeDtypeStruct((t, 512), F32),
                 jax.ShapeDtypeStruct((t, 512), BF16), jax.ShapeDtypeStruct((t, 512), BF16),
                 jax.ShapeDtypeStruct((t, 512), BF16),
                 jax.ShapeDtypeStruct((bsz, MOBA_HEADS, nb, MOBA_DH, MOBA_BLOCK), BF16),
                 jax.ShapeDtypeStruct((t // tm, nblk, 512), F32)]
    out_specs = [pl.BlockSpec((tm, 256), row), pl.BlockSpec((tm, 256), row),
                 pl.BlockSpec((tm, 512), row), pl.BlockSpec((tm, 512), row),
                 pl.BlockSpec((tm, 512), row), pl.BlockSpec((tm, 512), row),
                 pl.BlockSpec((tm, 512), row),
                 pl.BlockSpec((1, MOBA_HEADS, nblk, MOBA_DH, MOBA_BLOCK),
                              lambda i: (i // tiles_per_batch, 0, i % tiles_per_batch, 0, 0)),
                 pl.BlockSpec((1, nblk, 512), lambda i: (i, 0, 0))]
    return pl.pallas_call(
        _inproj_kernel,
        grid=(t // tm,),
        in_specs=[pl.BlockSpec((tm, d), row),
                  pl.BlockSpec((1, 1, 3, d), lambda i: (i // tiles_per_batch, 1, 0, 0)),
                  _const_spec((1, d)), _const_spec(wp.shape), _const_spec(wvt.shape),
                  _const_spec(wlr.shape), _const_spec(blr.shape)],
        out_specs=out_specs,
        out_shape=out_shape,
        scratch_shapes=[pltpu.VMEM((tm, d), BF16)],
        compiler_params=_params(("arbitrary",)),
        name="inproj",
    )(h, mod4, norm_w, wp, wvt, wlr, blr)


def _gla_kernel(q_ref, k_ref, v_ref, og_ref, g_ref, nw_ref, o_ref, st_scr, tri_scr, lev_scr, b_scr):
    L = GLA_CHUNK
    hq = q_ref.shape[1] // GLA_HEADS
    hv = v_ref.shape[1] // GLA_HEADS
    wq = q_ref.shape[1]
    pairs = GLA_HEADS // 2
    assert 2 * hq == 128 and hv == 128 and L == 128
    levels = [1 << p for p in range(int(math.log2(L)))]

    @pl.when(pl.program_id(1) == 0)
    def _():
        st_scr[...] = jnp.zeros_like(st_scr)
        tri_scr[...] = (lax.broadcasted_iota(jnp.int32, (L, L), 1)
                        <= lax.broadcasted_iota(jnp.int32, (L, L), 0)).astype(BF16)
        ti = lax.broadcasted_iota(jnp.int32, (L, 2 * L), 0)
        tj = lax.broadcasted_iota(jnp.int32, (L, 2 * L), 1) & (L - 1)
        xor = ti ^ tj
        lev = jnp.where(tj > ti, -1, 0)
        for p in range(len(levels)):
            lev = jnp.where((xor >= (1 << p)) & (tj < ti), p + 1, lev)
        lev_scr[...] = lev

    sub = lax.broadcasted_iota(jnp.int32, (L, wq), 0) & 7
    first_head = lax.broadcasted_iota(jnp.int32, (L, 2 * hq), 1) < hq
    stbd = ((lax.broadcasted_iota(jnp.int32, (2 * hv, 2 * hq), 0) < hv)
            == (lax.broadcasted_iota(jnp.int32, (2 * hv, 2 * hq), 1) < hq))

    def rows_bcast(slot, first, period):
        return jnp.concatenate([jnp.broadcast_to(b_scr[slot, r:r + 1, :], (period, wq))
                                for r in range(first, L, period)], axis=0)

    def midpoint(slot, s):
        if s >= 4:
            return rows_bcast(slot, s - 1, 2 * s)
        if s == 2:
            return jnp.where(sub < 4, rows_bcast(slot, 1, 8), rows_bcast(slot, 5, 8))
        return jnp.where(sub < 2, rows_bcast(slot, 0, 8),
                         jnp.where(sub < 4, rows_bcast(slot, 2, 8),
                                   jnp.where(sub < 6, rows_bcast(slot, 4, 8), rows_bcast(slot, 6, 8))))

    def pair_scores(qx, kx, p):
        kp = kx[:, p * 2 * hq:(p + 1) * 2 * hq]
        kk = jnp.concatenate([jnp.where(first_head, kp, 0.0), jnp.where(first_head, 0.0, kp)], axis=0)
        return _dot_nt(qx[:, p * 2 * hq:(p + 1) * 2 * hq].astype(BF16), kk.astype(BF16))

    def chunk(c, slot):
        r0 = pl.multiple_of(c * L, L)
        q = q_ref[pl.ds(r0, L), :]
        k = k_ref[pl.ds(r0, L), :]
        v = v_ref[pl.ds(r0, L), :]
        bb = _dot(tri_scr[...], g_ref[pl.ds(r0, L), :])
        b = bb[:, :wq] + bb[:, wq:]
        b_scr[slot] = b
        b_last = b[L - 1:L, :]
        q_in = (q * jnp.exp2(b)).astype(BF16)
        k_out = (k * jnp.exp2(b_last - b)).astype(BF16)
        dec = jnp.exp2(b_last)

        att = [jnp.where(lev_scr[...] == 0, pair_scores(q, k, p), 0.0) for p in range(pairs)]
        for li, s in enumerate(levels):
            w = jnp.exp2(-jnp.abs(b - midpoint(slot, s)))
            qx, kx = q * w, k * w
            hit = lev_scr[...] == li + 1
            att = [jnp.where(hit, pair_scores(qx, kx, p), att[p]) for p in range(pairs)]

        for p in range(pairs):
            ql = slice(p * 2 * hq, (p + 1) * 2 * hq)
            vl = slice(p * 2 * hv, (p + 1) * 2 * hv)
            st = st_scr[p]
            vp = v[:, vl]
            ab = att[p].astype(BF16)
            o = _dot_nt(q_in[:, ql], st.astype(BF16)) + jnp.concatenate(
                [_dot(ab[:, :L], vp[:, :hv]), _dot(ab[:, L:], vp[:, hv:])], axis=1)
            st_scr[p] = st * dec[:, ql] + jnp.where(stbd, _dot_tn(vp, k_out[:, ql]), 0.0)
            for hh in range(2):
                sl = slice(vl.start + hh * hv, vl.start + (hh + 1) * hv)
                gate = og_ref[pl.ds(r0, L), sl]
                oh = _rmsnorm(o[:, hh * hv:(hh + 1) * hv], nw_ref[...])
                o_ref[pl.ds(r0, L), sl] = (oh * (gate * _sigmoid(gate))).astype(BF16)

    def chunk_pair(i, carry):
        chunk(2 * i, 0)
        chunk(2 * i + 1, 1)
        return carry

    lax.fori_loop(0, q_ref.shape[0] // (2 * L), chunk_pair, 0)


def _gla(gq, gk, gv, gog, loga, gla_norm, bsz, seq):
    t = gq.shape[0]
    step = min(GLA_STEP, seq)
    spb = seq // step
    row = lambda b, i: (b * spb + i, 0)
    hv = gv.shape[1] // GLA_HEADS
    hq = gq.shape[1] // GLA_HEADS
    return pl.pallas_call(
        _gla_kernel,
        grid=(bsz, spb),
        in_specs=[pl.BlockSpec((step, gq.shape[1]), row), pl.BlockSpec((step, gk.shape[1]), row),
                  pl.BlockSpec((step, gv.shape[1]), row), pl.BlockSpec((step, gog.shape[1]), row),
                  pl.BlockSpec((step, loga.shape[1]), row),
                  pl.BlockSpec((1, hv), lambda b, i: (0, 0))],
        out_specs=pl.BlockSpec((step, gv.shape[1]), row),
        out_shape=jax.ShapeDtypeStruct((t, gv.shape[1]), BF16),
        scratch_shapes=[pltpu.VMEM((GLA_HEADS // 2, 2 * hv, 2 * hq), F32),
                        pltpu.VMEM((GLA_CHUNK, GLA_CHUNK), BF16),
                        pltpu.VMEM((GLA_CHUNK, 2 * GLA_CHUNK), jnp.int32),
                        pltpu.VMEM((2, GLA_CHUNK, gq.shape[1]), F32)],
        compiler_params=_params(("arbitrary", "arbitrary")),
        name="gla",
    )(gq, gk, gv, gog, loga, gla_norm)


def _rel_buckets(max_dist):
    n = np.arange(max_dist)
    nf = np.maximum(n, 1).astype(np.float64)
    large = MAX_EXACT + (np.log(nf / MAX_EXACT) / math.log(MAX_DISTANCE / MAX_EXACT)
                         * (NUM_BUCKETS - MAX_EXACT)).astype(np.int64)
    return np.where(n < MAX_EXACT, n, np.minimum(large, NUM_BUCKETS - 1))


_TBL_OWN, _TBL_PREV, _TBL_FAR = 0, 1, 2


def _bias_kernel(rb_ref, o_ref):
    h = pl.program_id(0)
    blk = MOBA_BLOCK
    buckets = _rel_buckets(2 * blk)
    assert (np.diff(buckets) >= 0).all()
    starts = {b: int(np.argmax(buckets == b)) for b in range(NUM_BUCKETS) if (buckets == b).any()}
    ki = lax.broadcasted_iota(jnp.int32, (blk, blk), 0)
    qi = lax.broadcasted_iota(jnp.int32, (blk, blk), 1)
    far = rb_ref[h, NUM_BUCKETS - 1]
    for tbl, base in ((_TBL_OWN, 0), (_TBL_PREV, blk)):
        dist = qi - ki + base
        val = jnp.full((blk, blk), rb_ref[h, 0], F32)
        for b in sorted(starts):
            if b > 0:
                val = jnp.where(dist >= starts[b], rb_ref[h, b], val)
        val = (val - far) * LOG2E
        if tbl == _TBL_OWN:
            val = jnp.where(dist >= 0, val, NEG)
        o_ref[0, tbl] = val
    o_ref[0, _TBL_FAR] = jnp.zeros((blk, blk), F32)


def _bias_tables(rel_bias):
    nh = rel_bias.shape[0]
    return pl.pallas_call(
        _bias_kernel,
        grid=(nh,),
        in_specs=[pl.BlockSpec(memory_space=pltpu.SMEM)],
        out_specs=pl.BlockSpec((1, 3, MOBA_BLOCK, MOBA_BLOCK), lambda h: (h, 0, 0, 0)),
        out_shape=jax.ShapeDtypeStruct((nh, 3, MOBA_BLOCK, MOBA_BLOCK), F32),
        compiler_params=_params(("arbitrary",)),
        name="moba_bias",
    )(rel_bias)


def _moba_kernel(q_ref, k_ref, vt_ref, km_ref, bias_ref, o_ref,
                 off_scr, m_scr, l_scr, al_scr, acc_scr, p_scr, s_scr):
    n = pl.program_id(1)
    blk = MOBA_BLOCK
    nb = km_ref.shape[1]
    heads = range(MOBA_HEADS)
    hsl = [slice(h * MOBA_DH, (h + 1) * MOBA_DH) for h in heads]

    brow = lax.broadcasted_iota(jnp.int32, (nb, blk), 0).astype(F32)
    for h in heads:
        gate = _dot_nt(km_ref[0, :, hsl[h]].astype(BF16), q_ref[:, hsl[h]])
        gate = jnp.where(brow < n.astype(F32), gate, NEG)
        off = jnp.full((nb, blk), NEG, F32)
        for _ in range(MOBA_TOPK):
            best = jnp.max(gate, axis=0, keepdims=True)
            first = jnp.min(jnp.where(gate == best, brow, float(nb)), axis=0, keepdims=True)
            pick = (brow == first) & (best > NEG)
            off = jnp.where(pick, 0.0, off)
            gate = jnp.where(pick, NEG, gate)
        off_scr[h] = off
        m_scr[h] = jnp.full((1, blk), NEG, F32)
        l_scr[h] = jnp.zeros((1, blk), F32)
        acc_scr[h] = jnp.zeros((MOBA_DH, blk), F32)
        for slot in range(2):
            al_scr[slot, h] = jnp.ones((1, blk), F32)
            p_scr[slot, h] = jnp.zeros((blk, blk), BF16)

    def qk(h, j, slot):
        r0 = pl.multiple_of(j * blk, blk)
        s_scr[slot, h] = _dot_nt(k_ref[pl.ds(r0, blk), hsl[h]], q_ref[:, hsl[h]])

    def pv(h, j, slot):
        acc_scr[h] = al_scr[slot, h] * acc_scr[h] + _dot(vt_ref[0, h, j], p_scr[slot, h])

    def step(slot, j, j_next, j_prev, table, live):
        for h in heads:
            pv(h, j_prev, 1 - slot)
            if j_next is not None:
                qk(h, j_next, 1 - slot)
        for h in heads:
            sh = s_scr[slot, h]
            if table is not None:
                sh = sh + bias_ref[h, table]
            top = jnp.max(sh, axis=0, keepdims=True)
            m_old = m_scr[h]
            if live is not None:
                off = jnp.where(live, off_scr[h, pl.ds(j, 1), :], NEG)
                m_new = jnp.maximum(m_old, top + off)
                m_sub = jnp.where(off < 0.0, -NEG, m_new)
            else:
                m_new = jnp.maximum(m_old, top)
                m_sub = m_new
            p = jnp.exp2(sh - m_sub)
            alpha = jnp.exp2(m_old - m_new)
            m_scr[h] = m_new
            al_scr[slot, h] = alpha
            l_scr[h] = alpha * l_scr[h] + jnp.sum(p, axis=0, keepdims=True)
            p_scr[slot, h] = p.astype(BF16)

    n_far = jnp.maximum(n - 1, 0)
    last_far = jnp.maximum(n_far - 1, 0)
    j_before = jnp.maximum(n - 1, 0)
    far_blk = lambda u: jnp.clip(u, 0, last_far)

    for h in heads:
        qk(h, jnp.where(n_far > 0, 0, j_before), 0)

    def far_pair(i, carry):
        u = 2 * i
        step(0, far_blk(u), far_blk(u + 1), far_blk(u - 1), None, u < n_far)
        nxt = jnp.where(u + 2 < n_far, far_blk(u + 2), j_before)
        step(1, far_blk(u + 1), nxt, far_blk(u), None, u + 1 < n_far)
        return carry

    lax.fori_loop(0, (n_far + 1) // 2, far_pair, 0)
    step(0, j_before, n, last_far, _TBL_PREV, n >= 1)
    step(1, n, None, j_before, _TBL_OWN, None)
    for h in heads:
        pv(h, n, 1)
        o_ref[:, hsl[h]] = (acc_scr[h] / l_scr[h]).T.astype(BF16)


def _moba(mq, mk, mvt, kmean, bias, bsz, seq):
    t, w = mq.shape
    blk = MOBA_BLOCK
    nb = seq // blk
    return pl.pallas_call(
        _moba_kernel,
        grid=(bsz, nb),
        in_specs=[pl.BlockSpec((blk, w), lambda b, n: (b * nb + n, 0)),
                  pl.BlockSpec((seq, w), lambda b, n: (b, 0)),
                  pl.BlockSpec((1, MOBA_HEADS, nb, MOBA_DH, blk), lambda b, n: (b, 0, 0, 0, 0)),
                  pl.BlockSpec((1, nb, w), lambda b, n: (b, 0, 0)),
                  _const_spec(bias.shape)],
        out_specs=pl.BlockSpec((blk, w), lambda b, n: (b * nb + n, 0)),
        out_shape=jax.ShapeDtypeStruct((t, w), BF16),
        scratch_shapes=[pltpu.VMEM((MOBA_HEADS, nb, blk), F32),
                        pltpu.VMEM((MOBA_HEADS, 1, blk), F32),
                        pltpu.VMEM((MOBA_HEADS, 1, blk), F32),
                        pltpu.VMEM((2, MOBA_HEADS, 1, blk), F32),
                        pltpu.VMEM((MOBA_HEADS, MOBA_DH, blk), F32),
                        pltpu.VMEM((2, MOBA_HEADS, blk, blk), BF16),
                        pltpu.VMEM((2, MOBA_HEADS, blk, blk), F32)],
        compiler_params=_params(("arbitrary", "arbitrary")),
        name="moba",
    )(mq, mk, mvt, kmean, bias)


def _merge_kernel(h_ref, mod_ref, nw_ref, oa_ref, ob_ref, wga_ref, wgb_ref, wa_ref, wb_ref, wo_ref,
                  o_ref, u_scr):
    mod = mod_ref[0, 0]
    u_scr[...] = _norm_mod(h_ref[...], nw_ref[...], mod).astype(BF16)
    ya = _dot(oa_ref[...], wa_ref[...])
    yb = _dot(ob_ref[...], wb_ref[...])
    merged = _sigmoid(_dot(u_scr[...], wga_ref[...])) * ya + _sigmoid(_dot(u_scr[...], wgb_ref[...])) * yb
    o_ref[...] = h_ref[...] + mod[2:3, :] * _dot(merged.astype(BF16), wo_ref[...])


def _merge(h, mod4, norm_w, oa, ob, wga, wgb, wa, wb, wo, tiles_per_batch):
    t, d = h.shape
    tm = TOKEN_TILE
    row = lambda i: (i, 0)
    return pl.pallas_call(
        _merge_kernel,
        grid=(t // tm,),
        in_specs=[pl.BlockSpec((tm, d), row),
                  pl.BlockSpec((1, 1, 3, d), lambda i: (i // tiles_per_batch, 1, 0, 0)),
                  _const_spec((1, d)),
                  pl.BlockSpec((tm, oa.shape[1]), row), pl.BlockSpec((tm, ob.shape[1]), row),
                  _const_spec(wga.shape), _const_spec(wgb.shape), _const_spec(wa.shape),
                  _const_spec(wb.shape), _const_spec(wo.shape)],
        out_specs=pl.BlockSpec((tm, d), row),
        out_shape=jax.ShapeDtypeStruct((t, d), F32),
        scratch_shapes=[pltpu.VMEM((tm, d), BF16)],
        compiler_params=_params(("arbitrary",)),
        name="merge",
    )(h, mod4, norm_w, oa, ob, wga, wgb, wa, wb, wo)


def kernel(x, c, w_ada, b_ada, norm_ff1, w_ff1_gate, w_ff1_up, w_ff1_down, norm_mix, w_in, w_gla_lr,
           b_gla_lr, gla_norm, rel_bias, w_br_gla, w_br_moba, w_out, norm_ff2, w_ff2_gate, w_ff2_up,
           w_ff2_down, norm_final):
    bsz, seq, d = x.shape
    depth = w_ada.shape[0]
    t = bsz * seq
    tiles_per_batch = seq // TOKEN_TILE
    qk_w = GLA_HEADS * (d // 16)
    v_w = GLA_HEADS * (d // 8)
    m_w = MOBA_HEADS * MOBA_DH
    offs = np.cumsum([0, qk_w, qk_w, v_w, GLA_LOWRANK, v_w, m_w, m_w, m_w, d, d])
    bias = _bias_tables(rel_bias)
    h = x.reshape(t, d)
    c_act_in = c
    for l in range(depth):
        mod4 = _adaln(c_act_in, w_ada[l], b_ada[l]).reshape(bsz, 3, 3, d)
        cast = lambda w: w.astype(BF16)
        h = _ffn(h, mod4, 0, norm_ff1[l].reshape(1, d), cast(w_ff1_gate[l]), cast(w_ff1_up[l]),
                 cast(w_ff1_down[l]), tiles_per_batch)
        wi = w_in[l]
        seg = lambda i: wi[:, offs[i]:offs[i + 1]]
        lr_pad = jnp.pad(seg(3), ((0, 0), (0, 128 - GLA_LOWRANK)))
        wp = cast(jnp.concatenate([seg(0), seg(1), seg(2), seg(4), seg(5), seg(6), lr_pad], axis=1))
        wvt = cast(seg(7).T)
        wlr = cast(jnp.pad(w_gla_lr[l], ((0, 128 - GLA_LOWRANK), (0, 0))))
        gq, gk, gv, gog, loga, mq, mk, mvt, kmean = _inproj(
            h, mod4, norm_mix[l].reshape(1, d), wp, wvt, wlr, b_gla_lr[l].reshape(1, qk_w), bsz, seq)
        o_a = _gla(gq, gk, gv, gog, loga, gla_norm[l].reshape(1, -1), bsz, seq)
        o_b = _moba(mq, mk, mvt, kmean.reshape(bsz, seq // MOBA_BLOCK, m_w), bias, bsz, seq)
        h = _merge(h, mod4, norm_mix[l].reshape(1, d), o_a, o_b, cast(seg(8)), cast(seg(9)),
                   cast(w_br_gla[l]), cast(w_br_moba[l]), cast(w_out[l]), tiles_per_batch)
        last = l == depth - 1
        h = _ffn(h, mod4, 2, norm_ff2[l].reshape(1, d), cast(w_ff2_gate[l]), cast(w_ff2_up[l]),
                 cast(w_ff2_down[l]), tiles_per_batch, norm_final.reshape(1, d) if last else None)
    return h.reshape(bsz, seq, d)
```

```python
import functools
import math

import numpy as np
import jax
import jax.numpy as jnp
from jax import lax
from jax.experimental import pallas as pl
from jax.experimental.pallas import tpu as pltpu

F32 = jnp.float32
BF16 = jnp.bfloat16

EPS = 1e-6
FFN_RES = 0.5
GLA_HEADS = 4
GLA_TAU = 16.0
GLA_LOWRANK = 16
GLA_CHUNK = 128
GLA_STEP = 1024
MOBA_HEADS = 4
MOBA_DH = 128
MOBA_BLOCK = 256
MOBA_TOPK = 3
MOBA_VROWS = MOBA_DH + 16
NUM_BUCKETS = 32
MAX_DISTANCE = 128
MAX_EXACT = NUM_BUCKETS // 2
NEG = -1e30
LOG2E = 1.4426950408889634
MOBA_QSCALE = LOG2E * MOBA_DH ** -0.5
TOKEN_TILE = 1024
INPROJ_TILE = 512
FF_CHUNK = 256
VMEM_LIMIT = 56 * 1024 * 1024


def _dot(a, b):
    return jnp.dot(a, b, preferred_element_type=F32)


def _dot_nt(a, b):
    return lax.dot_general(a, b, (((1,), (1,)), ((), ())), preferred_element_type=F32)


def _dot_tn(a, b):
    return lax.dot_general(a, b, (((0,), (0,)), ((), ())), preferred_element_type=F32)


def _sigmoid(x):
    return 1.0 / (1.0 + jnp.exp(-x))


def _rmsnorm(x, w):
    return x * lax.rsqrt(jnp.mean(x * x, axis=-1, keepdims=True) + EPS) * w


def _norm_mod(x, w, mod):
    return _rmsnorm(x, w) * (1.0 + mod[1:2, :]) + mod[0:1, :]


def _params(sem, flags=None):
    return pltpu.CompilerParams(dimension_semantics=sem, vmem_limit_bytes=VMEM_LIMIT, flags=flags)


def _const_spec(shape):
    nd = len(shape)
    return pl.BlockSpec(shape, lambda *_: (0,) * nd, pipeline_mode=pl.Buffered(1))


def _adaln_kernel(c_ref, w_ref, b_ref, o_ref):
    c = c_ref[...]
    ca = (c * _sigmoid(c)).astype(BF16)
    o_ref[...] = _dot(ca, w_ref[...].astype(BF16)) + b_ref[...]


def _adaln(c, w_ada, b_ada, layer):
    bsz, d = c.shape
    depth, _, n = w_ada.shape
    return pl.pallas_call(
        _adaln_kernel,
        grid=(n // d,),
        in_specs=[pl.BlockSpec((bsz, d), lambda j: (0, 0)),
                  pl.BlockSpec((None, d, d), lambda j: (layer, 0, j)),
                  pl.BlockSpec((None, 1, d), lambda j: (layer, 0, j))],
        out_specs=pl.BlockSpec((bsz, d), lambda j: (0, j)),
        out_shape=jax.ShapeDtypeStruct((bsz, n), F32),
        compiler_params=_params(("arbitrary",)),
        name="adaln",
    )(c, w_ada, b_ada.reshape(depth, 1, n))


def _ffn_kernel(x_ref, mod_ref, nw_ref, wg_ref, wu_ref, wd_ref, *rest, final_norm):
    if final_norm:
        nf_ref, o_ref, u_scr, a_scr = rest
    else:
        o_ref, u_scr, a_scr = rest
    mod = mod_ref[0, 0]
    u_scr[...] = _norm_mod(x_ref[...], nw_ref[...], mod).astype(BF16)
    d_ff = wg_ref.shape[1]
    for c in range(d_ff // FF_CHUNK):
        sl = slice(c * FF_CHUNK, (c + 1) * FF_CHUNK)
        g = _dot(u_scr[...], wg_ref[:, sl])
        up = _dot(u_scr[...], wu_ref[:, sl])
        a_scr[:, sl] = (g * _sigmoid(g) * up).astype(BF16)
    y = _dot(a_scr[...], wd_ref[...])
    out = x_ref[...] + (FFN_RES * mod[2:3, :]) * y
    if final_norm:
        out = _rmsnorm(out, nf_ref[...])
    o_ref[...] = out


def _ffn(h, mod4, layer, norm_w, wg, wu, wd, tiles_per_batch, norm_final=None):
    t, d = h.shape
    d_ff = wg.shape[1]
    tm = TOKEN_TILE
    in_specs = [pl.BlockSpec((tm, d), lambda i: (i, 0)),
                pl.BlockSpec((1, 1, 3, d), lambda i: (i // tiles_per_batch, layer, 0, 0)),
                _const_spec((1, d)), _const_spec((d, d_ff)), _const_spec((d, d_ff)), _const_spec((d_ff, d))]
    args = [h, mod4, norm_w, wg, wu, wd]
    if norm_final is not None:
        in_specs.append(_const_spec((1, d)))
        args.append(norm_final)
    return pl.pallas_call(
        functools.partial(_ffn_kernel, final_norm=norm_final is not None),
        grid=(t // tm,),
        in_specs=in_specs,
        out_specs=pl.BlockSpec((tm, d), lambda i: (i, 0)),
        out_shape=jax.ShapeDtypeStruct((t, d), F32),
        scratch_shapes=[pltpu.VMEM((tm, d), BF16), pltpu.VMEM((tm, d_ff), BF16)],
        compiler_params=_params(("arbitrary",)),
        name="ffn_final" if norm_final is not None else "ffn",
    )(*args)


_P_GQ, _P_GK, _P_GV, _P_GOG, _P_MQ, _P_MK, _P_LR, _P_END = 0, 256, 512, 1024, 1536, 2048, 2560, 2688


def _inproj_kernel(h_ref, mod_ref, nw_ref, wp_ref, wvt_ref, wlr_ref, blr_ref,
                   gq_ref, gk_ref, gv_ref, gog_ref, loga_ref, mq_ref, mk_ref, mvt_ref, kmean_ref, u_scr):
    u_scr[...] = _norm_mod(h_ref[...], nw_ref[...], mod_ref[0, 0]).astype(BF16)

    def proj(lo, hi):
        return _dot(u_scr[...], wp_ref[:, lo:hi])

    gq_ref[...] = proj(_P_GQ, _P_GK) * 0.125
    gk_ref[...] = proj(_P_GK, _P_GV)
    gv_ref[...] = proj(_P_GV, _P_GOG).astype(BF16)
    gog_ref[...] = proj(_P_GOG, _P_MQ)
    mq_ref[...] = (proj(_P_MQ, _P_MK) * MOBA_QSCALE).astype(BF16)
    mk = proj(_P_MK, _P_LR)
    mk_ref[...] = mk.astype(BF16)
    nblk = mk.shape[0] // MOBA_BLOCK
    kmean_ref[0] = jnp.mean(mk.reshape(nblk, MOBA_BLOCK, mk.shape[1]), axis=1)
    glr = proj(_P_LR, _P_END).astype(BF16)
    z = _dot(glr, wlr_ref[...]) + blr_ref[...]
    g = (jnp.minimum(z, 0.0) - jnp.log1p(jnp.exp(-jnp.abs(z)))) * (LOG2E / GLA_TAU)
    g_hi = g.astype(BF16)
    loga_ref[:, :g.shape[1]] = g_hi
    loga_ref[:, g.shape[1]:] = (g - g_hi.astype(F32)).astype(BF16)
    mvt = _dot_nt(wvt_ref[...], u_scr[...])
    ones = jnp.ones((MOBA_VROWS - MOBA_DH, MOBA_BLOCK), BF16)
    for h in range(MOBA_HEADS):
        for j in range(nblk):
            mvt_ref[0, h, j, :MOBA_DH, :] = mvt[h * MOBA_DH:(h + 1) * MOBA_DH,
                                                j * MOBA_BLOCK:(j + 1) * MOBA_BLOCK].astype(BF16)
            mvt_ref[0, h, j, MOBA_DH:, :] = ones


def _inproj(h, mod4, norm_w, wp, wvt, wlr, blr, bsz, seq):
    t, d = h.shape
    tm = INPROJ_TILE
    tiles_per_batch = seq // tm
    nblk = tm // MOBA_BLOCK
    nb = seq // MOBA_BLOCK
    row = lambda i: (i, 0)
    out_shape = [jax.ShapeDtypeStruct((t, 256), F32), jax.ShapeDtypeStruct((t, 256), F32),
                 jax.ShapeDtypeStruct((t, 512), BF16), jax.ShapeDtypeStruct((t, 512), F32),
                 jax.ShapeDtypeStruct((t, 512), BF16), jax.ShapeDtypeStruct((t, 512), BF16),
                 jax.ShapeDtypeStruct((t, 512), BF16),
                 jax.ShapeDtypeStruct((bsz, MOBA_HEADS, nb, MOBA_VROWS, MOBA_BLOCK), BF16),
                 jax.ShapeDtypeStruct((t // tm, nblk, 512), F32)]
    out_specs = [pl.BlockSpec((tm, 256), row), pl.BlockSpec((tm, 256), row),
                 pl.BlockSpec((tm, 512), row), pl.BlockSpec((tm, 512), row),
                 pl.BlockSpec((tm, 512), row), pl.BlockSpec((tm, 512), row),
                 pl.BlockSpec((tm, 512), row),
                 pl.BlockSpec((1, MOBA_HEADS, nblk, MOBA_VROWS, MOBA_BLOCK),
                              lambda i: (i // tiles_per_batch, 0, i % tiles_per_batch, 0, 0)),
                 pl.BlockSpec((1, nblk, 512), lambda i: (i, 0, 0))]
    return pl.pallas_call(
        _inproj_kernel,
        grid=(t // tm,),
        in_specs=[pl.BlockSpec((tm, d), row),
                  pl.BlockSpec((1, 1, 3, d), lambda i: (i // tiles_per_batch, 1, 0, 0)),
                  _const_spec((1, d)), _const_spec(wp.shape), _const_spec(wvt.shape),
                  _const_spec(wlr.shape), _const_spec(blr.shape)],
        out_specs=out_specs,
        out_shape=out_shape,
        scratch_shapes=[pltpu.VMEM((tm, d), BF16)],
        compiler_params=_params(("arbitrary",)),
        name="inproj",
    )(h, mod4, norm_w, wp, wvt, wlr, blr)


def _gla_kernel(q_ref, k_ref, v_ref, og_ref, g_ref, nw_ref, o_ref, st_scr, tri_scr, lev_scr, b_scr):
    L = GLA_CHUNK
    hq = q_ref.shape[1] // GLA_HEADS
    hv = v_ref.shape[1] // GLA_HEADS
    wq = q_ref.shape[1]
    pairs = GLA_HEADS // 2
    assert 2 * hq == 128 and hv == 128 and L == 128
    levels = [1 << p for p in range(int(math.log2(L)))]

    @pl.when(pl.program_id(1) == 0)
    def _():
        st_scr[...] = jnp.zeros_like(st_scr)
        tri_scr[...] = (lax.broadcasted_iota(jnp.int32, (L, L), 1)
                        <= lax.broadcasted_iota(jnp.int32, (L, L), 0)).astype(BF16)
        ti = lax.broadcasted_iota(jnp.int32, (L, 2 * L), 0)
        tj = lax.broadcasted_iota(jnp.int32, (L, 2 * L), 1) & (L - 1)
        xor = ti ^ tj
        lev = jnp.where(tj > ti, -1, 0)
        for p in range(len(levels)):
            lev = jnp.where((xor >= (1 << p)) & (tj < ti), p + 1, lev)
        lev_scr[...] = lev

    sub = lax.broadcasted_iota(jnp.int32, (L, wq), 0) & 7
    even_head = lax.broadcasted_iota(jnp.int32, (L, 2 * hq), 1) < hq
    stbd = ((lax.broadcasted_iota(jnp.int32, (2 * hv, 2 * hq), 0) < hv)
            == (lax.broadcasted_iota(jnp.int32, (2 * hv, 2 * hq), 1) < hq))

    def rows_bcast(slot, first, period):
        return jnp.concatenate([jnp.broadcast_to(b_scr[slot, r:r + 1, :], (period, wq))
                                for r in range(first, L, period)], axis=0)

    def midpoint(slot, s):
        if s >= 4:
            return rows_bcast(slot, s - 1, 2 * s)
        if s == 2:
            return jnp.where(sub < 4, rows_bcast(slot, 1, 8), rows_bcast(slot, 5, 8))
        return jnp.where(sub < 2, rows_bcast(slot, 0, 8),
                         jnp.where(sub < 4, rows_bcast(slot, 2, 8),
                                   jnp.where(sub < 6, rows_bcast(slot, 4, 8), rows_bcast(slot, 6, 8))))

    def pair_scores(qx, kx, p):
        pl_ = slice(p * 2 * hq, (p + 1) * 2 * hq)
        kp = kx[:, pl_]
        kk = jnp.concatenate([jnp.where(even_head, kp, 0.0), jnp.where(even_head, 0.0, kp)], axis=0)
        return _dot_nt(qx[:, pl_].astype(BF16), kk.astype(BF16))

    def chunk(c, slot):
        r0 = pl.multiple_of(c * L, L)
        q = q_ref[pl.ds(r0, L), :]
        k = k_ref[pl.ds(r0, L), :]
        v = v_ref[pl.ds(r0, L), :]
        bb = _dot(tri_scr[...], g_ref[pl.ds(r0, L), :])
        b = bb[:, :wq] + bb[:, wq:]
        b_scr[slot] = b
        b_last = b[L - 1:L, :]
        q_in = (q * jnp.exp2(b)).astype(BF16)
        k_out = (k * jnp.exp2(b_last - b)).astype(BF16)
        dec = jnp.exp2(b_last)

        att = [jnp.where(lev_scr[...] == 0, pair_scores(q, k, p), 0.0) for p in range(pairs)]
        for li, s in enumerate(levels):
            d = lax.bitcast_convert_type(b - midpoint(slot, s), jnp.uint32) | jnp.uint32(0x80000000)
            w = jnp.exp2(lax.bitcast_convert_type(d, F32))
            hit = lev_scr[...] == li + 1
            att = [jnp.where(hit, pair_scores(q * w, k * w, p), att[p]) for p in range(pairs)]

        for p in range(pairs):
            ql = slice(p * 2 * hq, (p + 1) * 2 * hq)
            vl = slice(p * 2 * hv, (p + 1) * 2 * hv)
            st = st_scr[p]
            vp = v[:, vl]
            ab = att[p].astype(BF16)
            o = _dot_nt(q_in[:, ql], st.astype(BF16)) + jnp.concatenate(
                [_dot(ab[:, :L], vp[:, :hv]), _dot(ab[:, L:], vp[:, hv:])], axis=1)
            st_scr[p] = st * dec[:, ql] + jnp.where(stbd, _dot_tn(vp, k_out[:, ql]), 0.0)
            for hh in range(2):
                sl = slice(vl.start + hh * hv, vl.start + (hh + 1) * hv)
                gate = og_ref[pl.ds(r0, L), sl]
                oh = _rmsnorm(o[:, hh * hv:(hh + 1) * hv], nw_ref[...])
                o_ref[pl.ds(r0, L), sl] = (oh * (gate * _sigmoid(gate))).astype(BF16)

    def chunk_pair(i, carry):
        chunk(2 * i, 0)
        chunk(2 * i + 1, 1)
        return carry

    lax.fori_loop(0, q_ref.shape[0] // (2 * L), chunk_pair, 0)


def _gla(gq, gk, gv, gog, loga, gla_norm, bsz, seq):
    t = gq.shape[0]
    step = min(GLA_STEP, seq)
    spb = seq // step
    row = lambda b, i: (b * spb + i, 0)
    hv = gv.shape[1] // GLA_HEADS
    hq = gq.shape[1] // GLA_HEADS
    return pl.pallas_call(
        _gla_kernel,
        grid=(bsz, spb),
        in_specs=[pl.BlockSpec((step, gq.shape[1]), row), pl.BlockSpec((step, gk.shape[1]), row),
                  pl.BlockSpec((step, gv.shape[1]), row), pl.BlockSpec((step, gog.shape[1]), row),
                  pl.BlockSpec((step, loga.shape[1]), row),
                  pl.BlockSpec((1, hv), lambda b, i: (0, 0))],
        out_specs=pl.BlockSpec((step, gv.shape[1]), row),
        out_shape=jax.ShapeDtypeStruct((t, gv.shape[1]), BF16),
        scratch_shapes=[pltpu.VMEM((GLA_HEADS // 2, 2 * hv, 2 * hq), F32),
                        pltpu.VMEM((GLA_CHUNK, GLA_CHUNK), BF16),
                        pltpu.VMEM((GLA_CHUNK, 2 * GLA_CHUNK), jnp.int32),
                        pltpu.VMEM((2, GLA_CHUNK, gq.shape[1]), F32)],
        compiler_params=_params(("arbitrary", "arbitrary")),
        name="gla",
    )(gq, gk, gv, gog, loga, gla_norm)


def _rel_buckets(max_dist):
    n = np.arange(max_dist)
    nf = np.maximum(n, 1).astype(np.float64)
    large = MAX_EXACT + (np.log(nf / MAX_EXACT) / math.log(MAX_DISTANCE / MAX_EXACT)
                         * (NUM_BUCKETS - MAX_EXACT)).astype(np.int64)
    return np.where(n < MAX_EXACT, n, np.minimum(large, NUM_BUCKETS - 1))


_TBL_OWN, _TBL_PREV, _TBL_FAR = 0, 1, 2


def _bias_kernel(rb_ref, o_ref):
    h = pl.program_id(0)
    blk = MOBA_BLOCK
    buckets = _rel_buckets(2 * blk)
    assert (np.diff(buckets) >= 0).all()
    starts = {b: int(np.argmax(buckets == b)) for b in range(NUM_BUCKETS) if (buckets == b).any()}
    ki = lax.broadcasted_iota(jnp.int32, (blk, blk), 0)
    qi = lax.broadcasted_iota(jnp.int32, (blk, blk), 1)
    far = rb_ref[h, NUM_BUCKETS - 1]
    for tbl, base in ((_TBL_OWN, 0), (_TBL_PREV, blk)):
        dist = qi - ki + base
        val = jnp.full((blk, blk), rb_ref[h, 0], F32)
        for b in sorted(starts):
            if b > 0:
                val = jnp.where(dist >= starts[b], rb_ref[h, b], val)
        val = (val - far) * LOG2E
        if tbl == _TBL_OWN:
            val = jnp.where(dist >= 0, val, NEG)
        o_ref[0, tbl] = val
    o_ref[0, _TBL_FAR] = jnp.zeros((blk, blk), F32)


def _bias_tables(rel_bias):
    nh = rel_bias.shape[0]
    return pl.pallas_call(
        _bias_kernel,
        grid=(nh,),
        in_specs=[pl.BlockSpec(memory_space=pltpu.SMEM)],
        out_specs=pl.BlockSpec((1, 3, MOBA_BLOCK, MOBA_BLOCK), lambda h: (h, 0, 0, 0)),
        out_shape=jax.ShapeDtypeStruct((nh, 3, MOBA_BLOCK, MOBA_BLOCK), F32),
        compiler_params=_params(("arbitrary",)),
        name="moba_bias",
    )(rel_bias)


def _moba_kernel(q_ref, k_ref, vt_ref, km_ref, bias_ref, o_ref,
                 off_scr, m_scr, al_scr, acc_scr, p_scr, s_scr):
    n = pl.program_id(1)
    blk = MOBA_BLOCK
    nb = km_ref.shape[1]
    heads = range(MOBA_HEADS)
    hsl = [slice(h * MOBA_DH, (h + 1) * MOBA_DH) for h in heads]

    brow = lax.broadcasted_iota(jnp.int32, (nb, MOBA_HEADS * blk), 0).astype(F32)
    gate = jnp.concatenate([_dot_nt(km_ref[0, :, hsl[h]].astype(BF16), q_ref[:, hsl[h]]) for h in heads],
                           axis=1)
    gate = jnp.where(brow < n.astype(F32), gate, NEG)
    off = jnp.full(gate.shape, NEG, F32)
    for _ in range(MOBA_TOPK):
        best = jnp.max(gate, axis=0, keepdims=True)
        first = jnp.min(jnp.where(gate == best, brow, float(nb)), axis=0, keepdims=True)
        pick = (brow == first) & (best > NEG)
        off = jnp.where(pick, 0.0, off)
        gate = jnp.where(pick, NEG, gate)
    for h in heads:
        off_scr[h] = off[:, h * blk:(h + 1) * blk]
        m_scr[h] = jnp.full((1, blk), NEG, F32)
        acc_scr[h] = jnp.zeros((MOBA_VROWS, blk), F32)
        al_scr[1, h] = jnp.ones((1, blk), F32)
        p_scr[1, h] = jnp.zeros((blk, blk), BF16)

    def qk(h, j, slot):
        r0 = pl.multiple_of(j * blk, blk)
        s_scr[slot, h] = _dot_nt(k_ref[pl.ds(r0, blk), hsl[h]], q_ref[:, hsl[h]])

    def pv(h, j, slot):
        acc_scr[h] = al_scr[slot, h] * acc_scr[h] + _dot(vt_ref[0, h, j], p_scr[slot, h])

    def step(slot, j, j_next, j_prev, table, live):
        for h in heads:
            pv(h, j_prev, 1 - slot)
            if j_next is not None:
                qk(h, j_next, 1 - slot)
        for h in heads:
            sh = s_scr[slot, h]
            if table is not None:
                sh = sh + bias_ref[h, table]
            top = jnp.max(sh, axis=0, keepdims=True)
            m_old = m_scr[h]
            if live is not None:
                off = jnp.where(live, off_scr[h, pl.ds(j, 1), :], NEG)
                m_new = jnp.maximum(m_old, top + off)
                m_sub = jnp.where(off < 0.0, -NEG, m_new)
            else:
                m_new = jnp.maximum(m_old, top)
                m_sub = m_new
            m_scr[h] = m_new
            al_scr[slot, h] = jnp.exp2(m_old - m_new)
            p_scr[slot, h] = jnp.exp2(sh - m_sub).astype(BF16)

    n_far = jnp.maximum(n - 1, 0)
    last_far = jnp.maximum(n_far - 1, 0)
    j_before = jnp.maximum(n - 1, 0)
    far_blk = lambda u: jnp.clip(u, 0, last_far)

    for h in heads:
        qk(h, jnp.where(n_far > 0, 0, j_before), 0)

    def far_pair(i, carry):
        u = 2 * i
        step(0, far_blk(u), far_blk(u + 1), far_blk(u - 1), None, u < n_far)
        nxt = jnp.where(u + 2 < n_far, far_blk(u + 2), j_before)
        step(1, far_blk(u + 1), nxt, far_blk(u), None, u + 1 < n_far)
        return carry

    lax.fori_loop(0, (n_far + 1) // 2, far_pair, 0)
    step(0, j_before, n, last_far, _TBL_PREV, n >= 1)
    step(1, n, None, j_before, _TBL_OWN, None)
    for h in heads:
        pv(h, n, 1)
        acc = acc_scr[h]
        o_ref[:, hsl[h]] = (acc[:MOBA_DH] / acc[MOBA_DH:MOBA_DH + 1]).T.astype(BF16)


def _moba(mq, mk, mvt, kmean, bias, bsz, seq):
    t, w = mq.shape
    blk = MOBA_BLOCK
    nb = seq // blk
    return pl.pallas_call(
        _moba_kernel,
        grid=(bsz, nb),
        in_specs=[pl.BlockSpec((blk, w), lambda b, n: (b * nb + n, 0)),
                  pl.BlockSpec((seq, w), lambda b, n: (b, 0)),
                  pl.BlockSpec((1, MOBA_HEADS, nb, MOBA_VROWS, blk), lambda b, n: (b, 0, 0, 0, 0)),
                  pl.BlockSpec((1, nb, w), lambda b, n: (b, 0, 0)),
                  _const_spec(bias.shape)],
        out_specs=pl.BlockSpec((blk, w), lambda b, n: (b * nb + n, 0)),
        out_shape=jax.ShapeDtypeStruct((t, w), BF16),
        scratch_shapes=[pltpu.VMEM((MOBA_HEADS, nb, blk), F32),
                        pltpu.VMEM((MOBA_HEADS, 1, blk), F32),
                        pltpu.VMEM((2, MOBA_HEADS, 1, blk), F32),
                        pltpu.VMEM((MOBA_HEADS, MOBA_VROWS, blk), F32),
                        pltpu.VMEM((2, MOBA_HEADS, blk, blk), BF16),
                        pltpu.VMEM((2, MOBA_HEADS, blk, blk), F32)],
        compiler_params=_params(("arbitrary", "arbitrary")),
        name="moba",
    )(mq, mk, mvt, kmean, bias)


def _merge_kernel(h_ref, mod_ref, nw_ref, oa_ref, ob_ref, wga_ref, wgb_ref, wa_ref, wb_ref, wo_ref,
                  o_ref, u_scr):
    mod = mod_ref[0, 0]
    u_scr[...] = _norm_mod(h_ref[...], nw_ref[...], mod).astype(BF16)
    ya = _dot(oa_ref[...], wa_ref[...])
    yb = _dot(ob_ref[...], wb_ref[...])
    merged = _sigmoid(_dot(u_scr[...], wga_ref[...])) * ya + _sigmoid(_dot(u_scr[...], wgb_ref[...])) * yb
    o_ref[...] = h_ref[...] + mod[2:3, :] * _dot(merged.astype(BF16), wo_ref[...])


def _merge(h, mod4, norm_w, oa, ob, wga, wgb, wa, wb, wo, tiles_per_batch):
    t, d = h.shape
    tm = TOKEN_TILE
    row = lambda i: (i, 0)
    return pl.pallas_call(
        _merge_kernel,
        grid=(t // tm,),
        in_specs=[pl.BlockSpec((tm, d), row),
                  pl.BlockSpec((1, 1, 3, d), lambda i: (i // tiles_per_batch, 1, 0, 0)),
                  _const_spec((1, d)),
                  pl.BlockSpec((tm, oa.shape[1]), row), pl.BlockSpec((tm, ob.shape[1]), row),
                  _const_spec(wga.shape), _const_spec(wgb.shape), _const_spec(wa.shape),
                  _const_spec(wb.shape), _const_spec(wo.shape)],
        out_specs=pl.BlockSpec((tm, d), row),
        out_shape=jax.ShapeDtypeStruct((t, d), F32),
        scratch_shapes=[pltpu.VMEM((tm, d), BF16)],
        compiler_params=_params(("arbitrary",)),
        name="merge",
    )(h, mod4, norm_w, oa, ob, wga, wgb, wa, wb, wo)


def kernel(x, c, w_ada, b_ada, norm_ff1, w_ff1_gate, w_ff1_up, w_ff1_down, norm_mix, w_in, w_gla_lr,
           b_gla_lr, gla_norm, rel_bias, w_br_gla, w_br_moba, w_out, norm_ff2, w_ff2_gate, w_ff2_up,
           w_ff2_down, norm_final):
    bsz, seq, d = x.shape
    depth = w_ada.shape[0]
    t = bsz * seq
    tiles_per_batch = seq // TOKEN_TILE
    qk_w = GLA_HEADS * (d // 16)
    v_w = GLA_HEADS * (d // 8)
    m_w = MOBA_HEADS * MOBA_DH
    offs = np.cumsum([0, qk_w, qk_w, v_w, GLA_LOWRANK, v_w, m_w, m_w, m_w, d, d])
    bias = _bias_tables(rel_bias)
    h = x.reshape(t, d)
    c_act_in = c
    for l in range(depth):
        mod4 = _adaln(c_act_in, w_ada, b_ada, l).reshape(bsz, 3, 3, d)
        cast = lambda w: w.astype(BF16)
        h = _ffn(h, mod4, 0, norm_ff1[l].reshape(1, d), cast(w_ff1_gate[l]), cast(w_ff1_up[l]),
                 cast(w_ff1_down[l]), tiles_per_batch)
        wi = w_in[l]
        seg = lambda i: wi[:, offs[i]:offs[i + 1]]
        lr_pad = jnp.pad(seg(3), ((0, 0), (0, 128 - GLA_LOWRANK)))
        wp = cast(jnp.concatenate([seg(0), seg(1), seg(2), seg(4), seg(5), seg(6), lr_pad], axis=1))
        wvt = cast(seg(7).T)
        wlr = cast(jnp.pad(w_gla_lr[l], ((0, 128 - GLA_LOWRANK), (0, 0))))
        gq, gk, gv, gog, loga, mq, mk, mvt, kmean = _inproj(
            h, mod4, norm_mix[l].reshape(1, d), wp, wvt, wlr, b_gla_lr[l].reshape(1, qk_w), bsz, seq)
        o_a = _gla(gq, gk, gv, gog, loga, gla_norm[l].reshape(1, -1), bsz, seq)
        o_b = _moba(mq, mk, mvt, kmean.reshape(bsz, seq // MOBA_BLOCK, m_w), bias, bsz, seq)
        h = _merge(h, mod4, norm_mix[l].reshape(1, d), o_a, o_b, cast(seg(8)), cast(seg(9)),
                   cast(w_br_gla[l]), cast(w_br_moba[l]), cast(w_out[l]), tiles_per_batch)
        last = l == depth - 1
        h = _ffn(h, mod4, 2, norm_ff2[l].reshape(1, d), cast(w_ff2_gate[l]), cast(w_ff2_up[l]),
                 cast(w_ff2_down[l]), tiles_per_batch, norm_final.reshape(1, d) if last else None)
    return h.reshape(bsz, seq, d)
```

```python
import functools
import math

import numpy as np
import jax
import jax.numpy as jnp
from jax import lax
from jax.experimental import pallas as pl
from jax.experimental.pallas import tpu as pltpu

F32 = jnp.float32
BF16 = jnp.bfloat16

EPS = 1e-6
FFN_RES = 0.5
GLA_HEADS = 4
GLA_TAU = 16.0
GLA_LOWRANK = 16
GLA_CHUNK = 128
GLA_STEP = 1024
GLA_UNROLL = 4
MOBA_HEADS = 4
MOBA_DH = 128
MOBA_BLOCK = 256
MOBA_TOPK = 3
MOBA_VROWS = MOBA_DH + 16
MOBA_ROWS = 2
NUM_BUCKETS = 32
MAX_DISTANCE = 128
MAX_EXACT = NUM_BUCKETS // 2
NEG = -1e30
LOG2E = 1.4426950408889634
MOBA_QSCALE = LOG2E * MOBA_DH ** -0.5
TOKEN_TILE = 1024
INPROJ_TILE = 512
FF_CHUNK = 256
VMEM_LIMIT = 56 * 1024 * 1024


def _dot(a, b):
    return jnp.dot(a, b, preferred_element_type=F32)


def _dot_nt(a, b):
    return lax.dot_general(a, b, (((1,), (1,)), ((), ())), preferred_element_type=F32)


def _dot_tn(a, b):
    return lax.dot_general(a, b, (((0,), (0,)), ((), ())), preferred_element_type=F32)


def _sigmoid(x):
    return 1.0 / (1.0 + jnp.exp(-x))


def _rmsnorm(x, w):
    return x * lax.rsqrt(jnp.mean(x * x, axis=-1, keepdims=True) + EPS) * w


def _norm_mod(x, w, mod):
    return _rmsnorm(x, w) * (1.0 + mod[1:2, :]) + mod[0:1, :]


def _params(sem, flags=None):
    return pltpu.CompilerParams(dimension_semantics=sem, vmem_limit_bytes=VMEM_LIMIT, flags=flags)


def _const_spec(shape):
    nd = len(shape)
    return pl.BlockSpec(shape, lambda *_: (0,) * nd, pipeline_mode=pl.Buffered(1))


def _adaln_kernel(c_ref, w_ref, b_ref, o_ref):
    c = c_ref[...]
    ca = (c * _sigmoid(c)).astype(BF16)
    o_ref[...] = _dot(ca, w_ref[...].astype(BF16)) + b_ref[...]


def _adaln(c, w_ada, b_ada, layer):
    bsz, d = c.shape
    depth, _, n = w_ada.shape
    return pl.pallas_call(
        _adaln_kernel,
        grid=(n // d,),
        in_specs=[pl.BlockSpec((bsz, d), lambda j: (0, 0)),
                  pl.BlockSpec((None, d, d), lambda j: (layer, 0, j)),
                  pl.BlockSpec((None, 1, d), lambda j: (layer, 0, j))],
        out_specs=pl.BlockSpec((bsz, d), lambda j: (0, j)),
        out_shape=jax.ShapeDtypeStruct((bsz, n), F32),
        compiler_params=_params(("arbitrary",)),
        name="adaln",
    )(c, w_ada, b_ada.reshape(depth, 1, n))


def _ffn_kernel(x_ref, mod_ref, nw_ref, wg_ref, wu_ref, wd_ref, *rest, final_norm):
    if final_norm:
        nf_ref, o_ref, u_scr, a_scr = rest
    else:
        o_ref, u_scr, a_scr = rest
    mod = mod_ref[0, 0]
    u_scr[...] = _norm_mod(x_ref[...], nw_ref[...], mod).astype(BF16)
    d_ff = wg_ref.shape[1]
    for c in range(d_ff // FF_CHUNK):
        sl = slice(c * FF_CHUNK, (c + 1) * FF_CHUNK)
        g = _dot(u_scr[...], wg_ref[:, sl])
        up = _dot(u_scr[...], wu_ref[:, sl])
        a_scr[:, sl] = (g * _sigmoid(g) * up).astype(BF16)
    y = _dot(a_scr[...], wd_ref[...])
    out = x_ref[...] + (FFN_RES * mod[2:3, :]) * y
    if final_norm:
        out = _rmsnorm(out, nf_ref[...])
    o_ref[...] = out


def _ffn(h, mod4, layer, norm_w, wg, wu, wd, tiles_per_batch, norm_final=None):
    t, d = h.shape
    d_ff = wg.shape[1]
    tm = TOKEN_TILE
    in_specs = [pl.BlockSpec((tm, d), lambda i: (i, 0)),
                pl.BlockSpec((1, 1, 3, d), lambda i: (i // tiles_per_batch, layer, 0, 0)),
                _const_spec((1, d)), _const_spec((d, d_ff)), _const_spec((d, d_ff)), _const_spec((d_ff, d))]
    args = [h, mod4, norm_w, wg, wu, wd]
    if norm_final is not None:
        in_specs.append(_const_spec((1, d)))
        args.append(norm_final)
    return pl.pallas_call(
        functools.partial(_ffn_kernel, final_norm=norm_final is not None),
        grid=(t // tm,),
        in_specs=in_specs,
        out_specs=pl.BlockSpec((tm, d), lambda i: (i, 0)),
        out_shape=jax.ShapeDtypeStruct((t, d), F32),
        scratch_shapes=[pltpu.VMEM((tm, d), BF16), pltpu.VMEM((tm, d_ff), BF16)],
        compiler_params=_params(("arbitrary",)),
        name="ffn_final" if norm_final is not None else "ffn",
    )(*args)


_P_GQ, _P_GK, _P_GV, _P_GOG, _P_MQ, _P_MK, _P_LR, _P_END = 0, 256, 512, 1024, 1536, 2048, 2560, 2688


def _inproj_kernel(h_ref, mod_ref, nw_ref, wp_ref, wvt_ref, wlr_ref, blr_ref,
                   gq_ref, gk_ref, gv_ref, gog_ref, loga_ref, mq_ref, mk_ref, mvt_ref, kmean_ref, u_scr):
    u_scr[...] = _norm_mod(h_ref[...], nw_ref[...], mod_ref[0, 0]).astype(BF16)

    def proj(lo, hi):
        return _dot(u_scr[...], wp_ref[:, lo:hi])

    gq_ref[...] = proj(_P_GQ, _P_GK) * 0.125
    gk_ref[...] = proj(_P_GK, _P_GV)
    gv_ref[...] = proj(_P_GV, _P_GOG).astype(BF16)
    gog_ref[...] = proj(_P_GOG, _P_MQ)
    mq_ref[...] = (proj(_P_MQ, _P_MK) * MOBA_QSCALE).astype(BF16)
    mk = proj(_P_MK, _P_LR)
    mk_ref[...] = mk.astype(BF16)
    nblk = mk.shape[0] // MOBA_BLOCK
    kmean_ref[0] = jnp.mean(mk.reshape(nblk, MOBA_BLOCK, mk.shape[1]), axis=1)
    glr = proj(_P_LR, _P_END).astype(BF16)
    z = _dot(glr, wlr_ref[...]) + blr_ref[...]
    g = (jnp.minimum(z, 0.0) - jnp.log1p(jnp.exp(-jnp.abs(z)))) * (LOG2E / GLA_TAU)
    g_hi = g.astype(BF16)
    loga_ref[:, :g.shape[1]] = g_hi
    loga_ref[:, g.shape[1]:] = (g - g_hi.astype(F32)).astype(BF16)
    mvt = _dot_nt(wvt_ref[...], u_scr[...])
    ones = jnp.ones((MOBA_VROWS - MOBA_DH, MOBA_BLOCK), BF16)
    for h in range(MOBA_HEADS):
        for j in range(nblk):
            mvt_ref[0, h, j, :MOBA_DH, :] = mvt[h * MOBA_DH:(h + 1) * MOBA_DH,
                                                j * MOBA_BLOCK:(j + 1) * MOBA_BLOCK].astype(BF16)
            mvt_ref[0, h, j, MOBA_DH:, :] = ones


def _inproj(h, mod4, norm_w, wp, wvt, wlr, blr, bsz, seq):
    t, d = h.shape
    tm = INPROJ_TILE
    tiles_per_batch = seq // tm
    nblk = tm // MOBA_BLOCK
    nb = seq // MOBA_BLOCK
    row = lambda i: (i, 0)
    out_shape = [jax.ShapeDtypeStruct((t, 256), F32), jax.ShapeDtypeStruct((t, 256), F32),
                 jax.ShapeDtypeStruct((t, 512), BF16), jax.ShapeDtypeStruct((t, 512), F32),
                 jax.ShapeDtypeStruct((t, 512), BF16), jax.ShapeDtypeStruct((t, 512), BF16),
                 jax.ShapeDtypeStruct((t, 512), BF16),
                 jax.ShapeDtypeStruct((bsz, MOBA_HEADS, nb, MOBA_VROWS, MOBA_BLOCK), BF16),
                 jax.ShapeDtypeStruct((t // tm, nblk, 512), F32)]
    out_specs = [pl.BlockSpec((tm, 256), row), pl.BlockSpec((tm, 256), row),
                 pl.BlockSpec((tm, 512), row), pl.BlockSpec((tm, 512), row),
                 pl.BlockSpec((tm, 512), row), pl.BlockSpec((tm, 512), row),
                 pl.BlockSpec((tm, 512), row),
                 pl.BlockSpec((1, MOBA_HEADS, nblk, MOBA_VROWS, MOBA_BLOCK),
                              lambda i: (i // tiles_per_batch, 0, i % tiles_per_batch, 0, 0)),
                 pl.BlockSpec((1, nblk, 512), lambda i: (i, 0, 0))]
    return pl.pallas_call(
        _inproj_kernel,
        grid=(t // tm,),
        in_specs=[pl.BlockSpec((tm, d), row),
                  pl.BlockSpec((1, 1, 3, d), lambda i: (i // tiles_per_batch, 1, 0, 0)),
                  _const_spec((1, d)), _const_spec(wp.shape), _const_spec(wvt.shape),
                  _const_spec(wlr.shape), _const_spec(blr.shape)],
        out_specs=out_specs,
        out_shape=out_shape,
        scratch_shapes=[pltpu.VMEM((tm, d), BF16)],
        compiler_params=_params(("arbitrary",)),
        name="inproj",
    )(h, mod4, norm_w, wp, wvt, wlr, blr)


def _gla_kernel(q_ref, k_ref, v_ref, og_ref, g_ref, nw_ref, o_ref, st_scr, tri_scr, lev_scr, b_scr):
    L = GLA_CHUNK
    hq = q_ref.shape[1] // GLA_HEADS
    hv = v_ref.shape[1] // GLA_HEADS
    wq = q_ref.shape[1]
    pairs = GLA_HEADS // 2
    assert 2 * hq == 128 and hv == 128 and L == 128
    levels = [1 << p for p in range(int(math.log2(L)))]

    @pl.when(pl.program_id(1) == 0)
    def _():
        st_scr[...] = jnp.zeros_like(st_scr)
        tri_scr[...] = (lax.broadcasted_iota(jnp.int32, (L, L), 1)
                        <= lax.broadcasted_iota(jnp.int32, (L, L), 0)).astype(BF16)
        ti = lax.broadcasted_iota(jnp.int32, (L, 2 * L), 0)
        tj = lax.broadcasted_iota(jnp.int32, (L, 2 * L), 1) & (L - 1)
        xor = ti ^ tj
        lev = jnp.where(tj > ti, -1, 0)
        for p in range(len(levels)):
            lev = jnp.where((xor >= (1 << p)) & (tj < ti), p + 1, lev)
        lev_scr[...] = lev

    sub = lax.broadcasted_iota(jnp.int32, (L, wq), 0) & 7
    even_head = lax.broadcasted_iota(jnp.int32, (L, 2 * hq), 1) < hq
    stbd = ((lax.broadcasted_iota(jnp.int32, (2 * hv, 2 * hq), 0) < hv)
            == (lax.broadcasted_iota(jnp.int32, (2 * hv, 2 * hq), 1) < hq))

    def rows_bcast(slot, first, period):
        return jnp.concatenate([jnp.broadcast_to(b_scr[slot, r:r + 1, :], (period, wq))
                                for r in range(first, L, period)], axis=0)

    def midpoint(slot, s):
        if s >= 4:
            return rows_bcast(slot, s - 1, 2 * s)
        if s == 2:
            return jnp.where(sub < 4, rows_bcast(slot, 1, 8), rows_bcast(slot, 5, 8))
        return jnp.where(sub < 2, rows_bcast(slot, 0, 8),
                         jnp.where(sub < 4, rows_bcast(slot, 2, 8),
                                   jnp.where(sub < 6, rows_bcast(slot, 4, 8), rows_bcast(slot, 6, 8))))

    def pair_scores(qx, kx, p):
        pl_ = slice(p * 2 * hq, (p + 1) * 2 * hq)
        kp = kx[:, pl_]
        kk = jnp.concatenate([jnp.where(even_head, kp, 0.0), jnp.where(even_head, 0.0, kp)], axis=0)
        return _dot_nt(qx[:, pl_].astype(BF16), kk.astype(BF16))

    def chunk(c, slot):
        r0 = pl.multiple_of(c * L, L)
        q = q_ref[pl.ds(r0, L), :]
        k = k_ref[pl.ds(r0, L), :]
        v = v_ref[pl.ds(r0, L), :]
        bb = _dot(tri_scr[...], g_ref[pl.ds(r0, L), :])
        b = bb[:, :wq] + bb[:, wq:]
        b_scr[slot] = b
        b_last = b[L - 1:L, :]
        q_in = (q * jnp.exp2(b)).astype(BF16)
        k_out = (k * jnp.exp2(b_last - b)).astype(BF16)
        dec = jnp.exp2(b_last)

        att = [jnp.where(lev_scr[...] == 0, pair_scores(q, k, p), 0.0) for p in range(pairs)]
        for li, s in enumerate(levels):
            d = lax.bitcast_convert_type(b - midpoint(slot, s), jnp.uint32) | jnp.uint32(0x80000000)
            w = jnp.exp2(lax.bitcast_convert_type(d, F32))
            hit = lev_scr[...] == li + 1
            att = [jnp.where(hit, pair_scores(q * w, k * w, p), att[p]) for p in range(pairs)]

        for p in range(pairs):
            ql = slice(p * 2 * hq, (p + 1) * 2 * hq)
            vl = slice(p * 2 * hv, (p + 1) * 2 * hv)
            st = st_scr[p]
            vp = v[:, vl]
            ab = att[p].astype(BF16)
            o = _dot_nt(q_in[:, ql], st.astype(BF16)) + jnp.concatenate(
                [_dot(ab[:, :L], vp[:, :hv]), _dot(ab[:, L:], vp[:, hv:])], axis=1)
            st_scr[p] = st * dec[:, ql] + jnp.where(stbd, _dot_tn(vp, k_out[:, ql]), 0.0)
            for hh in range(2):
                sl = slice(vl.start + hh * hv, vl.start + (hh + 1) * hv)
                gate = og_ref[pl.ds(r0, L), sl]
                oh = _rmsnorm(o[:, hh * hv:(hh + 1) * hv], nw_ref[...])
                o_ref[pl.ds(r0, L), sl] = (oh * (gate * _sigmoid(gate))).astype(BF16)

    def chunk_group(i, carry):
        for u in range(GLA_UNROLL):
            chunk(GLA_UNROLL * i + u, u)
        return carry

    lax.fori_loop(0, q_ref.shape[0] // (GLA_UNROLL * L), chunk_group, 0)


def _gla(gq, gk, gv, gog, loga, gla_norm, bsz, seq):
    t = gq.shape[0]
    step = min(GLA_STEP, seq)
    spb = seq // step
    row = lambda b, i: (b * spb + i, 0)
    hv = gv.shape[1] // GLA_HEADS
    hq = gq.shape[1] // GLA_HEADS
    return pl.pallas_call(
        _gla_kernel,
        grid=(bsz, spb),
        in_specs=[pl.BlockSpec((step, gq.shape[1]), row), pl.BlockSpec((step, gk.shape[1]), row),
                  pl.BlockSpec((step, gv.shape[1]), row), pl.BlockSpec((step, gog.shape[1]), row),
                  pl.BlockSpec((step, loga.shape[1]), row),
                  pl.BlockSpec((1, hv), lambda b, i: (0, 0))],
        out_specs=pl.BlockSpec((step, gv.shape[1]), row),
        out_shape=jax.ShapeDtypeStruct((t, gv.shape[1]), BF16),
        scratch_shapes=[pltpu.VMEM((GLA_HEADS // 2, 2 * hv, 2 * hq), F32),
                        pltpu.VMEM((GLA_CHUNK, GLA_CHUNK), BF16),
                        pltpu.VMEM((GLA_CHUNK, 2 * GLA_CHUNK), jnp.int32),
                        pltpu.VMEM((GLA_UNROLL, GLA_CHUNK, gq.shape[1]), F32)],
        compiler_params=_params(("arbitrary", "arbitrary")),
        name="gla",
    )(gq, gk, gv, gog, loga, gla_norm)


def _rel_buckets(max_dist):
    n = np.arange(max_dist)
    nf = np.maximum(n, 1).astype(np.float64)
    large = MAX_EXACT + (np.log(nf / MAX_EXACT) / math.log(MAX_DISTANCE / MAX_EXACT)
                         * (NUM_BUCKETS - MAX_EXACT)).astype(np.int64)
    return np.where(n < MAX_EXACT, n, np.minimum(large, NUM_BUCKETS - 1))


_TBL_OWN, _TBL_PREV, _TBL_FAR = 0, 1, 2


def _bias_kernel(rb_ref, o_ref):
    h = pl.program_id(0)
    blk = MOBA_BLOCK
    buckets = _rel_buckets(2 * blk)
    assert (np.diff(buckets) >= 0).all()
    starts = {b: int(np.argmax(buckets == b)) for b in range(NUM_BUCKETS) if (buckets == b).any()}
    ki = lax.broadcasted_iota(jnp.int32, (blk, blk), 0)
    qi = lax.broadcasted_iota(jnp.int32, (blk, blk), 1)
    far = rb_ref[h, NUM_BUCKETS - 1]
    for tbl, base in ((_TBL_OWN, 0), (_TBL_PREV, blk)):
        dist = qi - ki + base
        val = jnp.full((blk, blk), rb_ref[h, 0], F32)
        for b in sorted(starts):
            if b > 0:
                val = jnp.where(dist >= starts[b], rb_ref[h, b], val)
        val = (val - far) * LOG2E
        if tbl == _TBL_OWN:
            val = jnp.where(dist >= 0, val, NEG)
        o_ref[0, tbl] = val
    o_ref[0, _TBL_FAR] = jnp.zeros((blk, blk), F32)


def _bias_tables(rel_bias):
    nh = rel_bias.shape[0]
    return pl.pallas_call(
        _bias_kernel,
        grid=(nh,),
        in_specs=[pl.BlockSpec(memory_space=pltpu.SMEM)],
        out_specs=pl.BlockSpec((1, 3, MOBA_BLOCK, MOBA_BLOCK), lambda h: (h, 0, 0, 0)),
        out_shape=jax.ShapeDtypeStruct((nh, 3, MOBA_BLOCK, MOBA_BLOCK), F32),
        compiler_params=_params(("arbitrary",)),
        name="moba_bias",
    )(rel_bias)


def _moba_kernel(q_ref, k_ref, vt_ref, km_ref, bias_ref, o_ref,
                 off_scr, m_scr, al_scr, acc_scr, p_scr, s_scr):
    n = pl.program_id(1)
    blk = MOBA_BLOCK
    nb = km_ref.shape[1]
    heads = range(q_ref.shape[0] * MOBA_HEADS)
    row = [h // MOBA_HEADS for h in heads]
    hsl = [slice((h % MOBA_HEADS) * MOBA_DH, (h % MOBA_HEADS + 1) * MOBA_DH) for h in heads]

    brow = lax.broadcasted_iota(jnp.int32, (nb, len(heads) * blk), 0).astype(F32)
    gate = jnp.concatenate([_dot_nt(km_ref[row[h], :, hsl[h]].astype(BF16), q_ref[row[h], :, hsl[h]])
                            for h in heads], axis=1)
    gate = jnp.where(brow < n.astype(F32), gate, NEG)
    off = jnp.full(gate.shape, NEG, F32)
    for _ in range(MOBA_TOPK):
        best = jnp.max(gate, axis=0, keepdims=True)
        first = jnp.min(jnp.where(gate == best, brow, float(nb)), axis=0, keepdims=True)
        pick = (brow == first) & (best > NEG)
        off = jnp.where(pick, 0.0, off)
        gate = jnp.where(pick, NEG, gate)
    for h in heads:
        off_scr[h] = off[:, h * blk:(h + 1) * blk]
        m_scr[h] = jnp.full((1, blk), NEG, F32)
        acc_scr[h] = jnp.zeros((MOBA_VROWS, blk), F32)
        al_scr[1, h] = jnp.ones((1, blk), F32)
        p_scr[1, h] = jnp.zeros((blk, blk), BF16)

    def qk(h, j, slot):
        r0 = pl.multiple_of(j * blk, blk)
        s_scr[slot, h] = _dot_nt(k_ref[row[h], pl.ds(r0, blk), hsl[h]], q_ref[row[h], :, hsl[h]])

    def pv(h, j, slot):
        acc_scr[h] = al_scr[slot, h] * acc_scr[h] + _dot(vt_ref[row[h], h % MOBA_HEADS, j], p_scr[slot, h])

    def step(slot, j, j_next, j_prev, table, live):
        for h in heads:
            pv(h, j_prev, 1 - slot)
            if j_next is not None:
                qk(h, j_next, 1 - slot)
        for h in heads:
            sh = s_scr[slot, h]
            if table is not None:
                sh = sh + bias_ref[h % MOBA_HEADS, table]
            top = jnp.max(sh, axis=0, keepdims=True)
            m_old = m_scr[h]
            if live is not None:
                off = jnp.where(live, off_scr[h, pl.ds(j, 1), :], NEG)
                m_new = jnp.maximum(m_old, top + off)
                m_sub = jnp.where(off < 0.0, -NEG, m_new)
            else:
                m_new = jnp.maximum(m_old, top)
                m_sub = m_new
            m_scr[h] = m_new
            al_scr[slot, h] = jnp.exp2(m_old - m_new)
            p_scr[slot, h] = jnp.exp2(sh - m_sub).astype(BF16)

    n_far = jnp.maximum(n - 1, 0)
    last_far = jnp.maximum(n_far - 1, 0)
    j_before = jnp.maximum(n - 1, 0)
    far_blk = lambda u: jnp.clip(u, 0, last_far)

    for h in heads:
        qk(h, jnp.where(n_far > 0, 0, j_before), 0)

    def far_pair(i, carry):
        u = 2 * i
        step(0, far_blk(u), far_blk(u + 1), far_blk(u - 1), None, u < n_far)
        nxt = jnp.where(u + 2 < n_far, far_blk(u + 2), j_before)
        step(1, far_blk(u + 1), nxt, far_blk(u), None, u + 1 < n_far)
        return carry

    lax.fori_loop(0, (n_far + 1) // 2, far_pair, 0)
    step(0, j_before, n, last_far, _TBL_PREV, n >= 1)
    step(1, n, None, j_before, _TBL_OWN, None)
    for h in heads:
        pv(h, n, 1)
        acc = acc_scr[h]
        o_ref[row[h], :, hsl[h]] = (acc[:MOBA_DH] / acc[MOBA_DH:MOBA_DH + 1]).T.astype(BF16)


def _moba(mq, mk, mvt, kmean, bias, bsz, seq):
    t, w = mq.shape
    blk = MOBA_BLOCK
    nb = seq // blk
    g = MOBA_ROWS if bsz % MOBA_ROWS == 0 else 1
    units = g * MOBA_HEADS
    resident = lambda shape: pl.BlockSpec(shape, lambda b, n: (b,) + (0,) * (len(shape) - 1),
                                          pipeline_mode=pl.Buffered(1))
    out = pl.pallas_call(
        _moba_kernel,
        grid=(bsz // g, nb),
        in_specs=[pl.BlockSpec((g, blk, w), lambda b, n: (b, n, 0)),
                  resident((g, seq, w)),
                  resident((g, MOBA_HEADS, nb, MOBA_VROWS, blk)),
                  pl.BlockSpec((g, nb, w), lambda b, n: (b, 0, 0)),
                  _const_spec(bias.shape)],
        out_specs=pl.BlockSpec((g, blk, w), lambda b, n: (b, n, 0)),
        out_shape=jax.ShapeDtypeStruct((bsz, seq, w), BF16),
        scratch_shapes=[pltpu.VMEM((units, nb, blk), F32),
                        pltpu.VMEM((units, 1, blk), F32),
                        pltpu.VMEM((2, units, 1, blk), F32),
                        pltpu.VMEM((units, MOBA_VROWS, blk), F32),
                        pltpu.VMEM((2, units, blk, blk), BF16),
                        pltpu.VMEM((2, units, blk, blk), F32)],
        compiler_params=_params(("arbitrary", "arbitrary")),
        name="moba",
    )(mq.reshape(bsz, seq, w), mk.reshape(bsz, seq, w), mvt, kmean, bias)
    return out.reshape(t, w)


def _merge_kernel(h_ref, mod_ref, nw_ref, oa_ref, ob_ref, wga_ref, wgb_ref, wa_ref, wb_ref, wo_ref,
                  o_ref, u_scr):
    mod = mod_ref[0, 0]
    u_scr[...] = _norm_mod(h_ref[...], nw_ref[...], mod).astype(BF16)
    ya = _dot(oa_ref[...], wa_ref[...])
    yb = _dot(ob_ref[...], wb_ref[...])
    merged = _sigmoid(_dot(u_scr[...], wga_ref[...])) * ya + _sigmoid(_dot(u_scr[...], wgb_ref[...])) * yb
    o_ref[...] = h_ref[...] + mod[2:3, :] * _dot(merged.astype(BF16), wo_ref[...])


def _merge(h, mod4, norm_w, oa, ob, wga, wgb, wa, wb, wo, tiles_per_batch):
    t, d = h.shape
    tm = TOKEN_TILE
    row = lambda i: (i, 0)
    return pl.pallas_call(
        _merge_kernel,
        grid=(t // tm,),
        in_specs=[pl.BlockSpec((tm, d), row),
                  pl.BlockSpec((1, 1, 3, d), lambda i: (i // tiles_per_batch, 1, 0, 0)),
                  _const_spec((1, d)),
                  pl.BlockSpec((tm, oa.shape[1]), row), pl.BlockSpec((tm, ob.shape[1]), row),
                  _const_spec(wga.shape), _const_spec(wgb.shape), _const_spec(wa.shape),
                  _const_spec(wb.shape), _const_spec(wo.shape)],
        out_specs=pl.BlockSpec((tm, d), row),
        out_shape=jax.ShapeDtypeStruct((t, d), F32),
        scratch_shapes=[pltpu.VMEM((tm, d), BF16)],
        compiler_params=_params(("arbitrary",)),
        name="merge",
    )(h, mod4, norm_w, oa, ob, wga, wgb, wa, wb, wo)


def kernel(x, c, w_ada, b_ada, norm_ff1, w_ff1_gate, w_ff1_up, w_ff1_down, norm_mix, w_in, w_gla_lr,
           b_gla_lr, gla_norm, rel_bias, w_br_gla, w_br_moba, w_out, norm_ff2, w_ff2_gate, w_ff2_up,
           w_ff2_down, norm_final):
    bsz, seq, d = x.shape
    depth = w_ada.shape[0]
    t = bsz * seq
    tiles_per_batch = seq // TOKEN_TILE
    qk_w = GLA_HEADS * (d // 16)
    v_w = GLA_HEADS * (d // 8)
    m_w = MOBA_HEADS * MOBA_DH
    offs = np.cumsum([0, qk_w, qk_w, v_w, GLA_LOWRANK, v_w, m_w, m_w, m_w, d, d])
    bias = _bias_tables(rel_bias)
    h = x.reshape(t, d)
    c_act_in = c
    for l in range(depth):
        mod4 = _adaln(c_act_in, w_ada, b_ada, l).reshape(bsz, 3, 3, d)
        cast = lambda w: w.astype(BF16)
        h = _ffn(h, mod4, 0, norm_ff1[l].reshape(1, d), cast(w_ff1_gate[l]), cast(w_ff1_up[l]),
                 cast(w_ff1_down[l]), tiles_per_batch)
        wi = w_in[l]
        seg = lambda i: wi[:, offs[i]:offs[i + 1]]
        lr_pad = jnp.pad(seg(3), ((0, 0), (0, 128 - GLA_LOWRANK)))
        wp = cast(jnp.concatenate([seg(0), seg(1), seg(2), seg(4), seg(5), seg(6), lr_pad], axis=1))
        wvt = cast(seg(7).T)
        wlr = cast(jnp.pad(w_gla_lr[l], ((0, 128 - GLA_LOWRANK), (0, 0))))
        gq, gk, gv, gog, loga, mq, mk, mvt, kmean = _inproj(
            h, mod4, norm_mix[l].reshape(1, d), wp, wvt, wlr, b_gla_lr[l].reshape(1, qk_w), bsz, seq)
        o_a = _gla(gq, gk, gv, gog, loga, gla_norm[l].reshape(1, -1), bsz, seq)
        o_b = _moba(mq, mk, mvt, kmean.reshape(bsz, seq // MOBA_BLOCK, m_w), bias, bsz, seq)
        h = _merge(h, mod4, norm_mix[l].reshape(1, d), o_a, o_b, cast(seg(8)), cast(seg(9)),
                   cast(w_br_gla[l]), cast(w_br_moba[l]), cast(w_out[l]), tiles_per_batch)
        last = l == depth - 1
        h = _ffn(h, mod4, 2, norm_ff2[l].reshape(1, d), cast(w_ff2_gate[l]), cast(w_ff2_up[l]),
                 cast(w_ff2_down[l]), tiles_per_batch, norm_final.reshape(1, d) if last else None)
    return h.reshape(bsz, seq, d)
```

```python
import functools
import math

import numpy as np
import jax
import jax.numpy as jnp
from jax import lax
from jax.experimental import pallas as pl
from jax.experimental.pallas import tpu as pltpu

F32 = jnp.float32
BF16 = jnp.bfloat16

EPS = 1e-6
FFN_RES = 0.5
GLA_HEADS = 4
GLA_TAU = 16.0
GLA_LOWRANK = 16
GLA_CHUNK = 128
MOBA_HEADS = 4
MOBA_DH = 128
MOBA_BLOCK = 256
MOBA_TOPK = 3
MOBA_VROWS = MOBA_DH + 16
MOBA_ROWS = 2
MOBA_BODIES = (8, 4, 2)
NUM_BUCKETS = 32
MAX_DISTANCE = 128
MAX_EXACT = NUM_BUCKETS // 2
NEG = -1e30
LOG2E = 1.4426950408889634
MOBA_QSCALE = LOG2E * MOBA_DH ** -0.5
TOKEN_TILE = 1024
INPROJ_TILE = 512
FF_CHUNK = 256
VMEM_LIMIT = 56 * 1024 * 1024


def _dot(a, b):
    return jnp.dot(a, b, preferred_element_type=F32)


def _dot_nt(a, b):
    return lax.dot_general(a, b, (((1,), (1,)), ((), ())), preferred_element_type=F32)


def _dot_tn(a, b):
    return lax.dot_general(a, b, (((0,), (0,)), ((), ())), preferred_element_type=F32)


def _sigmoid(x):
    return 1.0 / (1.0 + jnp.exp(-x))


def _rmsnorm(x, w):
    return x * lax.rsqrt(jnp.mean(x * x, axis=-1, keepdims=True) + EPS) * w


def _norm_mod(x, w, mod):
    return _rmsnorm(x, w) * (1.0 + mod[1:2, :]) + mod[0:1, :]


def _params(sem, flags=None):
    return pltpu.CompilerParams(dimension_semantics=sem, vmem_limit_bytes=VMEM_LIMIT, flags=flags)


def _const_spec(shape):
    nd = len(shape)
    return pl.BlockSpec(shape, lambda *_: (0,) * nd, pipeline_mode=pl.Buffered(1))


def _adaln_kernel(c_ref, w_ref, b_ref, o_ref):
    c = c_ref[...]
    ca = (c * _sigmoid(c)).astype(BF16)
    o_ref[...] = _dot(ca, w_ref[...].astype(BF16)) + b_ref[...]


def _adaln(c, w_ada, b_ada, layer):
    bsz, d = c.shape
    depth, _, n = w_ada.shape
    return pl.pallas_call(
        _adaln_kernel,
        grid=(n // d,),
        in_specs=[pl.BlockSpec((bsz, d), lambda j: (0, 0)),
                  pl.BlockSpec((None, d, d), lambda j: (layer, 0, j)),
                  pl.BlockSpec((None, 1, d), lambda j: (layer, 0, j))],
        out_specs=pl.BlockSpec((bsz, d), lambda j: (0, j)),
        out_shape=jax.ShapeDtypeStruct((bsz, n), F32),
        compiler_params=_params(("arbitrary",)),
        name="adaln",
    )(c, w_ada, b_ada.reshape(depth, 1, n))


def _ffn_kernel(x_ref, mod_ref, nw_ref, wg_ref, wu_ref, wd_ref, *rest, final_norm):
    if final_norm:
        nf_ref, o_ref, u_scr, a_scr = rest
    else:
        o_ref, u_scr, a_scr = rest
    mod = mod_ref[0, 0]
    u_scr[...] = _norm_mod(x_ref[...], nw_ref[...], mod).astype(BF16)
    d_ff = wg_ref.shape[1]
    for c in range(d_ff // FF_CHUNK):
        sl = slice(c * FF_CHUNK, (c + 1) * FF_CHUNK)
        g = _dot(u_scr[...], wg_ref[:, sl])
        up = _dot(u_scr[...], wu_ref[:, sl])
        a_scr[:, sl] = (g * _sigmoid(g) * up).astype(BF16)
    y = _dot(a_scr[...], wd_ref[...])
    out = x_ref[...] + (FFN_RES * mod[2:3, :]) * y
    if final_norm:
        out = _rmsnorm(out, nf_ref[...])
    o_ref[...] = out


def _ffn(h, mod4, layer, norm_w, wg, wu, wd, tiles_per_batch, norm_final=None):
    t, d = h.shape
    d_ff = wg.shape[1]
    tm = TOKEN_TILE
    in_specs = [pl.BlockSpec((tm, d), lambda i: (i, 0)),
                pl.BlockSpec((1, 1, 3, d), lambda i: (i // tiles_per_batch, layer, 0, 0)),
                _const_spec((1, d)), _const_spec((d, d_ff)), _const_spec((d, d_ff)), _const_spec((d_ff, d))]
    args = [h, mod4, norm_w, wg, wu, wd]
    if norm_final is not None:
        in_specs.append(_const_spec((1, d)))
        args.append(norm_final)
    return pl.pallas_call(
        functools.partial(_ffn_kernel, final_norm=norm_final is not None),
        grid=(t // tm,),
        in_specs=in_specs,
        out_specs=pl.BlockSpec((tm, d), lambda i: (i, 0)),
        out_shape=jax.ShapeDtypeStruct((t, d), F32),
        scratch_shapes=[pltpu.VMEM((tm, d), BF16), pltpu.VMEM((tm, d_ff), BF16)],
        compiler_params=_params(("arbitrary",)),
        name="ffn_final" if norm_final is not None else "ffn",
    )(*args)


_P_GQ, _P_GK, _P_GV, _P_GOG, _P_MQ, _P_MK, _P_LR, _P_END = 0, 256, 512, 1024, 1536, 2048, 2560, 2688


def _inproj_kernel(h_ref, mod_ref, nw_ref, wp_ref, wvt_ref, wlr_ref, blr_ref, gnw_ref,
                   oa_ref, mq_ref, mk_ref, mvt_ref, kmean_ref,
                   u_scr, gq_ref, gk_ref, gv_ref, gog_ref, loga_ref, st_scr, tri_scr, lev_scr, b_scr,
                   *, tiles_per_batch):
    i = pl.program_id(0)

    @pl.when(i == 0)
    def _():
        _gla_tables(tri_scr, lev_scr)

    @pl.when(lax.rem(i, tiles_per_batch) == 0)
    def _():
        st_scr[...] = jnp.zeros_like(st_scr)

    u_scr[...] = _norm_mod(h_ref[...], nw_ref[...], mod_ref[0, 0]).astype(BF16)

    def proj(lo, hi):
        return _dot(u_scr[...], wp_ref[:, lo:hi])

    gq_ref[...] = proj(_P_GQ, _P_GK) * 0.125
    gk_ref[...] = proj(_P_GK, _P_GV)
    gv_ref[...] = proj(_P_GV, _P_GOG).astype(BF16)
    gog_ref[...] = proj(_P_GOG, _P_MQ)
    glr = proj(_P_LR, _P_END).astype(BF16)
    z = _dot(glr, wlr_ref[...]) + blr_ref[...]
    g = (jnp.minimum(z, 0.0) - jnp.log1p(jnp.exp(-jnp.abs(z)))) * (LOG2E / GLA_TAU)
    g_hi = g.astype(BF16)
    loga_ref[:, :g.shape[1]] = g_hi
    loga_ref[:, g.shape[1]:] = (g - g_hi.astype(F32)).astype(BF16)

    def moba_q():
        mq_ref[...] = (proj(_P_MQ, _P_MK) * MOBA_QSCALE).astype(BF16)

    def moba_k():
        mk = proj(_P_MK, _P_LR)
        mk_ref[...] = mk.astype(BF16)
        kmean_ref[0] = jnp.mean(mk.reshape(-1, MOBA_BLOCK, mk.shape[1]), axis=1)

    def moba_v():
        mvt = _dot_nt(wvt_ref[...], u_scr[...])
        ones = jnp.ones((MOBA_VROWS - MOBA_DH, MOBA_BLOCK), BF16)
        for h in range(MOBA_HEADS):
            for j in range(mvt.shape[1] // MOBA_BLOCK):
                mvt_ref[0, h, j, :MOBA_DH, :] = mvt[h * MOBA_DH:(h + 1) * MOBA_DH,
                                                    j * MOBA_BLOCK:(j + 1) * MOBA_BLOCK].astype(BF16)
                mvt_ref[0, h, j, MOBA_DH:, :] = ones

    moba_q()
    moba_k()
    moba_v()
    chunk = _gla_chunk_fn(gq_ref, gk_ref, gv_ref, gog_ref, loga_ref, gnw_ref, oa_ref,
                          st_scr, tri_scr, lev_scr, b_scr)
    for c in range(h_ref.shape[0] // GLA_CHUNK):
        chunk(c * GLA_CHUNK, c)


def _inproj(h, mod4, norm_w, wp, wvt, wlr, blr, gla_norm, bsz, seq):
    t, d = h.shape
    tm = INPROJ_TILE
    tiles_per_batch = seq // tm
    nblk = tm // MOBA_BLOCK
    nb = seq // MOBA_BLOCK
    qk_w, v_w = _P_GK - _P_GQ, _P_GOG - _P_GV
    row = lambda i: (i, 0)
    out_shape = [jax.ShapeDtypeStruct((t, v_w), BF16), jax.ShapeDtypeStruct((t, 512), BF16),
                 jax.ShapeDtypeStruct((t, 512), BF16),
                 jax.ShapeDtypeStruct((bsz, MOBA_HEADS, nb, MOBA_VROWS, MOBA_BLOCK), BF16),
                 jax.ShapeDtypeStruct((t // tm, nblk, 512), F32)]
    out_specs = [pl.BlockSpec((tm, v_w), row), pl.BlockSpec((tm, 512), row), pl.BlockSpec((tm, 512), row),
                 pl.BlockSpec((1, MOBA_HEADS, nblk, MOBA_VROWS, MOBA_BLOCK),
                              lambda i: (i // tiles_per_batch, 0, i % tiles_per_batch, 0, 0)),
                 pl.BlockSpec((1, nblk, 512), lambda i: (i, 0, 0))]
    return pl.pallas_call(
        functools.partial(_inproj_kernel, tiles_per_batch=tiles_per_batch),
        grid=(t // tm,),
        in_specs=[pl.BlockSpec((tm, d), row),
                  pl.BlockSpec((1, 1, 3, d), lambda i: (i // tiles_per_batch, 1, 0, 0)),
                  _const_spec((1, d)), _const_spec(wp.shape), _const_spec(wvt.shape),
                  _const_spec(wlr.shape), _const_spec(blr.shape), _const_spec(gla_norm.shape)],
        out_specs=out_specs,
        out_shape=out_shape,
        scratch_shapes=[pltpu.VMEM((tm, d), BF16),
                        pltpu.VMEM((tm, qk_w), F32), pltpu.VMEM((tm, qk_w), F32),
                        pltpu.VMEM((tm, v_w), BF16), pltpu.VMEM((tm, v_w), F32),
                        pltpu.VMEM((tm, 2 * qk_w), BF16),
                        pltpu.VMEM((GLA_HEADS // 2, 2 * v_w // GLA_HEADS, 2 * qk_w // GLA_HEADS), F32),
                        pltpu.VMEM((GLA_CHUNK, GLA_CHUNK), BF16),
                        pltpu.VMEM((GLA_CHUNK, 2 * GLA_CHUNK), jnp.int32),
                        pltpu.VMEM((tm // GLA_CHUNK, GLA_CHUNK, qk_w), F32)],
        compiler_params=_params(("arbitrary",)),
        name="inproj",
    )(h, mod4, norm_w, wp, wvt, wlr, blr, gla_norm)


_GLA_LEVELS = [1 << p for p in range(int(math.log2(GLA_CHUNK)))]


def _gla_tables(tri_scr, lev_scr):
    L = GLA_CHUNK
    tri_scr[...] = (lax.broadcasted_iota(jnp.int32, (L, L), 1)
                    <= lax.broadcasted_iota(jnp.int32, (L, L), 0)).astype(BF16)
    ti = lax.broadcasted_iota(jnp.int32, (L, 2 * L), 0)
    tj = lax.broadcasted_iota(jnp.int32, (L, 2 * L), 1) & (L - 1)
    xor = ti ^ tj
    lev = jnp.where(tj > ti, -1, 0)
    for p in range(len(_GLA_LEVELS)):
        lev = jnp.where((xor >= (1 << p)) & (tj < ti), p + 1, lev)
    lev_scr[...] = lev


def _gla_chunk_fn(q_ref, k_ref, v_ref, og_ref, g_ref, nw_ref, o_ref, st_scr, tri_scr, lev_scr, b_scr):
    L = GLA_CHUNK
    hq = q_ref.shape[1] // GLA_HEADS
    hv = v_ref.shape[1] // GLA_HEADS
    wq = q_ref.shape[1]
    pairs = GLA_HEADS // 2
    assert 2 * hq == 128 and hv == 128 and L == 128
    levels = _GLA_LEVELS

    sub = lax.broadcasted_iota(jnp.int32, (L, wq), 0) & 7
    even_head = lax.broadcasted_iota(jnp.int32, (L, 2 * hq), 1) < hq
    stbd = ((lax.broadcasted_iota(jnp.int32, (2 * hv, 2 * hq), 0) < hv)
            == (lax.broadcasted_iota(jnp.int32, (2 * hv, 2 * hq), 1) < hq))

    def rows_bcast(slot, first, period):
        return jnp.concatenate([jnp.broadcast_to(b_scr[slot, r:r + 1, :], (period, wq))
                                for r in range(first, L, period)], axis=0)

    def midpoint(slot, s):
        if s >= 4:
            return rows_bcast(slot, s - 1, 2 * s)
        if s == 2:
            return jnp.where(sub < 4, rows_bcast(slot, 1, 8), rows_bcast(slot, 5, 8))
        return jnp.where(sub < 2, rows_bcast(slot, 0, 8),
                         jnp.where(sub < 4, rows_bcast(slot, 2, 8),
                                   jnp.where(sub < 6, rows_bcast(slot, 4, 8), rows_bcast(slot, 6, 8))))

    def pair_scores(qx, kx, p):
        pl_ = slice(p * 2 * hq, (p + 1) * 2 * hq)
        kp = kx[:, pl_]
        kk = jnp.concatenate([jnp.where(even_head, kp, 0.0), jnp.where(even_head, 0.0, kp)], axis=0)
        return _dot_nt(qx[:, pl_].astype(BF16), kk.astype(BF16))

    def chunk(r0, slot):
        q = q_ref[pl.ds(r0, L), :]
        k = k_ref[pl.ds(r0, L), :]
        v = v_ref[pl.ds(r0, L), :]
        bb = _dot(tri_scr[...], g_ref[pl.ds(r0, L), :])
        b = bb[:, :wq] + bb[:, wq:]
        b_scr[slot] = b
        b_last = b[L - 1:L, :]
        q_in = (q * jnp.exp2(b)).astype(BF16)
        k_out = (k * jnp.exp2(b_last - b)).astype(BF16)
        dec = jnp.exp2(b_last)

        att = [jnp.where(lev_scr[...] == 0, pair_scores(q, k, p), 0.0) for p in range(pairs)]
        for li, s in enumerate(levels):
            d = lax.bitcast_convert_type(b - midpoint(slot, s), jnp.uint32) | jnp.uint32(0x80000000)
            w = jnp.exp2(lax.bitcast_convert_type(d, F32))
            hit = lev_scr[...] == li + 1
            att = [jnp.where(hit, pair_scores(q * w, k * w, p), att[p]) for p in range(pairs)]

        for p in range(pairs):
            ql = slice(p * 2 * hq, (p + 1) * 2 * hq)
            vl = slice(p * 2 * hv, (p + 1) * 2 * hv)
            st = st_scr[p]
            vp = v[:, vl]
            ab = att[p].astype(BF16)
            o = _dot_nt(q_in[:, ql], st.astype(BF16)) + jnp.concatenate(
                [_dot(ab[:, :L], vp[:, :hv]), _dot(ab[:, L:], vp[:, hv:])], axis=1)
            st_scr[p] = st * dec[:, ql] + jnp.where(stbd, _dot_tn(vp, k_out[:, ql]), 0.0)
            for hh in range(2):
                sl = slice(vl.start + hh * hv, vl.start + (hh + 1) * hv)
                gate = og_ref[pl.ds(r0, L), sl]
                oh = _rmsnorm(o[:, hh * hv:(hh + 1) * hv], nw_ref[...])
                o_ref[pl.ds(r0, L), sl] = (oh * (gate * _sigmoid(gate))).astype(BF16)

    return chunk


def _rel_buckets(max_dist):
    n = np.arange(max_dist)
    nf = np.maximum(n, 1).astype(np.float64)
    large = MAX_EXACT + (np.log(nf / MAX_EXACT) / math.log(MAX_DISTANCE / MAX_EXACT)
                         * (NUM_BUCKETS - MAX_EXACT)).astype(np.int64)
    return np.where(n < MAX_EXACT, n, np.minimum(large, NUM_BUCKETS - 1))


_TBL_OWN, _TBL_PREV, _TBL_FAR = 0, 1, 2


def _bias_kernel(rb_ref, o_ref):
    h = pl.program_id(0)
    blk = MOBA_BLOCK
    buckets = _rel_buckets(2 * blk)
    assert (np.diff(buckets) >= 0).all()
    starts = {b: int(np.argmax(buckets == b)) for b in range(NUM_BUCKETS) if (buckets == b).any()}
    ki = lax.broadcasted_iota(jnp.int32, (blk, blk), 0)
    qi = lax.broadcasted_iota(jnp.int32, (blk, blk), 1)
    far = rb_ref[h, NUM_BUCKETS - 1]
    for tbl, base in ((_TBL_OWN, 0), (_TBL_PREV, blk)):
        dist = qi - ki + base
        val = jnp.full((blk, blk), rb_ref[h, 0], F32)
        for b in sorted(starts):
            if b > 0:
                val = jnp.where(dist >= starts[b], rb_ref[h, b], val)
        val = (val - far) * LOG2E
        if tbl == _TBL_OWN:
            val = jnp.where(dist >= 0, val, NEG)
        o_ref[0, tbl] = val
    o_ref[0, _TBL_FAR] = jnp.zeros((blk, blk), F32)


def _bias_tables(rel_bias):
    nh = rel_bias.shape[0]
    return pl.pallas_call(
        _bias_kernel,
        grid=(nh,),
        in_specs=[pl.BlockSpec(memory_space=pltpu.SMEM)],
        out_specs=pl.BlockSpec((1, 3, MOBA_BLOCK, MOBA_BLOCK), lambda h: (h, 0, 0, 0)),
        out_shape=jax.ShapeDtypeStruct((nh, 3, MOBA_BLOCK, MOBA_BLOCK), F32),
        compiler_params=_params(("arbitrary",)),
        name="moba_bias",
    )(rel_bias)


def _moba_kernel(q_ref, k_ref, vt_ref, km_ref, bias_ref, o_ref,
                 off_scr, m_scr, al_scr, acc_scr, p_scr, s_scr):
    n = pl.program_id(1)
    blk = MOBA_BLOCK
    nb = km_ref.shape[1]
    heads = range(q_ref.shape[0] * MOBA_HEADS)
    row = [h // MOBA_HEADS for h in heads]
    hsl = [slice((h % MOBA_HEADS) * MOBA_DH, (h % MOBA_HEADS + 1) * MOBA_DH) for h in heads]

    brow = lax.broadcasted_iota(jnp.int32, (nb, len(heads) * blk), 0).astype(F32)
    gate = jnp.concatenate([_dot_nt(km_ref[row[h], :, hsl[h]].astype(BF16), q_ref[row[h], :, hsl[h]])
                            for h in heads], axis=1)
    gate = jnp.where(brow < n.astype(F32), gate, NEG)
    off = jnp.full(gate.shape, NEG, F32)
    for _ in range(MOBA_TOPK):
        best = jnp.max(gate, axis=0, keepdims=True)
        first = jnp.min(jnp.where(gate == best, brow, float(nb)), axis=0, keepdims=True)
        pick = (brow == first) & (best > NEG)
        off = jnp.where(pick, 0.0, off)
        gate = jnp.where(pick, NEG, gate)
    for h in heads:
        off_scr[h] = off[:, h * blk:(h + 1) * blk]
        m_scr[h] = jnp.full((1, blk), NEG, F32)
        acc_scr[h] = jnp.zeros((MOBA_VROWS, blk), F32)
        al_scr[1, h] = jnp.ones((1, blk), F32)
        p_scr[1, h] = jnp.zeros((blk, blk), BF16)

    def qk(h, j, slot):
        r0 = pl.multiple_of(j * blk, blk)
        s_scr[slot, h] = _dot_nt(k_ref[row[h], pl.ds(r0, blk), hsl[h]], q_ref[row[h], :, hsl[h]])

    def pv(h, j, slot):
        acc_scr[h] = al_scr[slot, h] * acc_scr[h] + _dot(vt_ref[row[h], h % MOBA_HEADS, j], p_scr[slot, h])

    def step(slot, j, j_next, j_prev, table, live):
        for h in heads:
            pv(h, j_prev, 1 - slot)
            if j_next is not None:
                qk(h, j_next, 1 - slot)
        for h in heads:
            sh = s_scr[slot, h]
            if table is not None:
                sh = sh + bias_ref[h % MOBA_HEADS, table]
            top = jnp.max(sh, axis=0, keepdims=True)
            m_old = m_scr[h]
            if live is not None:
                off = jnp.where(live, off_scr[h, pl.ds(j, 1), :], NEG)
                m_new = jnp.maximum(m_old, top + off)
                m_sub = jnp.where(off < 0.0, -NEG, m_new)
            else:
                m_new = jnp.maximum(m_old, top)
                m_sub = m_new
            m_scr[h] = m_new
            al_scr[slot, h] = jnp.exp2(m_old - m_new)
            p_scr[slot, h] = jnp.exp2(sh - m_sub).astype(BF16)

    n_far = jnp.maximum(n - 1, 0)
    last_far = jnp.maximum(n_far - 1, 0)
    j_before = jnp.maximum(n - 1, 0)
    far_blk = lambda u: jnp.clip(u, 0, last_far)

    for h in heads:
        qk(h, jnp.where(n_far > 0, 0, j_before), 0)

    def far_steps(u, count):
        for c in range(count):
            nxt = far_blk(u + c + 1)
            if c == count - 1:
                nxt = jnp.where(u + count < n_far, nxt, j_before)
            step(c % 2, far_blk(u + c), nxt, far_blk(u + c - 1), None, u + c < n_far)

    done = 0
    for size in MOBA_BODIES:
        left = n_far - done
        trips = (left + 1) // 2 if size == 2 else left // size

        def body(i, carry, size=size, done=done):
            far_steps(done + size * i, size)
            return carry

        lax.fori_loop(0, trips, body, 0)
        done = done + trips * size
    step(0, j_before, n, last_far, _TBL_PREV, n >= 1)
    step(1, n, None, j_before, _TBL_OWN, None)
    for h in heads:
        pv(h, n, 1)
        acc = acc_scr[h]
        o_ref[row[h], :, hsl[h]] = (acc[:MOBA_DH] / acc[MOBA_DH:MOBA_DH + 1]).T.astype(BF16)


def _moba(mq, mk, mvt, kmean, bias, bsz, seq):
    t, w = mq.shape
    blk = MOBA_BLOCK
    nb = seq // blk
    g = MOBA_ROWS if bsz % MOBA_ROWS == 0 else 1
    units = g * MOBA_HEADS
    resident = lambda shape: pl.BlockSpec(shape, lambda b, n: (b,) + (0,) * (len(shape) - 1),
                                          pipeline_mode=pl.Buffered(1))
    out = pl.pallas_call(
        _moba_kernel,
        grid=(bsz // g, nb),
        in_specs=[pl.BlockSpec((g, blk, w), lambda b, n: (b, n, 0)),
                  resident((g, seq, w)),
                  resident((g, MOBA_HEADS, nb, MOBA_VROWS, blk)),
                  pl.BlockSpec((g, nb, w), lambda b, n: (b, 0, 0)),
                  _const_spec(bias.shape)],
        out_specs=pl.BlockSpec((g, blk, w), lambda b, n: (b, n, 0)),
        out_shape=jax.ShapeDtypeStruct((bsz, seq, w), BF16),
        scratch_shapes=[pltpu.VMEM((units, nb, blk), F32),
                        pltpu.VMEM((units, 1, blk), F32),
                        pltpu.VMEM((2, units, 1, blk), F32),
                        pltpu.VMEM((units, MOBA_VROWS, blk), F32),
                        pltpu.VMEM((2, units, blk, blk), BF16),
                        pltpu.VMEM((2, units, blk, blk), F32)],
        compiler_params=_params(("arbitrary", "arbitrary")),
        name="moba",
    )(mq.reshape(bsz, seq, w), mk.reshape(bsz, seq, w), mvt, kmean, bias)
    return out.reshape(t, w)


def _merge_kernel(h_ref, mod_ref, nw_ref, oa_ref, ob_ref, wga_ref, wgb_ref, wa_ref, wb_ref, wo_ref,
                  o_ref, u_scr):
    mod = mod_ref[0, 0]
    u_scr[...] = _norm_mod(h_ref[...], nw_ref[...], mod).astype(BF16)
    ya = _dot(oa_ref[...], wa_ref[...])
    yb = _dot(ob_ref[...], wb_ref[...])
    merged = _sigmoid(_dot(u_scr[...], wga_ref[...])) * ya + _sigmoid(_dot(u_scr[...], wgb_ref[...])) * yb
    o_ref[...] = h_ref[...] + mod[2:3, :] * _dot(merged.astype(BF16), wo_ref[...])


def _merge(h, mod4, norm_w, oa, ob, wga, wgb, wa, wb, wo, tiles_per_batch):
    t, d = h.shape
    tm = TOKEN_TILE
    row = lambda i: (i, 0)
    return pl.pallas_call(
        _merge_kernel,
        grid=(t // tm,),
        in_specs=[pl.BlockSpec((tm, d), row),
                  pl.BlockSpec((1, 1, 3, d), lambda i: (i // tiles_per_batch, 1, 0, 0)),
                  _const_spec((1, d)),
                  pl.BlockSpec((tm, oa.shape[1]), row), pl.BlockSpec((tm, ob.shape[1]), row),
                  _const_spec(wga.shape), _const_spec(wgb.shape), _const_spec(wa.shape),
                  _const_spec(wb.shape), _const_spec(wo.shape)],
        out_specs=pl.BlockSpec((tm, d), row),
        out_shape=jax.ShapeDtypeStruct((t, d), F32),
        scratch_shapes=[pltpu.VMEM((tm, d), BF16)],
        compiler_params=_params(("arbitrary",)),
        name="merge",
    )(h, mod4, norm_w, oa, ob, wga, wgb, wa, wb, wo)


def kernel(x, c, w_ada, b_ada, norm_ff1, w_ff1_gate, w_ff1_up, w_ff1_down, norm_mix, w_in, w_gla_lr,
           b_gla_lr, gla_norm, rel_bias, w_br_gla, w_br_moba, w_out, norm_ff2, w_ff2_gate, w_ff2_up,
           w_ff2_down, norm_final):
    bsz, seq, d = x.shape
    depth = w_ada.shape[0]
    t = bsz * seq
    tiles_per_batch = seq // TOKEN_TILE
    qk_w = GLA_HEADS * (d // 16)
    v_w = GLA_HEADS * (d // 8)
    m_w = MOBA_HEADS * MOBA_DH
    offs = np.cumsum([0, qk_w, qk_w, v_w, GLA_LOWRANK, v_w, m_w, m_w, m_w, d, d])
    bias = _bias_tables(rel_bias)
    h = x.reshape(t, d)
    c_act_in = c
    for l in range(depth):
        mod4 = _adaln(c_act_in, w_ada, b_ada, l).reshape(bsz, 3, 3, d)
        cast = lambda w: w.astype(BF16)
        h = _ffn(h, mod4, 0, norm_ff1[l].reshape(1, d), cast(w_ff1_gate[l]), cast(w_ff1_up[l]),
                 cast(w_ff1_down[l]), tiles_per_batch)
        wi = w_in[l]
        seg = lambda i: wi[:, offs[i]:offs[i + 1]]
        lr_pad = jnp.pad(seg(3), ((0, 0), (0, 128 - GLA_LOWRANK)))
        wp = cast(jnp.concatenate([seg(0), seg(1), seg(2), seg(4), seg(5), seg(6), lr_pad], axis=1))
        wvt = cast(seg(7).T)
        wlr = cast(jnp.pad(w_gla_lr[l], ((0, 128 - GLA_LOWRANK), (0, 0))))
        o_a, mq, mk, mvt, kmean = _inproj(
            h, mod4, norm_mix[l].reshape(1, d), wp, wvt, wlr, b_gla_lr[l].reshape(1, qk_w),
            gla_norm[l].reshape(1, -1), bsz, seq)
        o_b = _moba(mq, mk, mvt, kmean.reshape(bsz, seq // MOBA_BLOCK, m_w), bias, bsz, seq)
        h = _merge(h, mod4, norm_mix[l].reshape(1, d), o_a, o_b, cast(seg(8)), cast(seg(9)),
                   cast(w_br_gla[l]), cast(w_br_moba[l]), cast(w_out[l]), tiles_per_batch)
        last = l == depth - 1
        h = _ffn(h, mod4, 2, norm_ff2[l].reshape(1, d), cast(w_ff2_gate[l]), cast(w_ff2_up[l]),
                 cast(w_ff2_down[l]), tiles_per_batch, norm_final.reshape(1, d) if last else None)
    return h.reshape(bsz, seq, d)
```

```python
import functools
import math

import numpy as np
import jax
import jax.numpy as jnp
from jax import lax
from jax.experimental import pallas as pl
from jax.experimental.pallas import tpu as pltpu

F32 = jnp.float32
BF16 = jnp.bfloat16

EPS = 1e-6
FFN_RES = 0.5
GLA_HEADS = 4
GLA_TAU = 16.0
GLA_LOWRANK = 16
GLA_CHUNK = 128
MOBA_HEADS = 4
MOBA_DH = 128
MOBA_BLOCK = 256
MOBA_TOPK = 3
MOBA_VROWS = MOBA_DH + 16
MOBA_ROWS = 2
MOBA_BODIES = (8, 4, 2)
NUM_BUCKETS = 32
MAX_DISTANCE = 128
MAX_EXACT = NUM_BUCKETS // 2
NEG = -1e30
LOG2E = 1.4426950408889634
MOBA_QSCALE = LOG2E * MOBA_DH ** -0.5
TOKEN_TILE = 1024
FFN_TILE = 1024
INPROJ_TILE = 512
FF_CHUNK = 256
V7X_VMEM_BYTES = 64 * 1024 * 1024
VMEM_LIMIT = V7X_VMEM_BYTES - 3 * 1024 * 1024


def _dot(a, b):
    return jnp.dot(a, b, preferred_element_type=F32)


def _dot_nt(a, b):
    return lax.dot_general(a, b, (((1,), (1,)), ((), ())), preferred_element_type=F32)


def _dot_tn(a, b):
    return lax.dot_general(a, b, (((0,), (0,)), ((), ())), preferred_element_type=F32)


def _sigmoid(x):
    return 1.0 / (1.0 + jnp.exp(-x))


def _rmsnorm(x, w):
    return x * lax.rsqrt(jnp.mean(x * x, axis=-1, keepdims=True) + EPS) * w


def _norm_mod(x, w, mod):
    return _rmsnorm(x, w) * (1.0 + mod[1:2, :]) + mod[0:1, :]


def _params(sem):
    return pltpu.CompilerParams(dimension_semantics=sem, vmem_limit_bytes=VMEM_LIMIT)


def _const_spec(shape):
    nd = len(shape)
    return pl.BlockSpec(shape, lambda *_: (0,) * nd, pipeline_mode=pl.Buffered(1))


def _adaln_kernel(c_ref, w_ref, b_ref, o_ref):
    c = c_ref[...]
    ca = (c * _sigmoid(c)).astype(BF16)
    o_ref[...] = _dot(ca, w_ref[...].astype(BF16)) + b_ref[...]


def _adaln(c, w_ada, b_ada, layer):
    bsz, d = c.shape
    depth, _, n = w_ada.shape
    return pl.pallas_call(
        _adaln_kernel,
        grid=(n // d,),
        in_specs=[pl.BlockSpec((bsz, d), lambda j: (0, 0)),
                  pl.BlockSpec((None, d, d), lambda j: (layer, 0, j)),
                  pl.BlockSpec((None, 1, d), lambda j: (layer, 0, j))],
        out_specs=pl.BlockSpec((bsz, d), lambda j: (0, j)),
        out_shape=jax.ShapeDtypeStruct((bsz, n), F32),
        compiler_params=_params(("arbitrary",)),
        name="adaln",
    )(c, w_ada, b_ada.reshape(depth, 1, n))


def _ffn_kernel(x_ref, mod_ref, nw_ref, wg_ref, wu_ref, wd_ref, *rest, final_norm):
    if final_norm:
        nf_ref, o_ref, u_scr, a_scr = rest
    else:
        o_ref, u_scr, a_scr = rest
    mod = mod_ref[0, 0]
    u_scr[...] = _norm_mod(x_ref[...], nw_ref[...], mod).astype(BF16)
    d_ff = wg_ref.shape[1]
    for c in range(d_ff // FF_CHUNK):
        sl = slice(c * FF_CHUNK, (c + 1) * FF_CHUNK)
        g = _dot(u_scr[...], wg_ref[:, sl].astype(BF16))
        up = _dot(u_scr[...], wu_ref[:, sl].astype(BF16))
        a_scr[:, sl] = (g * _sigmoid(g) * up).astype(BF16)
    y = _dot(a_scr[...], wd_ref[...].astype(BF16))
    out = x_ref[...] + (FFN_RES * mod[2:3, :]) * y
    if final_norm:
        out = _rmsnorm(out, nf_ref[...])
    o_ref[...] = out


def _ffn(h, mod4, layer, norm_w, wg, wu, wd, seq, norm_final=None):
    t, d = h.shape
    d_ff = wg.shape[1]
    tm = FFN_TILE
    tiles_per_batch = seq // tm
    in_specs = [pl.BlockSpec((tm, d), lambda i: (i, 0)),
                pl.BlockSpec((1, 1, 3, d), lambda i: (i // tiles_per_batch, layer, 0, 0)),
                _const_spec((1, d)), _const_spec((d, d_ff)), _const_spec((d, d_ff)), _const_spec((d_ff, d))]
    args = [h, mod4, norm_w, wg, wu, wd]
    if norm_final is not None:
        in_specs.append(_const_spec((1, d)))
        args.append(norm_final)
    return pl.pallas_call(
        functools.partial(_ffn_kernel, final_norm=norm_final is not None),
        grid=(t // tm,),
        in_specs=in_specs,
        out_specs=pl.BlockSpec((tm, d), lambda i: (i, 0)),
        out_shape=jax.ShapeDtypeStruct((t, d), F32),
        scratch_shapes=[pltpu.VMEM((tm, d), BF16), pltpu.VMEM((tm, d_ff), BF16)],
        compiler_params=_params(("arbitrary",)),
        name="ffn_final" if norm_final is not None else "ffn",
    )(*args)


_P_GQ, _P_GK, _P_GV, _P_GOG, _P_MQ, _P_MK, _P_LR, _P_END = 0, 256, 512, 1024, 1536, 2048, 2560, 2688


def _inproj_kernel(h_ref, mod_ref, nw_ref, wp_ref, wvt_ref, wlr_ref, blr_ref, gnw_ref,
                   oa_ref, mq_ref, mk_ref, mvt_ref, kmean_ref,
                   u_scr, gq_scr, gk_scr, gv_scr, gog_scr, dec_scr, st_scr, tri_scr, lev_scr, b_scr,
                   *, tiles_per_batch):
    i = pl.program_id(0)

    @pl.when(i == 0)
    def _():
        _gla_tables(tri_scr, lev_scr)

    @pl.when(lax.rem(i, tiles_per_batch) == 0)
    def _():
        st_scr[...] = jnp.zeros_like(st_scr)

    u_scr[...] = _norm_mod(h_ref[...], nw_ref[...], mod_ref[0, 0]).astype(BF16)

    def proj(lo, hi):
        return _dot(u_scr[...], wp_ref[:, lo:hi])

    gq_scr[...] = proj(_P_GQ, _P_GK) * 0.125
    gk_scr[...] = proj(_P_GK, _P_GV)
    gv_scr[...] = proj(_P_GV, _P_GOG).astype(BF16)
    gog_scr[...] = proj(_P_GOG, _P_MQ)
    glr = proj(_P_LR, _P_END).astype(BF16)
    z = _dot(glr, wlr_ref[...]) + blr_ref[...]
    g = (jnp.minimum(z, 0.0) - jnp.log1p(jnp.exp(-jnp.abs(z)))) * (LOG2E / GLA_TAU)
    g_hi = g.astype(BF16)
    dec_scr[:, :g.shape[1]] = g_hi
    dec_scr[:, g.shape[1]:] = (g - g_hi.astype(F32)).astype(BF16)

    mq_ref[...] = (proj(_P_MQ, _P_MK) * MOBA_QSCALE).astype(BF16)
    mk = proj(_P_MK, _P_LR)
    mk_ref[...] = mk.astype(BF16)
    kmean_ref[0] = jnp.mean(mk.reshape(-1, MOBA_BLOCK, mk.shape[1]), axis=1)
    mvt = _dot_nt(wvt_ref[...], u_scr[...])
    ones = jnp.ones((MOBA_VROWS - MOBA_DH, MOBA_BLOCK), BF16)
    for h in range(MOBA_HEADS):
        for j in range(mvt.shape[1] // MOBA_BLOCK):
            mvt_ref[0, h, j, :MOBA_DH, :] = mvt[h * MOBA_DH:(h + 1) * MOBA_DH,
                                                j * MOBA_BLOCK:(j + 1) * MOBA_BLOCK].astype(BF16)
            mvt_ref[0, h, j, MOBA_DH:, :] = ones

    chunk = _gla_chunk_fn(gq_scr, gk_scr, gv_scr, gog_scr, dec_scr, gnw_ref, oa_ref,
                          st_scr, tri_scr, lev_scr, b_scr)
    for c in range(h_ref.shape[0] // GLA_CHUNK):
        chunk(c * GLA_CHUNK, c)


def _inproj(h, mod4, norm_w, wp, wvt, wlr, blr, gla_norm, bsz, seq):
    t, d = h.shape
    tm = INPROJ_TILE
    tiles_per_batch = seq // tm
    nblk = tm // MOBA_BLOCK
    nb = seq // MOBA_BLOCK
    qk_w, v_w = _P_GK - _P_GQ, _P_GOG - _P_GV
    row = lambda i: (i, 0)
    out_shape = [jax.ShapeDtypeStruct((t, v_w), BF16), jax.ShapeDtypeStruct((t, 512), BF16),
                 jax.ShapeDtypeStruct((t, 512), BF16),
                 jax.ShapeDtypeStruct((bsz, MOBA_HEADS, nb, MOBA_VROWS, MOBA_BLOCK), BF16),
                 jax.ShapeDtypeStruct((t // tm, nblk, 512), F32)]
    out_specs = [pl.BlockSpec((tm, v_w), row), pl.BlockSpec((tm, 512), row), pl.BlockSpec((tm, 512), row),
                 pl.BlockSpec((1, MOBA_HEADS, nblk, MOBA_VROWS, MOBA_BLOCK),
                              lambda i: (i // tiles_per_batch, 0, i % tiles_per_batch, 0, 0)),
                 pl.BlockSpec((1, nblk, 512), lambda i: (i, 0, 0))]
    return pl.pallas_call(
        functools.partial(_inproj_kernel, tiles_per_batch=tiles_per_batch),
        grid=(t // tm,),
        in_specs=[pl.BlockSpec((tm, d), row),
                  pl.BlockSpec((1, 1, 3, d), lambda i: (i // tiles_per_batch, 1, 0, 0)),
                  _const_spec((1, d)), _const_spec(wp.shape), _const_spec(wvt.shape),
                  _const_spec(wlr.shape), _const_spec(blr.shape), _const_spec(gla_norm.shape)],
        out_specs=out_specs,
        out_shape=out_shape,
        scratch_shapes=[pltpu.VMEM((tm, d), BF16),
                        pltpu.VMEM((tm, qk_w), F32), pltpu.VMEM((tm, qk_w), F32),
                        pltpu.VMEM((tm, v_w), BF16), pltpu.VMEM((tm, v_w), F32),
                        pltpu.VMEM((tm, 2 * qk_w), BF16),
                        pltpu.VMEM((GLA_HEADS // 2, 2 * v_w // GLA_HEADS, 2 * qk_w // GLA_HEADS), F32),
                        pltpu.VMEM((GLA_CHUNK, GLA_CHUNK), BF16),
                        pltpu.VMEM((GLA_CHUNK, 2 * GLA_CHUNK), jnp.int32),
                        pltpu.VMEM((tm // GLA_CHUNK, GLA_CHUNK, qk_w), F32)],
        compiler_params=_params(("arbitrary",)),
        name="inproj",
    )(h, mod4, norm_w, wp, wvt, wlr, blr, gla_norm)


_GLA_LEVELS = [1 << p for p in range(int(math.log2(GLA_CHUNK)))]


def _gla_tables(tri_scr, lev_scr):
    L = GLA_CHUNK
    tri_scr[...] = (lax.broadcasted_iota(jnp.int32, (L, L), 1)
                    <= lax.broadcasted_iota(jnp.int32, (L, L), 0)).astype(BF16)
    ti = lax.broadcasted_iota(jnp.int32, (L, 2 * L), 0)
    tj = lax.broadcasted_iota(jnp.int32, (L, 2 * L), 1) & (L - 1)
    xor = ti ^ tj
    lev = jnp.where(tj > ti, -1, 0)
    for p in range(len(_GLA_LEVELS)):
        lev = jnp.where((xor >= (1 << p)) & (tj < ti), p + 1, lev)
    lev_scr[...] = lev


def _gla_chunk_fn(q_ref, k_ref, v_ref, og_ref, g_ref, nw_ref, o_ref, st_scr, tri_scr, lev_scr, b_scr):
    L = GLA_CHUNK
    hq = q_ref.shape[1] // GLA_HEADS
    hv = v_ref.shape[1] // GLA_HEADS
    wq = q_ref.shape[1]
    pairs = GLA_HEADS // 2
    assert 2 * hq == 128 and hv == 128 and L == 128
    levels = _GLA_LEVELS

    sub = lax.broadcasted_iota(jnp.int32, (L, wq), 0) & 7
    even_head = lax.broadcasted_iota(jnp.int32, (L, 2 * hq), 1) < hq
    stbd = ((lax.broadcasted_iota(jnp.int32, (2 * hv, 2 * hq), 0) < hv)
            == (lax.broadcasted_iota(jnp.int32, (2 * hv, 2 * hq), 1) < hq))

    def rows_bcast(slot, first, period):
        return jnp.concatenate([jnp.broadcast_to(b_scr[slot, r:r + 1, :], (period, wq))
                                for r in range(first, L, period)], axis=0)

    def midpoint(slot, s):
        if s >= 4:
            return rows_bcast(slot, s - 1, 2 * s)
        if s == 2:
            return jnp.where(sub < 4, rows_bcast(slot, 1, 8), rows_bcast(slot, 5, 8))
        return jnp.where(sub < 2, rows_bcast(slot, 0, 8),
                         jnp.where(sub < 4, rows_bcast(slot, 2, 8),
                                   jnp.where(sub < 6, rows_bcast(slot, 4, 8), rows_bcast(slot, 6, 8))))

    def pair_scores(qx, kx, p):
        pl_ = slice(p * 2 * hq, (p + 1) * 2 * hq)
        kp = kx[:, pl_]
        kk = jnp.concatenate([jnp.where(even_head, kp, 0.0), jnp.where(even_head, 0.0, kp)], axis=0)
        return _dot_nt(qx[:, pl_].astype(BF16), kk.astype(BF16))

    def chunk(r0, slot):
        q = q_ref[pl.ds(r0, L), :]
        k = k_ref[pl.ds(r0, L), :]
        v = v_ref[pl.ds(r0, L), :]
        bb = _dot(tri_scr[...], g_ref[pl.ds(r0, L), :])
        b = bb[:, :wq] + bb[:, wq:]
        b_scr[slot] = b
        b_last = b[L - 1:L, :]
        q_in = (q * jnp.exp2(b)).astype(BF16)
        k_out = (k * jnp.exp2(b_last - b)).astype(BF16)
        dec = jnp.exp2(b_last)

        att = [jnp.where(lev_scr[...] == 0, pair_scores(q, k, p), 0.0) for p in range(pairs)]
        for li, s in enumerate(levels):
            d = lax.bitcast_convert_type(b - midpoint(slot, s), jnp.uint32) | jnp.uint32(0x80000000)
            w = jnp.exp2(lax.bitcast_convert_type(d, F32))
            hit = lev_scr[...] == li + 1
            att = [jnp.where(hit, pair_scores(q * w, k * w, p), att[p]) for p in range(pairs)]

        for p in range(pairs):
            ql = slice(p * 2 * hq, (p + 1) * 2 * hq)
            vl = slice(p * 2 * hv, (p + 1) * 2 * hv)
            st = st_scr[p]
            vp = v[:, vl]
            ab = att[p].astype(BF16)
            o = _dot_nt(q_in[:, ql], st.astype(BF16)) + jnp.concatenate(
                [_dot(ab[:, :L], vp[:, :hv]), _dot(ab[:, L:], vp[:, hv:])], axis=1)
            st_scr[p] = st * dec[:, ql] + jnp.where(stbd, _dot_tn(vp, k_out[:, ql]), 0.0)
            for hh in range(2):
                sl = slice(vl.start + hh * hv, vl.start + (hh + 1) * hv)
                gate = og_ref[pl.ds(r0, L), sl]
                oh = _rmsnorm(o[:, hh * hv:(hh + 1) * hv], nw_ref[...])
                o_ref[pl.ds(r0, L), sl] = (oh * (gate * _sigmoid(gate))).astype(BF16)

    return chunk


def _rel_buckets(max_dist):
    n = np.arange(max_dist)
    nf = np.maximum(n, 1).astype(np.float64)
    large = MAX_EXACT + (np.log(nf / MAX_EXACT) / math.log(MAX_DISTANCE / MAX_EXACT)
                         * (NUM_BUCKETS - MAX_EXACT)).astype(np.int64)
    return np.where(n < MAX_EXACT, n, np.minimum(large, NUM_BUCKETS - 1))


_TBL_OWN, _TBL_PREV, _TBL_FAR = 0, 1, 2


def _bias_kernel(rb_ref, o_ref):
    h = pl.program_id(0)
    blk = MOBA_BLOCK
    buckets = _rel_buckets(2 * blk)
    assert (np.diff(buckets) >= 0).all()
    starts = {b: int(np.argmax(buckets == b)) for b in range(NUM_BUCKETS) if (buckets == b).any()}
    ki = lax.broadcasted_iota(jnp.int32, (blk, blk), 0)
    qi = lax.broadcasted_iota(jnp.int32, (blk, blk), 1)
    far = rb_ref[h, NUM_BUCKETS - 1]
    for tbl, base in ((_TBL_OWN, 0), (_TBL_PREV, blk)):
        dist = qi - ki + base
        val = jnp.full((blk, blk), rb_ref[h, 0], F32)
        for b in sorted(starts):
            if b > 0:
                val = jnp.where(dist >= starts[b], rb_ref[h, b], val)
        val = (val - far) * LOG2E
        if tbl == _TBL_OWN:
            val = jnp.where(dist >= 0, val, NEG)
        o_ref[0, tbl] = val
    o_ref[0, _TBL_FAR] = jnp.zeros((blk, blk), F32)


def _bias_tables(rel_bias):
    nh = rel_bias.shape[0]
    return pl.pallas_call(
        _bias_kernel,
        grid=(nh,),
        in_specs=[pl.BlockSpec(memory_space=pltpu.SMEM)],
        out_specs=pl.BlockSpec((1, 3, MOBA_BLOCK, MOBA_BLOCK), lambda h: (h, 0, 0, 0)),
        out_shape=jax.ShapeDtypeStruct((nh, 3, MOBA_BLOCK, MOBA_BLOCK), F32),
        compiler_params=_params(("arbitrary",)),
        name="moba_bias",
    )(rel_bias)


def _moba_kernel(q_ref, k_ref, vt_ref, km_ref, bias_ref, o_ref,
                 off_scr, m_scr, al_scr, acc_scr, p_scr, s_scr):
    n = pl.program_id(1)
    blk = MOBA_BLOCK
    nb = km_ref.shape[1]
    heads = range(q_ref.shape[0] * MOBA_HEADS)
    row = [h // MOBA_HEADS for h in heads]
    hsl = [slice((h % MOBA_HEADS) * MOBA_DH, (h % MOBA_HEADS + 1) * MOBA_DH) for h in heads]

    brow = lax.broadcasted_iota(jnp.int32, (nb, len(heads) * blk), 0).astype(F32)
    gate = jnp.concatenate([_dot_nt(km_ref[row[h], :, hsl[h]].astype(BF16), q_ref[row[h], :, hsl[h]])
                            for h in heads], axis=1)
    gate = jnp.where(brow < n.astype(F32), gate, NEG)
    off = jnp.full(gate.shape, NEG, F32)
    for _ in range(MOBA_TOPK):
        best = jnp.max(gate, axis=0, keepdims=True)
        first = jnp.min(jnp.where(gate == best, brow, float(nb)), axis=0, keepdims=True)
        pick = (brow == first) & (best > NEG)
        off = jnp.where(pick, 0.0, off)
        gate = jnp.where(pick, NEG, gate)
    for h in heads:
        off_scr[h] = off[:, h * blk:(h + 1) * blk]
        m_scr[h] = jnp.full((1, blk), NEG, F32)
        acc_scr[h] = jnp.zeros((MOBA_VROWS, blk), F32)
        al_scr[1, h] = jnp.ones((1, blk), F32)
        p_scr[1, h] = jnp.zeros((blk, blk), BF16)

    def qk(h, j, slot):
        r0 = pl.multiple_of(j * blk, blk)
        s_scr[slot, h] = _dot_nt(k_ref[row[h], pl.ds(r0, blk), hsl[h]], q_ref[row[h], :, hsl[h]])

    def pv(h, j, slot):
        acc_scr[h] = al_scr[slot, h] * acc_scr[h] + _dot(vt_ref[row[h], h % MOBA_HEADS, j], p_scr[slot, h])

    def step(slot, j, j_next, j_prev, table, live):
        for h in heads:
            pv(h, j_prev, 1 - slot)
            if j_next is not None:
                qk(h, j_next, 1 - slot)
        for h in heads:
            sh = s_scr[slot, h]
            if table is not None:
                sh = sh + bias_ref[h % MOBA_HEADS, table]
            top = jnp.max(sh, axis=0, keepdims=True)
            m_old = m_scr[h]
            if live is not None:
                off = jnp.where(live, off_scr[h, pl.ds(j, 1), :], NEG)
                m_new = jnp.maximum(m_old, top + off)
                m_sub = jnp.where(off < 0.0, -NEG, m_new)
            else:
                m_new = jnp.maximum(m_old, top)
                m_sub = m_new
            m_scr[h] = m_new
            al_scr[slot, h] = jnp.exp2(m_old - m_new)
            p_scr[slot, h] = jnp.exp2(sh - m_sub).astype(BF16)

    n_far = jnp.maximum(n - 1, 0)
    last_far = jnp.maximum(n_far - 1, 0)
    j_before = jnp.maximum(n - 1, 0)
    far_blk = lambda u: jnp.clip(u, 0, last_far)

    for h in heads:
        qk(h, jnp.where(n_far > 0, 0, j_before), 0)

    def far_steps(u, count):
        for c in range(count):
            nxt = far_blk(u + c + 1)
            if c == count - 1:
                nxt = jnp.where(u + count < n_far, nxt, j_before)
            step(c % 2, far_blk(u + c), nxt, far_blk(u + c - 1), None, u + c < n_far)

    done = 0
    for size in MOBA_BODIES:
        left = n_far - done
        trips = (left + 1) // 2 if size == 2 else left // size

        def body(i, carry, size=size, done=done):
            far_steps(done + size * i, size)
            return carry

        lax.fori_loop(0, trips, body, 0)
        done = done + trips * size
    step(0, j_before, n, last_far, _TBL_PREV, n >= 1)
    step(1, n, None, j_before, _TBL_OWN, None)
    for h in heads:
        pv(h, n, 1)
        acc = acc_scr[h]
        o_ref[row[h], :, hsl[h]] = (acc[:MOBA_DH] / acc[MOBA_DH:MOBA_DH + 1]).T.astype(BF16)


def _moba(mq, mk, mvt, kmean, bias, bsz, seq):
    t, w = mq.shape
    blk = MOBA_BLOCK
    nb = seq // blk
    g = MOBA_ROWS if bsz % MOBA_ROWS == 0 else 1
    units = g * MOBA_HEADS
    resident = lambda shape: pl.BlockSpec(shape, lambda b, n: (b,) + (0,) * (len(shape) - 1),
                                          pipeline_mode=pl.Buffered(1))
    out = pl.pallas_call(
        _moba_kernel,
        grid=(bsz // g, nb),
        in_specs=[pl.BlockSpec((g, blk, w), lambda b, n: (b, n, 0)),
                  resident((g, seq, w)),
                  resident((g, MOBA_HEADS, nb, MOBA_VROWS, blk)),
                  pl.BlockSpec((g, nb, w), lambda b, n: (b, 0, 0)),
                  _const_spec(bias.shape)],
        out_specs=pl.BlockSpec((g, blk, w), lambda b, n: (b, n, 0)),
        out_shape=jax.ShapeDtypeStruct((bsz, seq, w), BF16),
        scratch_shapes=[pltpu.VMEM((units, nb, blk), F32),
                        pltpu.VMEM((units, 1, blk), F32),
                        pltpu.VMEM((2, units, 1, blk), F32),
                        pltpu.VMEM((units, MOBA_VROWS, blk), F32),
                        pltpu.VMEM((2, units, blk, blk), BF16),
                        pltpu.VMEM((2, units, blk, blk), F32)],
        compiler_params=_params(("arbitrary", "arbitrary")),
        name="moba",
    )(mq.reshape(bsz, seq, w), mk.reshape(bsz, seq, w), mvt, kmean, bias)
    return out.reshape(t, w)


def _merge_kernel(h_ref, mod_ref, nw_ref, oa_ref, ob_ref, wga_ref, wgb_ref, wa_ref, wb_ref, wo_ref,
                  o_ref, u_scr):
    mod = mod_ref[0, 0]
    u_scr[...] = _norm_mod(h_ref[...], nw_ref[...], mod).astype(BF16)
    ya = _dot(oa_ref[...], wa_ref[...])
    yb = _dot(ob_ref[...], wb_ref[...])
    merged = _sigmoid(_dot(u_scr[...], wga_ref[...])) * ya + _sigmoid(_dot(u_scr[...], wgb_ref[...])) * yb
    o_ref[...] = h_ref[...] + mod[2:3, :] * _dot(merged.astype(BF16), wo_ref[...])


def _merge(h, mod4, norm_w, oa, ob, wga, wgb, wa, wb, wo, tiles_per_batch):
    t, d = h.shape
    tm = TOKEN_TILE
    row = lambda i: (i, 0)
    return pl.pallas_call(
        _merge_kernel,
        grid=(t // tm,),
        in_specs=[pl.BlockSpec((tm, d), row),
                  pl.BlockSpec((1, 1, 3, d), lambda i: (i // tiles_per_batch, 1, 0, 0)),
                  _const_spec((1, d)),
                  pl.BlockSpec((tm, oa.shape[1]), row), pl.BlockSpec((tm, ob.shape[1]), row),
                  _const_spec(wga.shape), _const_spec(wgb.shape), _const_spec(wa.shape),
                  _const_spec(wb.shape), _const_spec(wo.shape)],
        out_specs=pl.BlockSpec((tm, d), row),
        out_shape=jax.ShapeDtypeStruct((t, d), F32),
        scratch_shapes=[pltpu.VMEM((tm, d), BF16)],
        compiler_params=_params(("arbitrary",)),
        name="merge",
    )(h, mod4, norm_w, oa, ob, wga, wgb, wa, wb, wo)


def kernel(x, c, w_ada, b_ada, norm_ff1, w_ff1_gate, w_ff1_up, w_ff1_down, norm_mix, w_in, w_gla_lr,
           b_gla_lr, gla_norm, rel_bias, w_br_gla, w_br_moba, w_out, norm_ff2, w_ff2_gate, w_ff2_up,
           w_ff2_down, norm_final):
    bsz, seq, d = x.shape
    depth = w_ada.shape[0]
    t = bsz * seq
    tiles_per_batch = seq // TOKEN_TILE
    qk_w = GLA_HEADS * (d // 16)
    v_w = GLA_HEADS * (d // 8)
    m_w = MOBA_HEADS * MOBA_DH
    offs = np.cumsum([0, qk_w, qk_w, v_w, GLA_LOWRANK, v_w, m_w, m_w, m_w, d, d])
    bias = _bias_tables(rel_bias)
    h = x.reshape(t, d)
    c_act_in = c
    for l in range(depth):
        mod4 = _adaln(c_act_in, w_ada, b_ada, l).reshape(bsz, 3, 3, d)
        cast = lambda w: w.astype(BF16)
        h = _ffn(h, mod4, 0, norm_ff1[l].reshape(1, d), w_ff1_gate[l], w_ff1_up[l], w_ff1_down[l], seq)
        wi = w_in[l]
        seg = lambda i: wi[:, offs[i]:offs[i + 1]]
        lr_pad = jnp.pad(seg(3), ((0, 0), (0, 128 - GLA_LOWRANK)))
        wp = cast(jnp.concatenate([seg(0), seg(1), seg(2), seg(4), seg(5), seg(6), lr_pad], axis=1))
        wvt = cast(seg(7).T)
        wlr = cast(jnp.pad(w_gla_lr[l], ((0, 128 - GLA_LOWRANK), (0, 0))))
        o_a, mq, mk, mvt, kmean = _inproj(
            h, mod4, norm_mix[l].reshape(1, d), wp, wvt, wlr, b_gla_lr[l].reshape(1, qk_w),
            gla_norm[l].reshape(1, -1), bsz, seq)
        o_b = _moba(mq, mk, mvt, kmean.reshape(bsz, seq // MOBA_BLOCK, m_w), bias, bsz, seq)
        h = _merge(h, mod4, norm_mix[l].reshape(1, d), o_a, o_b, cast(seg(8)), cast(seg(9)),
                   cast(w_br_gla[l]), cast(w_br_moba[l]), cast(w_out[l]), tiles_per_batch)
        last = l == depth - 1
        h = _ffn(h, mod4, 2, norm_ff2[l].reshape(1, d), w_ff2_gate[l], w_ff2_up[l], w_ff2_down[l], seq,
                 norm_final.reshape(1, d) if last else None)
    return h.reshape(bsz, seq, d)
```

```python
import functools
import math

import numpy as np
import jax
import jax.numpy as jnp
from jax import lax
from jax.experimental import pallas as pl
from jax.experimental.pallas import tpu as pltpu

F32 = jnp.float32
BF16 = jnp.bfloat16

EPS = 1e-6
FFN_RES = 0.5
GLA_HEADS = 4
GLA_TAU = 16.0
GLA_LOWRANK = 16
GLA_CHUNK = 128
MOBA_HEADS = 4
MOBA_DH = 128
MOBA_BLOCK = 256
MOBA_TOPK = 3
MOBA_VROWS = MOBA_DH + 16
MOBA_ROWS = 2
MOBA_BODIES = (8, 4, 2)
NUM_BUCKETS = 32
MAX_DISTANCE = 128
MAX_EXACT = NUM_BUCKETS // 2
NEG = -1e30
LOG2E = 1.4426950408889634
MOBA_QSCALE = LOG2E * MOBA_DH ** -0.5
TOKEN_TILE = 1024
FFN_TILE = 1024
INPROJ_TILE = 512
FF_CHUNK = 256
V7X_VMEM_BYTES = 64 * 1024 * 1024
VMEM_LIMIT = V7X_VMEM_BYTES - 3 * 1024 * 1024


def _dot(a, b):
    return jnp.dot(a, b, preferred_element_type=F32)


def _dot_nt(a, b):
    return lax.dot_general(a, b, (((1,), (1,)), ((), ())), preferred_element_type=F32)


def _dot_tn(a, b):
    return lax.dot_general(a, b, (((0,), (0,)), ((), ())), preferred_element_type=F32)


def _sigmoid(x):
    return 1.0 / (1.0 + jnp.exp(-x))


def _rmsnorm(x, w):
    return x * lax.rsqrt(jnp.mean(x * x, axis=-1, keepdims=True) + EPS) * w


def _norm_mod(x, w, mod):
    return _rmsnorm(x, w) * (1.0 + mod[1:2, :]) + mod[0:1, :]


def _params(sem):
    return pltpu.CompilerParams(dimension_semantics=sem, vmem_limit_bytes=VMEM_LIMIT)


def _const_spec(shape):
    nd = len(shape)
    return pl.BlockSpec(shape, lambda *_: (0,) * nd, pipeline_mode=pl.Buffered(1))


def _adaln_kernel(c_ref, w_ref, b_ref, o_ref):
    c = c_ref[...]
    ca = (c * _sigmoid(c)).astype(BF16)
    o_ref[...] = _dot(ca, w_ref[...].astype(BF16)) + b_ref[...]


def _adaln(c, w_ada, b_ada, layer):
    bsz, d = c.shape
    depth, _, n = w_ada.shape
    return pl.pallas_call(
        _adaln_kernel,
        grid=(n // d,),
        in_specs=[pl.BlockSpec((bsz, d), lambda j: (0, 0)),
                  pl.BlockSpec((None, d, d), lambda j: (layer, 0, j)),
                  pl.BlockSpec((None, 1, d), lambda j: (layer, 0, j))],
        out_specs=pl.BlockSpec((bsz, d), lambda j: (0, j)),
        out_shape=jax.ShapeDtypeStruct((bsz, n), F32),
        compiler_params=_params(("arbitrary",)),
        name="adaln",
    )(c, w_ada, b_ada.reshape(depth, 1, n))


def _ffn_kernel(x_ref, mod_ref, nw_ref, wg_ref, wu_ref, wd_ref, *rest, final_norm):
    if final_norm:
        nf_ref, o_ref, u_scr, a_scr = rest
    else:
        o_ref, u_scr, a_scr = rest
    mod = mod_ref[0, 0]
    u_scr[...] = _norm_mod(x_ref[...], nw_ref[...], mod).astype(BF16)
    d_ff = wg_ref.shape[1]
    for c in range(d_ff // FF_CHUNK):
        sl = slice(c * FF_CHUNK, (c + 1) * FF_CHUNK)
        g = _dot(u_scr[...], wg_ref[:, sl].astype(BF16))
        up = _dot(u_scr[...], wu_ref[:, sl].astype(BF16))
        a_scr[:, sl] = (g * _sigmoid(g) * up).astype(BF16)
    y = _dot(a_scr[...], wd_ref[...].astype(BF16))
    out = x_ref[...] + (FFN_RES * mod[2:3, :]) * y
    if final_norm:
        out = _rmsnorm(out, nf_ref[...])
    o_ref[...] = out


def _ffn(h, mod4, layer, norm_w, wg, wu, wd, seq, norm_final=None):
    t, d = h.shape
    d_ff = wg.shape[1]
    tm = FFN_TILE
    tiles_per_batch = seq // tm
    in_specs = [pl.BlockSpec((tm, d), lambda i: (i, 0)),
                pl.BlockSpec((1, 1, 3, d), lambda i: (i // tiles_per_batch, layer, 0, 0)),
                _const_spec((1, d)), _const_spec((d, d_ff)), _const_spec((d, d_ff)), _const_spec((d_ff, d))]
    args = [h, mod4, norm_w, wg, wu, wd]
    if norm_final is not None:
        in_specs.append(_const_spec((1, d)))
        args.append(norm_final)
    return pl.pallas_call(
        functools.partial(_ffn_kernel, final_norm=norm_final is not None),
        grid=(t // tm,),
        in_specs=in_specs,
        out_specs=pl.BlockSpec((tm, d), lambda i: (i, 0)),
        out_shape=jax.ShapeDtypeStruct((t, d), F32),
        scratch_shapes=[pltpu.VMEM((tm, d), BF16), pltpu.VMEM((tm, d_ff), BF16)],
        compiler_params=_params(("arbitrary",)),
        name="ffn_final" if norm_final is not None else "ffn",
    )(*args)


_P_GQ, _P_GK, _P_GV, _P_GOG, _P_MQ, _P_MK, _P_LR, _P_END = 0, 256, 512, 1024, 1536, 2048, 2560, 2688


def _inproj_kernel(h_ref, mod_ref, nw_ref, wp_ref, wvt_ref, wlr_ref, blr_ref, gnw_ref,
                   oa_ref, mq_ref, mk_ref, mvt_ref, kmean_ref,
                   u_scr, gq_scr, gk_scr, gv_scr, gog_scr, dec_scr, st_scr, tri_scr, lev_scr, b_scr, w_scr,
                   *, tiles_per_batch):
    i = pl.program_id(0)

    @pl.when(i == 0)
    def _():
        _gla_tables(tri_scr, lev_scr)

    @pl.when(lax.rem(i, tiles_per_batch) == 0)
    def _():
        st_scr[...] = jnp.zeros_like(st_scr)

    u_scr[...] = _norm_mod(h_ref[...], nw_ref[...], mod_ref[0, 0]).astype(BF16)

    def proj(lo, hi):
        return _dot(u_scr[...], wp_ref[:, lo:hi])

    glr = proj(_P_LR, _P_END).astype(BF16)
    z = _dot(glr, wlr_ref[...]) + blr_ref[...]
    g = (jnp.minimum(z, 0.0) - jnp.log1p(jnp.exp(-jnp.abs(z)))) * (LOG2E / GLA_TAU)
    g_hi = g.astype(BF16)
    dec_scr[:, :g.shape[1]] = g_hi
    dec_scr[:, g.shape[1]:] = (g - g_hi.astype(F32)).astype(BF16)
    cumsum, prepare, chunk = _gla_chunk_fns(gq_scr, gk_scr, gv_scr, gog_scr, dec_scr, gnw_ref, oa_ref,
                                            st_scr, tri_scr, lev_scr, b_scr, w_scr)
    n_chunks = h_ref.shape[0] // GLA_CHUNK
    for c in range(n_chunks):
        cumsum(c * GLA_CHUNK, c)
        prepare(c)

    gq_scr[...] = proj(_P_GQ, _P_GK) * 0.125
    gk_scr[...] = proj(_P_GK, _P_GV)
    gv_scr[...] = proj(_P_GV, _P_GOG).astype(BF16)
    gog_scr[...] = proj(_P_GOG, _P_MQ)
    mq_ref[...] = (proj(_P_MQ, _P_MK) * MOBA_QSCALE).astype(BF16)
    mk = proj(_P_MK, _P_LR)
    mk_ref[...] = mk.astype(BF16)
    kmean_ref[0] = jnp.mean(mk.reshape(-1, MOBA_BLOCK, mk.shape[1]), axis=1)
    mvt = _dot_nt(wvt_ref[...], u_scr[...])
    ones = jnp.ones((MOBA_VROWS - MOBA_DH, MOBA_BLOCK), BF16)
    for h in range(MOBA_HEADS):
        for j in range(mvt.shape[1] // MOBA_BLOCK):
            mvt_ref[0, h, j, :MOBA_DH, :] = mvt[h * MOBA_DH:(h + 1) * MOBA_DH,
                                                j * MOBA_BLOCK:(j + 1) * MOBA_BLOCK].astype(BF16)
            mvt_ref[0, h, j, MOBA_DH:, :] = ones

    for c in range(n_chunks):
        chunk(c * GLA_CHUNK, c)


def _inproj(h, mod4, norm_w, wp, wvt, wlr, blr, gla_norm, bsz, seq):
    t, d = h.shape
    tm = INPROJ_TILE
    tiles_per_batch = seq // tm
    nblk = tm // MOBA_BLOCK
    nb = seq // MOBA_BLOCK
    qk_w, v_w = _P_GK - _P_GQ, _P_GOG - _P_GV
    row = lambda i: (i, 0)
    out_shape = [jax.ShapeDtypeStruct((t, v_w), BF16), jax.ShapeDtypeStruct((t, 512), BF16),
                 jax.ShapeDtypeStruct((t, 512), BF16),
                 jax.ShapeDtypeStruct((bsz, MOBA_HEADS, nb, MOBA_VROWS, MOBA_BLOCK), BF16),
                 jax.ShapeDtypeStruct((t // tm, nblk, 512), F32)]
    out_specs = [pl.BlockSpec((tm, v_w), row), pl.BlockSpec((tm, 512), row), pl.BlockSpec((tm, 512), row),
                 pl.BlockSpec((1, MOBA_HEADS, nblk, MOBA_VROWS, MOBA_BLOCK),
                              lambda i: (i // tiles_per_batch, 0, i % tiles_per_batch, 0, 0)),
                 pl.BlockSpec((1, nblk, 512), lambda i: (i, 0, 0))]
    return pl.pallas_call(
        functools.partial(_inproj_kernel, tiles_per_batch=tiles_per_batch),
        grid=(t // tm,),
        in_specs=[pl.BlockSpec((tm, d), row),
                  pl.BlockSpec((1, 1, 3, d), lambda i: (i // tiles_per_batch, 1, 0, 0)),
                  _const_spec((1, d)), _const_spec(wp.shape), _const_spec(wvt.shape),
                  _const_spec(wlr.shape), _const_spec(blr.shape), _const_spec(gla_norm.shape)],
        out_specs=out_specs,
        out_shape=out_shape,
        scratch_shapes=[pltpu.VMEM((tm, d), BF16),
                        pltpu.VMEM((tm, qk_w), F32), pltpu.VMEM((tm, qk_w), F32),
                        pltpu.VMEM((tm, v_w), BF16), pltpu.VMEM((tm, v_w), F32),
                        pltpu.VMEM((tm, 2 * qk_w), BF16),
                        pltpu.VMEM((GLA_HEADS // 2, 2 * v_w // GLA_HEADS, 2 * qk_w // GLA_HEADS), F32),
                        pltpu.VMEM((GLA_CHUNK, GLA_CHUNK), BF16),
                        pltpu.VMEM((GLA_CHUNK, 2 * GLA_CHUNK), jnp.int32),
                        pltpu.VMEM((tm // GLA_CHUNK, GLA_CHUNK, qk_w), F32),
                        pltpu.VMEM((tm // GLA_CHUNK, len(_GLA_LEVELS) + 2, GLA_CHUNK, qk_w), F32)],
        compiler_params=_params(("arbitrary",)),
        name="inproj",
    )(h, mod4, norm_w, wp, wvt, wlr, blr, gla_norm)


_GLA_LEVELS = [1 << p for p in range(int(math.log2(GLA_CHUNK)))]


def _gla_tables(tri_scr, lev_scr):
    L = GLA_CHUNK
    tri_scr[...] = (lax.broadcasted_iota(jnp.int32, (L, L), 1)
                    <= lax.broadcasted_iota(jnp.int32, (L, L), 0)).astype(BF16)
    ti = lax.broadcasted_iota(jnp.int32, (L, 2 * L), 0)
    tj = lax.broadcasted_iota(jnp.int32, (L, 2 * L), 1) & (L - 1)
    xor = ti ^ tj
    lev = jnp.where(tj > ti, -1, 0)
    for p in range(len(_GLA_LEVELS)):
        lev = jnp.where((xor >= (1 << p)) & (tj < ti), p + 1, lev)
    lev_scr[...] = lev


def _gla_chunk_fns(q_ref, k_ref, v_ref, og_ref, g_ref, nw_ref, o_ref, st_scr, tri_scr, lev_scr, b_scr, w_scr):
    L = GLA_CHUNK
    hq = q_ref.shape[1] // GLA_HEADS
    hv = v_ref.shape[1] // GLA_HEADS
    wq = q_ref.shape[1]
    pairs = GLA_HEADS // 2
    assert 2 * hq == 128 and hv == 128 and L == 128
    levels = _GLA_LEVELS

    sub = lax.broadcasted_iota(jnp.int32, (L, wq), 0) & 7
    even_head = lax.broadcasted_iota(jnp.int32, (L, 2 * hq), 1) < hq
    stbd = ((lax.broadcasted_iota(jnp.int32, (2 * hv, 2 * hq), 0) < hv)
            == (lax.broadcasted_iota(jnp.int32, (2 * hv, 2 * hq), 1) < hq))

    def rows_bcast(slot, first, period):
        return jnp.concatenate([jnp.broadcast_to(b_scr[slot, r:r + 1, :], (period, wq))
                                for r in range(first, L, period)], axis=0)

    def midpoint(slot, s):
        if s >= 4:
            return rows_bcast(slot, s - 1, 2 * s)
        if s == 2:
            return jnp.where(sub < 4, rows_bcast(slot, 1, 8), rows_bcast(slot, 5, 8))
        return jnp.where(sub < 2, rows_bcast(slot, 0, 8),
                         jnp.where(sub < 4, rows_bcast(slot, 2, 8),
                                   jnp.where(sub < 6, rows_bcast(slot, 4, 8), rows_bcast(slot, 6, 8))))

    def cumsum(r0, slot):
        bb = _dot(tri_scr[...], g_ref[pl.ds(r0, L), :])
        b_scr[slot] = bb[:, :wq] + bb[:, wq:]

    def pair_scores(qx, kx, p):
        pl_ = slice(p * 2 * hq, (p + 1) * 2 * hq)
        kp = kx[:, pl_]
        kk = jnp.concatenate([jnp.where(even_head, kp, 0.0), jnp.where(even_head, 0.0, kp)], axis=0)
        return _dot_nt(qx[:, pl_].astype(BF16), kk.astype(BF16))

    def prepare(slot):
        b = b_scr[slot]
        for li, s in enumerate(levels):
            d = lax.bitcast_convert_type(b - midpoint(slot, s), jnp.uint32) | jnp.uint32(0x80000000)
            w_scr[slot, li] = jnp.exp2(lax.bitcast_convert_type(d, F32))
        w_scr[slot, len(levels)] = jnp.exp2(b)
        w_scr[slot, len(levels) + 1] = jnp.exp2(b[L - 1:L, :] - b)

    def chunk(r0, slot):
        q = q_ref[pl.ds(r0, L), :]
        k = k_ref[pl.ds(r0, L), :]
        v = v_ref[pl.ds(r0, L), :]
        q_in = (q * w_scr[slot, len(levels)]).astype(BF16)
        k_out = (k * w_scr[slot, len(levels) + 1]).astype(BF16)
        dec = jnp.exp2(b_scr[slot, L - 1:L, :])

        att = [jnp.where(lev_scr[...] == 0, pair_scores(q, k, p), 0.0) for p in range(pairs)]
        for li in range(len(levels)):
            w = w_scr[slot, li]
            hit = lev_scr[...] == li + 1
            att = [jnp.where(hit, pair_scores(q * w, k * w, p), att[p]) for p in range(pairs)]

        for p in range(pairs):
            ql = slice(p * 2 * hq, (p + 1) * 2 * hq)
            vl = slice(p * 2 * hv, (p + 1) * 2 * hv)
            st = st_scr[p]
            vp = v[:, vl]
            ab = att[p].astype(BF16)
            o = _dot_nt(q_in[:, ql], st.astype(BF16)) + jnp.concatenate(
                [_dot(ab[:, :L], vp[:, :hv]), _dot(ab[:, L:], vp[:, hv:])], axis=1)
            st_scr[p] = st * dec[:, ql] + jnp.where(stbd, _dot_tn(vp, k_out[:, ql]), 0.0)
            for hh in range(2):
                sl = slice(vl.start + hh * hv, vl.start + (hh + 1) * hv)
                gate = og_ref[pl.ds(r0, L), sl]
                oh = _rmsnorm(o[:, hh * hv:(hh + 1) * hv], nw_ref[...])
                o_ref[pl.ds(r0, L), sl] = (oh * (gate * _sigmoid(gate))).astype(BF16)

    return cumsum, prepare, chunk


def _rel_buckets(max_dist):
    n = np.arange(max_dist)
    nf = np.maximum(n, 1).astype(np.float64)
    large = MAX_EXACT + (np.log(nf / MAX_EXACT) / math.log(MAX_DISTANCE / MAX_EXACT)
                         * (NUM_BUCKETS - MAX_EXACT)).astype(np.int64)
    return np.where(n < MAX_EXACT, n, np.minimum(large, NUM_BUCKETS - 1))


_TBL_OWN, _TBL_PREV, _TBL_FAR = 0, 1, 2


def _bias_kernel(rb_ref, o_ref):
    h = pl.program_id(0)
    blk = MOBA_BLOCK
    buckets = _rel_buckets(2 * blk)
    assert (np.diff(buckets) >= 0).all()
    starts = {b: int(np.argmax(buckets == b)) for b in range(NUM_BUCKETS) if (buckets == b).any()}
    ki = lax.broadcasted_iota(jnp.int32, (blk, blk), 0)
    qi = lax.broadcasted_iota(jnp.int32, (blk, blk), 1)
    far = rb_ref[h, NUM_BUCKETS - 1]
    for tbl, base in ((_TBL_OWN, 0), (_TBL_PREV, blk)):
        dist = qi - ki + base
        val = jnp.full((blk, blk), rb_ref[h, 0], F32)
        for b in sorted(starts):
            if b > 0:
                val = jnp.where(dist >= starts[b], rb_ref[h, b], val)
        val = (val - far) * LOG2E
        if tbl == _TBL_OWN:
            val = jnp.where(dist >= 0, val, NEG)
        o_ref[0, tbl] = val
    o_ref[0, _TBL_FAR] = jnp.zeros((blk, blk), F32)


def _bias_tables(rel_bias):
    nh = rel_bias.shape[0]
    return pl.pallas_call(
        _bias_kernel,
        grid=(nh,),
        in_specs=[pl.BlockSpec(memory_space=pltpu.SMEM)],
        out_specs=pl.BlockSpec((1, 3, MOBA_BLOCK, MOBA_BLOCK), lambda h: (h, 0, 0, 0)),
        out_shape=jax.ShapeDtypeStruct((nh, 3, MOBA_BLOCK, MOBA_BLOCK), F32),
        compiler_params=_params(("arbitrary",)),
        name="moba_bias",
    )(rel_bias)


def _moba_kernel(q_ref, k_ref, vt_ref, km_ref, bias_ref, o_ref,
                 off_scr, m_scr, al_scr, acc_scr, p_scr, s_scr):
    n = pl.program_id(1)
    blk = MOBA_BLOCK
    nb = km_ref.shape[1]
    heads = range(q_ref.shape[0] * MOBA_HEADS)
    row = [h // MOBA_HEADS for h in heads]
    hsl = [slice((h % MOBA_HEADS) * MOBA_DH, (h % MOBA_HEADS + 1) * MOBA_DH) for h in heads]

    brow = lax.broadcasted_iota(jnp.int32, (nb, len(heads) * blk), 0).astype(F32)
    gate = jnp.concatenate([_dot_nt(km_ref[row[h], :, hsl[h]].astype(BF16), q_ref[row[h], :, hsl[h]])
                            for h in heads], axis=1)
    gate = jnp.where(brow < n.astype(F32), gate, NEG)
    off = jnp.full(gate.shape, NEG, F32)
    for _ in range(MOBA_TOPK):
        best = jnp.max(gate, axis=0, keepdims=True)
        first = jnp.min(jnp.where(gate == best, brow, float(nb)), axis=0, keepdims=True)
        pick = (brow == first) & (best > NEG)
        off = jnp.where(pick, 0.0, off)
        gate = jnp.where(pick, NEG, gate)
    for h in heads:
        off_scr[h] = off[:, h * blk:(h + 1) * blk]
        m_scr[h] = jnp.full((1, blk), NEG, F32)
        acc_scr[h] = jnp.zeros((MOBA_VROWS, blk), F32)
        al_scr[1, h] = jnp.ones((1, blk), F32)
        p_scr[1, h] = jnp.zeros((blk, blk), BF16)

    def qk(h, j, slot):
        r0 = pl.multiple_of(j * blk, blk)
        s_scr[slot, h] = _dot_nt(k_ref[row[h], pl.ds(r0, blk), hsl[h]], q_ref[row[h], :, hsl[h]])

    def pv(h, j, slot):
        acc_scr[h] = al_scr[slot, h] * acc_scr[h] + _dot(vt_ref[row[h], h % MOBA_HEADS, j], p_scr[slot, h])

    def step(slot, j, j_next, j_prev, table, live):
        for h in heads:
            pv(h, j_prev, 1 - slot)
            if j_next is not None:
                qk(h, j_next, 1 - slot)
        for h in heads:
            sh = s_scr[slot, h]
            if table is not None:
                sh = sh + bias_ref[h % MOBA_HEADS, table]
            top = jnp.max(sh, axis=0, keepdims=True)
            m_old = m_scr[h]
            if live is not None:
                off = jnp.where(live, off_scr[h, pl.ds(j, 1), :], NEG)
                m_new = jnp.maximum(m_old, top + off)
                m_sub = jnp.where(off < 0.0, -NEG, m_new)
            else:
                m_new = jnp.maximum(m_old, top)
                m_sub = m_new
            m_scr[h] = m_new
            al_scr[slot, h] = jnp.exp2(m_old - m_new)
            p_scr[slot, h] = jnp.exp2(sh - m_sub).astype(BF16)

    n_far = jnp.maximum(n - 1, 0)
    last_far = jnp.maximum(n_far - 1, 0)
    j_before = jnp.maximum(n - 1, 0)
    far_blk = lambda u: jnp.clip(u, 0, last_far)

    for h in heads:
        qk(h, jnp.where(n_far > 0, 0, j_before), 0)

    def far_steps(u, count):
        for c in range(count):
            nxt = far_blk(u + c + 1)
            if c == count - 1:
                nxt = jnp.where(u + count < n_far, nxt, j_before)
            step(c % 2, far_blk(u + c), nxt, far_blk(u + c - 1), None, u + c < n_far)

    done = 0
    for size in MOBA_BODIES:
        left = n_far - done
        trips = (left + 1) // 2 if size == 2 else left // size

        def body(i, carry, size=size, done=done):
            far_steps(done + size * i, size)
            return carry

        lax.fori_loop(0, trips, body, 0)
        done = done + trips * size
    step(0, j_before, n, last_far, _TBL_PREV, n >= 1)
    step(1, n, None, j_before, _TBL_OWN, None)
    for h in heads:
        pv(h, n, 1)
        acc = acc_scr[h]
        o_ref[row[h], :, hsl[h]] = (acc[:MOBA_DH] / acc[MOBA_DH:MOBA_DH + 1]).T.astype(BF16)


def _moba(mq, mk, mvt, kmean, bias, bsz, seq):
    t, w = mq.shape
    blk = MOBA_BLOCK
    nb = seq // blk
    g = MOBA_ROWS if bsz % MOBA_ROWS == 0 else 1
    units = g * MOBA_HEADS
    resident = lambda shape: pl.BlockSpec(shape, lambda b, n: (b,) + (0,) * (len(shape) - 1),
                                          pipeline_mode=pl.Buffered(1))
    out = pl.pallas_call(
        _moba_kernel,
        grid=(bsz // g, nb),
        in_specs=[pl.BlockSpec((g, blk, w), lambda b, n: (b, n, 0)),
                  resident((g, seq, w)),
                  resident((g, MOBA_HEADS, nb, MOBA_VROWS, blk)),
                  pl.BlockSpec((g, nb, w), lambda b, n: (b, 0, 0)),
                  _const_spec(bias.shape)],
        out_specs=pl.BlockSpec((g, blk, w), lambda b, n: (b, n, 0)),
        out_shape=jax.ShapeDtypeStruct((bsz, seq, w), BF16),
        scratch_shapes=[pltpu.VMEM((units, nb, blk), F32),
                        pltpu.VMEM((units, 1, blk), F32),
                        pltpu.VMEM((2, units, 1, blk), F32),
                        pltpu.VMEM((units, MOBA_VROWS, blk), F32),
                        pltpu.VMEM((2, units, blk, blk), BF16),
                        pltpu.VMEM((2, units, blk, blk), F32)],
        compiler_params=_params(("arbitrary", "arbitrary")),
        name="moba",
    )(mq.reshape(bsz, seq, w), mk.reshape(bsz, seq, w), mvt, kmean, bias)
    return out.reshape(t, w)


def _merge_kernel(h_ref, mod_ref, nw_ref, oa_ref, ob_ref, wga_ref, wgb_ref, wa_ref, wb_ref, wo_ref,
                  o_ref, u_scr):
    mod = mod_ref[0, 0]
    u_scr[...] = _norm_mod(h_ref[...], nw_ref[...], mod).astype(BF16)
    ya = _dot(oa_ref[...], wa_ref[...])
    yb = _dot(ob_ref[...], wb_ref[...])
    merged = _sigmoid(_dot(u_scr[...], wga_ref[...])) * ya + _sigmoid(_dot(u_scr[...], wgb_ref[...])) * yb
    o_ref[...] = h_ref[...] + mod[2:3, :] * _dot(merged.astype(BF16), wo_ref[...])


def _merge(h, mod4, norm_w, oa, ob, wga, wgb, wa, wb, wo, tiles_per_batch):
    t, d = h.shape
    tm = TOKEN_TILE
    row = lambda i: (i, 0)
    return pl.pallas_call(
        _merge_kernel,
        grid=(t // tm,),
        in_specs=[pl.BlockSpec((tm, d), row),
                  pl.BlockSpec((1, 1, 3, d), lambda i: (i // tiles_per_batch, 1, 0, 0)),
                  _const_spec((1, d)),
                  pl.BlockSpec((tm, oa.shape[1]), row), pl.BlockSpec((tm, ob.shape[1]), row),
                  _const_spec(wga.shape), _const_spec(wgb.shape), _const_spec(wa.shape),
                  _const_spec(wb.shape), _const_spec(wo.shape)],
        out_specs=pl.BlockSpec((tm, d), row),
        out_shape=jax.ShapeDtypeStruct((t, d), F32),
        scratch_shapes=[pltpu.VMEM((tm, d), BF16)],
        compiler_params=_params(("arbitrary",)),
        name="merge",
    )(h, mod4, norm_w, oa, ob, wga, wgb, wa, wb, wo)


def kernel(x, c, w_ada, b_ada, norm_ff1, w_ff1_gate, w_ff1_up, w_ff1_down, norm_mix, w_in, w_gla_lr,
           b_gla_lr, gla_norm, rel_bias, w_br_gla, w_br_moba, w_out, norm_ff2, w_ff2_gate, w_ff2_up,
           w_ff2_down, norm_final):
    bsz, seq, d = x.shape
    depth = w_ada.shape[0]
    t = bsz * seq
    tiles_per_batch = seq // TOKEN_TILE
    qk_w = GLA_HEADS * (d // 16)
    v_w = GLA_HEADS * (d // 8)
    m_w = MOBA_HEADS * MOBA_DH
    offs = np.cumsum([0, qk_w, qk_w, v_w, GLA_LOWRANK, v_w, m_w, m_w, m_w, d, d])
    bias = _bias_tables(rel_bias)
    h = x.reshape(t, d)
    c_act_in = c
    for l in range(depth):
        mod4 = _adaln(c_act_in, w_ada, b_ada, l).reshape(bsz, 3, 3, d)
        cast = lambda w: w.astype(BF16)
        h = _ffn(h, mod4, 0, norm_ff1[l].reshape(1, d), w_ff1_gate[l], w_ff1_up[l], w_ff1_down[l], seq)
        wi = w_in[l]
        seg = lambda i: wi[:, offs[i]:offs[i + 1]]
        lr_pad = jnp.pad(seg(3), ((0, 0), (0, 128 - GLA_LOWRANK)))
        wp = cast(jnp.concatenate([seg(0), seg(1), seg(2), seg(4), seg(5), seg(6), lr_pad], axis=1))
        wvt = cast(seg(7).T)
        wlr = cast(jnp.pad(w_gla_lr[l], ((0, 128 - GLA_LOWRANK), (0, 0))))
        o_a, mq, mk, mvt, kmean = _inproj(
            h, mod4, norm_mix[l].reshape(1, d), wp, wvt, wlr, b_gla_lr[l].reshape(1, qk_w),
            gla_norm[l].reshape(1, -1), bsz, seq)
        o_b = _moba(mq, mk, mvt, kmean.reshape(bsz, seq // MOBA_BLOCK, m_w), bias, bsz, seq)
        h = _merge(h, mod4, norm_mix[l].reshape(1, d), o_a, o_b, cast(seg(8)), cast(seg(9)),
                   cast(w_br_gla[l]), cast(w_br_moba[l]), cast(w_out[l]), tiles_per_batch)
        last = l == depth - 1
        h = _ffn(h, mod4, 2, norm_ff2[l].reshape(1, d), w_ff2_gate[l], w_ff2_up[l], w_ff2_down[l], seq,
                 norm_final.reshape(1, d) if last else None)
    return h.reshape(bsz, seq, d)
```

```python
import functools
import math

import numpy as np
import jax
import jax.numpy as jnp
from jax import lax
from jax.experimental import pallas as pl
from jax.experimental.pallas import tpu as pltpu

F32 = jnp.float32
BF16 = jnp.bfloat16

EPS = 1e-6
FFN_RES = 0.5
GLA_HEADS = 4
GLA_TAU = 16.0
GLA_LOWRANK = 16
GLA_CHUNK = 128
MOBA_HEADS = 4
MOBA_DH = 128
MOBA_BLOCK = 256
MOBA_TOPK = 3
MOBA_VROWS = MOBA_DH + 16
MOBA_ROWS = 2
MOBA_BODIES = (8, 4, 2)
NUM_BUCKETS = 32
MAX_DISTANCE = 128
MAX_EXACT = NUM_BUCKETS // 2
NEG = -1e30
LOG2E = 1.4426950408889634
MOBA_QSCALE = LOG2E * MOBA_DH ** -0.5
TOKEN_TILE = 1024
FFN_TILE = 1024
INPROJ_TILE = 1024
FF_CHUNK = 256
V7X_VMEM_BYTES = 64 * 1024 * 1024
VMEM_LIMIT = V7X_VMEM_BYTES - 3 * 1024 * 1024


def _dot(a, b):
    return jnp.dot(a, b, preferred_element_type=F32)


def _dot_nt(a, b):
    return lax.dot_general(a, b, (((1,), (1,)), ((), ())), preferred_element_type=F32)


def _dot_tn(a, b):
    return lax.dot_general(a, b, (((0,), (0,)), ((), ())), preferred_element_type=F32)


def _sigmoid(x):
    return 1.0 / (1.0 + jnp.exp(-x))


def _rmsnorm(x, w):
    return x * lax.rsqrt(jnp.mean(x * x, axis=-1, keepdims=True) + EPS) * w


def _norm_mod(x, w, mod):
    return _rmsnorm(x, w) * (1.0 + mod[1:2, :]) + mod[0:1, :]


def _params(sem):
    return pltpu.CompilerParams(dimension_semantics=sem, vmem_limit_bytes=VMEM_LIMIT)


def _const_spec(shape):
    nd = len(shape)
    return pl.BlockSpec(shape, lambda *_: (0,) * nd, pipeline_mode=pl.Buffered(1))


def _adaln_kernel(c_ref, w_ref, b_ref, o_ref):
    c = c_ref[...]
    ca = (c * _sigmoid(c)).astype(BF16)
    o_ref[...] = _dot(ca, w_ref[...].astype(BF16)) + b_ref[...]


def _adaln(c, w_ada, b_ada, layer):
    bsz, d = c.shape
    depth, _, n = w_ada.shape
    return pl.pallas_call(
        _adaln_kernel,
        grid=(n // d,),
        in_specs=[pl.BlockSpec((bsz, d), lambda j: (0, 0)),
                  pl.BlockSpec((None, d, d), lambda j: (layer, 0, j)),
                  pl.BlockSpec((None, 1, d), lambda j: (layer, 0, j))],
        out_specs=pl.BlockSpec((bsz, d), lambda j: (0, j)),
        out_shape=jax.ShapeDtypeStruct((bsz, n), F32),
        compiler_params=_params(("arbitrary",)),
        name="adaln",
    )(c, w_ada, b_ada.reshape(depth, 1, n))


def _ffn_kernel(x_ref, mod_ref, nw_ref, wg_ref, wu_ref, wd_ref, *rest, final_norm):
    if final_norm:
        nf_ref, o_ref, u_scr, a_scr = rest
    else:
        o_ref, u_scr, a_scr = rest
    mod = mod_ref[0, 0]
    u_scr[...] = _norm_mod(x_ref[...], nw_ref[...], mod).astype(BF16)
    d_ff = wg_ref.shape[1]
    for c in range(d_ff // FF_CHUNK):
        sl = slice(c * FF_CHUNK, (c + 1) * FF_CHUNK)
        g = _dot(u_scr[...], wg_ref[:, sl].astype(BF16))
        up = _dot(u_scr[...], wu_ref[:, sl].astype(BF16))
        a_scr[:, sl] = (g * _sigmoid(g) * up).astype(BF16)
    y = _dot(a_scr[...], wd_ref[...].astype(BF16))
    out = x_ref[...] + (FFN_RES * mod[2:3, :]) * y
    if final_norm:
        out = _rmsnorm(out, nf_ref[...])
    o_ref[...] = out


def _ffn(h, mod4, layer, norm_w, wg, wu, wd, seq, norm_final=None):
    t, d = h.shape
    d_ff = wg.shape[1]
    tm = FFN_TILE
    tiles_per_batch = seq // tm
    in_specs = [pl.BlockSpec((tm, d), lambda i: (i, 0)),
                pl.BlockSpec((1, 1, 3, d), lambda i: (i // tiles_per_batch, layer, 0, 0)),
                _const_spec((1, d)), _const_spec((d, d_ff)), _const_spec((d, d_ff)), _const_spec((d_ff, d))]
    args = [h, mod4, norm_w, wg, wu, wd]
    if norm_final is not None:
        in_specs.append(_const_spec((1, d)))
        args.append(norm_final)
    return pl.pallas_call(
        functools.partial(_ffn_kernel, final_norm=norm_final is not None),
        grid=(t // tm,),
        in_specs=in_specs,
        out_specs=pl.BlockSpec((tm, d), lambda i: (i, 0)),
        out_shape=jax.ShapeDtypeStruct((t, d), F32),
        scratch_shapes=[pltpu.VMEM((tm, d), BF16), pltpu.VMEM((tm, d_ff), BF16)],
        compiler_params=_params(("arbitrary",)),
        name="ffn_final" if norm_final is not None else "ffn",
    )(*args)


_P_GQ, _P_GK, _P_GV, _P_GOG, _P_MQ, _P_MK, _P_LR, _P_END = 0, 256, 512, 1024, 1536, 2048, 2560, 2688


def _inproj_kernel(h_ref, mod_ref, nw_ref, wp_ref, wvt_ref, wlr_ref, blr_ref, gnw_ref,
                   oa_ref, mq_ref, mk_ref, mvt_ref, kmean_ref,
                   u_scr, gq_scr, gk_scr, gv_scr, gog_scr, dec_scr, st_scr, tri_scr, lev_scr, b_scr, w_scr,
                   *, tiles_per_batch):
    i = pl.program_id(0)

    @pl.when(i == 0)
    def _():
        _gla_tables(tri_scr, lev_scr)

    @pl.when(lax.rem(i, tiles_per_batch) == 0)
    def _():
        st_scr[...] = jnp.zeros_like(st_scr)

    u_scr[...] = _norm_mod(h_ref[...], nw_ref[...], mod_ref[0, 0]).astype(BF16)

    def proj(lo, hi):
        return _dot(u_scr[...], wp_ref[:, lo:hi])

    glr = proj(_P_LR, _P_END).astype(BF16)
    z = _dot(glr, wlr_ref[...]) + blr_ref[...]
    g = (jnp.minimum(z, 0.0) - jnp.log1p(jnp.exp(-jnp.abs(z)))) * (LOG2E / GLA_TAU)
    g_hi = g.astype(BF16)
    dec_scr[:, :g.shape[1]] = g_hi
    dec_scr[:, g.shape[1]:] = (g - g_hi.astype(F32)).astype(BF16)
    cumsum, prepare, chunk = _gla_chunk_fns(gq_scr, gk_scr, gv_scr, gog_scr, dec_scr, gnw_ref, oa_ref,
                                            st_scr, tri_scr, lev_scr, b_scr, w_scr)
    n_chunks = h_ref.shape[0] // GLA_CHUNK
    for c in range(n_chunks):
        cumsum(c * GLA_CHUNK, c)
        prepare(c)

    gq_scr[...] = proj(_P_GQ, _P_GK) * 0.125
    gk_scr[...] = proj(_P_GK, _P_GV)
    gv_scr[...] = proj(_P_GV, _P_GOG).astype(BF16)
    gate = proj(_P_GOG, _P_MQ)
    gog_scr[...] = gate * _sigmoid(gate)
    mq_ref[...] = (proj(_P_MQ, _P_MK) * MOBA_QSCALE).astype(BF16)
    mk = proj(_P_MK, _P_LR)
    mk_ref[...] = mk.astype(BF16)
    kmean_ref[0] = jnp.mean(mk.reshape(-1, MOBA_BLOCK, mk.shape[1]), axis=1)
    mvt = _dot_nt(wvt_ref[...], u_scr[...])
    ones = jnp.ones((MOBA_VROWS - MOBA_DH, MOBA_BLOCK), BF16)
    for h in range(MOBA_HEADS):
        for j in range(mvt.shape[1] // MOBA_BLOCK):
            mvt_ref[0, h, j, :MOBA_DH, :] = mvt[h * MOBA_DH:(h + 1) * MOBA_DH,
                                                j * MOBA_BLOCK:(j + 1) * MOBA_BLOCK].astype(BF16)
            mvt_ref[0, h, j, MOBA_DH:, :] = ones

    for c in range(n_chunks):
        chunk(c * GLA_CHUNK, c)


def _inproj(h, mod4, norm_w, wp, wvt, wlr, blr, gla_norm, bsz, seq):
    t, d = h.shape
    tm = INPROJ_TILE
    tiles_per_batch = seq // tm
    nblk = tm // MOBA_BLOCK
    nb = seq // MOBA_BLOCK
    qk_w, v_w = _P_GK - _P_GQ, _P_GOG - _P_GV
    row = lambda i: (i, 0)
    out_shape = [jax.ShapeDtypeStruct((t, v_w), BF16), jax.ShapeDtypeStruct((t, 512), BF16),
                 jax.ShapeDtypeStruct((t, 512), BF16),
                 jax.ShapeDtypeStruct((bsz, MOBA_HEADS, nb, MOBA_VROWS, MOBA_BLOCK), BF16),
                 jax.ShapeDtypeStruct((t // tm, nblk, 512), F32)]
    out_specs = [pl.BlockSpec((tm, v_w), row), pl.BlockSpec((tm, 512), row), pl.BlockSpec((tm, 512), row),
                 pl.BlockSpec((1, MOBA_HEADS, nblk, MOBA_VROWS, MOBA_BLOCK),
                              lambda i: (i // tiles_per_batch, 0, i % tiles_per_batch, 0, 0)),
                 pl.BlockSpec((1, nblk, 512), lambda i: (i, 0, 0))]
    return pl.pallas_call(
        functools.partial(_inproj_kernel, tiles_per_batch=tiles_per_batch),
        grid=(t // tm,),
        in_specs=[pl.BlockSpec((tm, d), row),
                  pl.BlockSpec((1, 1, 3, d), lambda i: (i // tiles_per_batch, 1, 0, 0)),
                  _const_spec((1, d)), _const_spec(wp.shape), _const_spec(wvt.shape),
                  _const_spec(wlr.shape), _const_spec(blr.shape), _const_spec(gla_norm.shape)],
        out_specs=out_specs,
        out_shape=out_shape,
        scratch_shapes=[pltpu.VMEM((tm, d), BF16),
                        pltpu.VMEM((tm, qk_w), F32), pltpu.VMEM((tm, qk_w), F32),
                        pltpu.VMEM((tm, v_w), BF16), pltpu.VMEM((tm, v_w), F32),
                        pltpu.VMEM((tm, 2 * qk_w), BF16),
                        pltpu.VMEM((GLA_HEADS // 2, 2 * v_w // GLA_HEADS, 2 * qk_w // GLA_HEADS), F32),
                        pltpu.VMEM((GLA_CHUNK, GLA_CHUNK), BF16),
                        pltpu.VMEM((GLA_CHUNK, 2 * GLA_CHUNK), jnp.int32),
                        pltpu.VMEM((tm // GLA_CHUNK, GLA_CHUNK, qk_w), F32),
                        pltpu.VMEM((tm // GLA_CHUNK, len(_GLA_LEVELS) + 2, GLA_CHUNK, qk_w), F32)],
        compiler_params=_params(("arbitrary",)),
        name="inproj",
    )(h, mod4, norm_w, wp, wvt, wlr, blr, gla_norm)


_GLA_LEVELS = [1 << p for p in range(int(math.log2(GLA_CHUNK)))]


def _gla_tables(tri_scr, lev_scr):
    L = GLA_CHUNK
    tri_scr[...] = (lax.broadcasted_iota(jnp.int32, (L, L), 1)
                    <= lax.broadcasted_iota(jnp.int32, (L, L), 0)).astype(BF16)
    ti = lax.broadcasted_iota(jnp.int32, (L, 2 * L), 0)
    tj = lax.broadcasted_iota(jnp.int32, (L, 2 * L), 1) & (L - 1)
    xor = ti ^ tj
    lev = jnp.where(tj > ti, -1, 0)
    for p in range(len(_GLA_LEVELS)):
        lev = jnp.where((xor >= (1 << p)) & (tj < ti), p + 1, lev)
    lev_scr[...] = lev


def _gla_chunk_fns(q_ref, k_ref, v_ref, og_ref, g_ref, nw_ref, o_ref, st_scr, tri_scr, lev_scr, b_scr, w_scr):
    L = GLA_CHUNK
    hq = q_ref.shape[1] // GLA_HEADS
    hv = v_ref.shape[1] // GLA_HEADS
    wq = q_ref.shape[1]
    pairs = GLA_HEADS // 2
    assert 2 * hq == 128 and hv == 128 and L == 128
    levels = _GLA_LEVELS

    sub = lax.broadcasted_iota(jnp.int32, (L, wq), 0) & 7
    even_head = lax.broadcasted_iota(jnp.int32, (L, 2 * hq), 1) < hq
    stbd = ((lax.broadcasted_iota(jnp.int32, (2 * hv, 2 * hq), 0) < hv)
            == (lax.broadcasted_iota(jnp.int32, (2 * hv, 2 * hq), 1) < hq))

    def rows_bcast(slot, first, period):
        return jnp.concatenate([jnp.broadcast_to(b_scr[slot, r:r + 1, :], (period, wq))
                                for r in range(first, L, period)], axis=0)

    def midpoint(slot, s):
        if s >= 4:
            return rows_bcast(slot, s - 1, 2 * s)
        if s == 2:
            return jnp.where(sub < 4, rows_bcast(slot, 1, 8), rows_bcast(slot, 5, 8))
        return jnp.where(sub < 2, rows_bcast(slot, 0, 8),
                         jnp.where(sub < 4, rows_bcast(slot, 2, 8),
                                   jnp.where(sub < 6, rows_bcast(slot, 4, 8), rows_bcast(slot, 6, 8))))

    def cumsum(r0, slot):
        bb = _dot(tri_scr[...], g_ref[pl.ds(r0, L), :])
        b_scr[slot] = bb[:, :wq] + bb[:, wq:]

    def pair_scores(qx, kx, p):
        pl_ = slice(p * 2 * hq, (p + 1) * 2 * hq)
        kp = kx[:, pl_]
        kk = jnp.concatenate([jnp.where(even_head, kp, 0.0), jnp.where(even_head, 0.0, kp)], axis=0)
        return _dot_nt(qx[:, pl_].astype(BF16), kk.astype(BF16))

    def prepare(slot):
        b = b_scr[slot]
        for li, s in enumerate(levels):
            d = lax.bitcast_convert_type(b - midpoint(slot, s), jnp.uint32) | jnp.uint32(0x80000000)
            w_scr[slot, li] = jnp.exp2(lax.bitcast_convert_type(d, F32))
        w_scr[slot, len(levels)] = jnp.exp2(b)
        w_scr[slot, len(levels) + 1] = jnp.exp2(b[L - 1:L, :] - b)

    def chunk(r0, slot):
        q = q_ref[pl.ds(r0, L), :]
        k = k_ref[pl.ds(r0, L), :]
        v = v_ref[pl.ds(r0, L), :]
        q_in = (q * w_scr[slot, len(levels)]).astype(BF16)
        k_out = (k * w_scr[slot, len(levels) + 1]).astype(BF16)
        dec = jnp.exp2(b_scr[slot, L - 1:L, :])

        att = [jnp.where(lev_scr[...] == 0, pair_scores(q, k, p), 0.0) for p in range(pairs)]
        for li in range(len(levels)):
            w = w_scr[slot, li]
            hit = lev_scr[...] == li + 1
            att = [jnp.where(hit, pair_scores(q * w, k * w, p), att[p]) for p in range(pairs)]

        for p in range(pairs):
            ql = slice(p * 2 * hq, (p + 1) * 2 * hq)
            vl = slice(p * 2 * hv, (p + 1) * 2 * hv)
            st = st_scr[p]
            vp = v[:, vl]
            ab = att[p].astype(BF16)
            o = _dot_nt(q_in[:, ql], st.astype(BF16)) + jnp.concatenate(
                [_dot(ab[:, :L], vp[:, :hv]), _dot(ab[:, L:], vp[:, hv:])], axis=1)
            st_scr[p] = st * dec[:, ql] + jnp.where(stbd, _dot_tn(vp, k_out[:, ql]), 0.0)
            for hh in range(2):
                sl = slice(vl.start + hh * hv, vl.start + (hh + 1) * hv)
                oh = _rmsnorm(o[:, hh * hv:(hh + 1) * hv], nw_ref[...])
                o_ref[pl.ds(r0, L), sl] = (oh * og_ref[pl.ds(r0, L), sl]).astype(BF16)

    return cumsum, prepare, chunk


def _rel_buckets(max_dist):
    n = np.arange(max_dist)
    nf = np.maximum(n, 1).astype(np.float64)
    large = MAX_EXACT + (np.log(nf / MAX_EXACT) / math.log(MAX_DISTANCE / MAX_EXACT)
                         * (NUM_BUCKETS - MAX_EXACT)).astype(np.int64)
    return np.where(n < MAX_EXACT, n, np.minimum(large, NUM_BUCKETS - 1))


_TBL_OWN, _TBL_PREV, _TBL_FAR = 0, 1, 2


def _bias_kernel(rb_ref, o_ref):
    h = pl.program_id(0)
    blk = MOBA_BLOCK
    buckets = _rel_buckets(2 * blk)
    assert (np.diff(buckets) >= 0).all()
    starts = {b: int(np.argmax(buckets == b)) for b in range(NUM_BUCKETS) if (buckets == b).any()}
    ki = lax.broadcasted_iota(jnp.int32, (blk, blk), 0)
    qi = lax.broadcasted_iota(jnp.int32, (blk, blk), 1)
    far = rb_ref[h, NUM_BUCKETS - 1]
    for tbl, base in ((_TBL_OWN, 0), (_TBL_PREV, blk)):
        dist = qi - ki + base
        val = jnp.full((blk, blk), rb_ref[h, 0], F32)
        for b in sorted(starts):
            if b > 0:
                val = jnp.where(dist >= starts[b], rb_ref[h, b], val)
        val = (val - far) * LOG2E
        if tbl == _TBL_OWN:
            val = jnp.where(dist >= 0, val, NEG)
        o_ref[0, tbl] = val
    o_ref[0, _TBL_FAR] = jnp.zeros((blk, blk), F32)


def _bias_tables(rel_bias):
    nh = rel_bias.shape[0]
    return pl.pallas_call(
        _bias_kernel,
        grid=(nh,),
        in_specs=[pl.BlockSpec(memory_space=pltpu.SMEM)],
        out_specs=pl.BlockSpec((1, 3, MOBA_BLOCK, MOBA_BLOCK), lambda h: (h, 0, 0, 0)),
        out_shape=jax.ShapeDtypeStruct((nh, 3, MOBA_BLOCK, MOBA_BLOCK), F32),
        compiler_params=_params(("arbitrary",)),
        name="moba_bias",
    )(rel_bias)


def _moba_kernel(q_ref, k_ref, vt_ref, km_ref, bias_ref, o_ref,
                 off_scr, m_scr, al_scr, acc_scr, p_scr, s_scr):
    n = pl.program_id(1)
    blk = MOBA_BLOCK
    nb = km_ref.shape[1]
    heads = range(q_ref.shape[0] * MOBA_HEADS)
    row = [h // MOBA_HEADS for h in heads]
    hsl = [slice((h % MOBA_HEADS) * MOBA_DH, (h % MOBA_HEADS + 1) * MOBA_DH) for h in heads]

    brow = lax.broadcasted_iota(jnp.int32, (nb, len(heads) * blk), 0).astype(F32)
    gate = jnp.concatenate([_dot_nt(km_ref[row[h], :, hsl[h]].astype(BF16), q_ref[row[h], :, hsl[h]])
                            for h in heads], axis=1)
    gate = jnp.where(brow < n.astype(F32), gate, NEG)
    off = jnp.full(gate.shape, NEG, F32)
    for _ in range(MOBA_TOPK):
        best = jnp.max(gate, axis=0, keepdims=True)
        first = jnp.min(jnp.where(gate == best, brow, float(nb)), axis=0, keepdims=True)
        pick = (brow == first) & (best > NEG)
        off = jnp.where(pick, 0.0, off)
        gate = jnp.where(pick, NEG, gate)
    for h in heads:
        off_scr[h] = off[:, h * blk:(h + 1) * blk]
        m_scr[h] = jnp.full((1, blk), NEG, F32)
        acc_scr[h] = jnp.zeros((MOBA_VROWS, blk), F32)
        al_scr[1, h] = jnp.ones((1, blk), F32)
        p_scr[1, h] = jnp.zeros((blk, blk), BF16)

    def qk(h, j, slot):
        r0 = pl.multiple_of(j * blk, blk)
        s_scr[slot, h] = _dot_nt(k_ref[row[h], pl.ds(r0, blk), hsl[h]], q_ref[row[h], :, hsl[h]])

    def pv(h, j, slot):
        acc_scr[h] = al_scr[slot, h] * acc_scr[h] + _dot(vt_ref[row[h], h % MOBA_HEADS, j], p_scr[slot, h])

    def step(slot, j, j_next, j_prev, table, live):
        for h in heads:
            pv(h, j_prev, 1 - slot)
            if j_next is not None:
                qk(h, j_next, 1 - slot)
        for h in heads:
            sh = s_scr[slot, h]
            if table is not None:
                sh = sh + bias_ref[h % MOBA_HEADS, table]
            top = jnp.max(sh, axis=0, keepdims=True)
            m_old = m_scr[h]
            if live is not None:
                off = jnp.where(live, off_scr[h, pl.ds(j, 1), :], NEG)
                m_new = jnp.maximum(m_old, top + off)
                m_sub = jnp.where(off < 0.0, -NEG, m_new)
            else:
                m_new = jnp.maximum(m_old, top)
                m_sub = m_new
            m_scr[h] = m_new
            al_scr[slot, h] = jnp.exp2(m_old - m_new)
            p_scr[slot, h] = jnp.exp2(sh - m_sub).astype(BF16)

    n_far = jnp.maximum(n - 1, 0)
    last_far = jnp.maximum(n_far - 1, 0)
    j_before = jnp.maximum(n - 1, 0)
    far_blk = lambda u: jnp.clip(u, 0, last_far)

    for h in heads:
        qk(h, jnp.where(n_far > 0, 0, j_before), 0)

    def far_steps(u, count):
        for c in range(count):
            nxt = far_blk(u + c + 1)
            if c == count - 1:
                nxt = jnp.where(u + count < n_far, nxt, j_before)
            step(c % 2, far_blk(u + c), nxt, far_blk(u + c - 1), None, u + c < n_far)

    done = 0
    for size in MOBA_BODIES:
        left = n_far - done
        trips = (left + 1) // 2 if size == 2 else left // size

        def body(i, carry, size=size, done=done):
            far_steps(done + size * i, size)
            return carry

        lax.fori_loop(0, trips, body, 0)
        done = done + trips * size
    step(0, j_before, n, last_far, _TBL_PREV, n >= 1)
    step(1, n, None, j_before, _TBL_OWN, None)
    for h in heads:
        pv(h, n, 1)
        acc = acc_scr[h]
        o_ref[row[h], :, hsl[h]] = (acc[:MOBA_DH] / acc[MOBA_DH:MOBA_DH + 1]).T.astype(BF16)


def _moba(mq, mk, mvt, kmean, bias, bsz, seq):
    t, w = mq.shape
    blk = MOBA_BLOCK
    nb = seq // blk
    g = MOBA_ROWS if bsz % MOBA_ROWS == 0 else 1
    units = g * MOBA_HEADS
    resident = lambda shape: pl.BlockSpec(shape, lambda b, n: (b,) + (0,) * (len(shape) - 1),
                                          pipeline_mode=pl.Buffered(1))
    out = pl.pallas_call(
        _moba_kernel,
        grid=(bsz // g, nb),
        in_specs=[pl.BlockSpec((g, blk, w), lambda b, n: (b, n, 0)),
                  resident((g, seq, w)),
                  resident((g, MOBA_HEADS, nb, MOBA_VROWS, blk)),
                  pl.BlockSpec((g, nb, w), lambda b, n: (b, 0, 0)),
                  _const_spec(bias.shape)],
        out_specs=pl.BlockSpec((g, blk, w), lambda b, n: (b, n, 0)),
        out_shape=jax.ShapeDtypeStruct((bsz, seq, w), BF16),
        scratch_shapes=[pltpu.VMEM((units, nb, blk), F32),
                        pltpu.VMEM((units, 1, blk), F32),
                        pltpu.VMEM((2, units, 1, blk), F32),
                        pltpu.VMEM((units, MOBA_VROWS, blk), F32),
                        pltpu.VMEM((2, units, blk, blk), BF16),
                        pltpu.VMEM((2, units, blk, blk), F32)],
        compiler_params=_params(("arbitrary", "arbitrary")),
        name="moba",
    )(mq.reshape(bsz, seq, w), mk.reshape(bsz, seq, w), mvt, kmean, bias)
    return out.reshape(t, w)


def _merge_kernel(h_ref, mod_ref, nw_ref, oa_ref, ob_ref, wga_ref, wgb_ref, wa_ref, wb_ref, wo_ref,
                  o_ref, u_scr):
    mod = mod_ref[0, 0]
    u_scr[...] = _norm_mod(h_ref[...], nw_ref[...], mod).astype(BF16)
    ya = _dot(oa_ref[...], wa_ref[...])
    yb = _dot(ob_ref[...], wb_ref[...])
    merged = _sigmoid(_dot(u_scr[...], wga_ref[...])) * ya + _sigmoid(_dot(u_scr[...], wgb_ref[...])) * yb
    o_ref[...] = h_ref[...] + mod[2:3, :] * _dot(merged.astype(BF16), wo_ref[...])


def _merge(h, mod4, norm_w, oa, ob, wga, wgb, wa, wb, wo, tiles_per_batch):
    t, d = h.shape
    tm = TOKEN_TILE
    row = lambda i: (i, 0)
    return pl.pallas_call(
        _merge_kernel,
        grid=(t // tm,),
        in_specs=[pl.BlockSpec((tm, d), row),
                  pl.BlockSpec((1, 1, 3, d), lambda i: (i // tiles_per_batch, 1, 0, 0)),
                  _const_spec((1, d)),
                  pl.BlockSpec((tm, oa.shape[1]), row), pl.BlockSpec((tm, ob.shape[1]), row),
                  _const_spec(wga.shape), _const_spec(wgb.shape), _const_spec(wa.shape),
                  _const_spec(wb.shape), _const_spec(wo.shape)],
        out_specs=pl.BlockSpec((tm, d), row),
        out_shape=jax.ShapeDtypeStruct((t, d), F32),
        scratch_shapes=[pltpu.VMEM((tm, d), BF16)],
        compiler_params=_params(("arbitrary",)),
        name="merge",
    )(h, mod4, norm_w, oa, ob, wga, wgb, wa, wb, wo)


def kernel(x, c, w_ada, b_ada, norm_ff1, w_ff1_gate, w_ff1_up, w_ff1_down, norm_mix, w_in, w_gla_lr,
           b_gla_lr, gla_norm, rel_bias, w_br_gla, w_br_moba, w_out, norm_ff2, w_ff2_gate, w_ff2_up,
           w_ff2_down, norm_final):
    bsz, seq, d = x.shape
    depth = w_ada.shape[0]
    t = bsz * seq
    tiles_per_batch = seq // TOKEN_TILE
    qk_w = GLA_HEADS * (d // 16)
    v_w = GLA_HEADS * (d // 8)
    m_w = MOBA_HEADS * MOBA_DH
    offs = np.cumsum([0, qk_w, qk_w, v_w, GLA_LOWRANK, v_w, m_w, m_w, m_w, d, d])
    bias = _bias_tables(rel_bias)
    h = x.reshape(t, d)
    c_act_in = c
    for l in range(depth):
        mod4 = _adaln(c_act_in, w_ada, b_ada, l).reshape(bsz, 3, 3, d)
        cast = lambda w: w.astype(BF16)
        h = _ffn(h, mod4, 0, norm_ff1[l].reshape(1, d), w_ff1_gate[l], w_ff1_up[l], w_ff1_down[l], seq)
        wi = w_in[l]
        seg = lambda i: wi[:, offs[i]:offs[i + 1]]
        lr_pad = jnp.pad(seg(3), ((0, 0), (0, 128 - GLA_LOWRANK)))
        wp = cast(jnp.concatenate([seg(0), seg(1), seg(2), seg(4), seg(5), seg(6), lr_pad], axis=1))
        wvt = cast(seg(7).T)
        wlr = cast(jnp.pad(w_gla_lr[l], ((0, 128 - GLA_LOWRANK), (0, 0))))
        o_a, mq, mk, mvt, kmean = _inproj(
            h, mod4, norm_mix[l].reshape(1, d), wp, wvt, wlr, b_gla_lr[l].reshape(1, qk_w),
            gla_norm[l].reshape(1, -1), bsz, seq)
        o_b = _moba(mq, mk, mvt, kmean.reshape(bsz, seq // MOBA_BLOCK, m_w), bias, bsz, seq)
        h = _merge(h, mod4, norm_mix[l].reshape(1, d), o_a, o_b, cast(seg(8)), cast(seg(9)),
                   cast(w_br_gla[l]), cast(w_br_moba[l]), cast(w_out[l]), tiles_per_batch)
        last = l == depth - 1
        h = _ffn(h, mod4, 2, norm_ff2[l].reshape(1, d), w_ff2_gate[l], w_ff2_up[l], w_ff2_down[l], seq,
                 norm_final.reshape(1, d) if last else None)
    return h.reshape(bsz, seq, d)
```

```python
import functools
import math

import numpy as np
import jax
import jax.numpy as jnp
from jax import lax
from jax.experimental import pallas as pl
from jax.experimental.pallas import tpu as pltpu

F32 = jnp.float32
BF16 = jnp.bfloat16

EPS = 1e-6
FFN_RES = 0.5
GLA_HEADS = 4
GLA_TAU = 16.0
GLA_LOWRANK = 16
GLA_CHUNK = 128
MOBA_HEADS = 4
MOBA_DH = 128
MOBA_BLOCK = 256
MOBA_TOPK = 3
MOBA_VROWS = MOBA_DH + 16
MOBA_ROWS = 2
MOBA_BODIES = (8, 4, 2)
NUM_BUCKETS = 32
MAX_DISTANCE = 128
MAX_EXACT = NUM_BUCKETS // 2
NEG = -1e30
LOG2E = 1.4426950408889634
MOBA_QSCALE = LOG2E * MOBA_DH ** -0.5
TOKEN_TILE = 1024
FFN_TILE = 1024
INPROJ_TILE = 1024
FF_CHUNK = 256
V7X_VMEM_BYTES = 64 * 1024 * 1024
VMEM_LIMIT = V7X_VMEM_BYTES - 3 * 1024 * 1024


def _dot(a, b):
    return jnp.dot(a, b, preferred_element_type=F32)


def _dot_nt(a, b):
    return lax.dot_general(a, b, (((1,), (1,)), ((), ())), preferred_element_type=F32)


def _dot_tn(a, b):
    return lax.dot_general(a, b, (((0,), (0,)), ((), ())), preferred_element_type=F32)


def _sigmoid(x):
    return 1.0 / (1.0 + jnp.exp(-x))


def _rmsnorm(x, w):
    return x * lax.rsqrt(jnp.mean(x * x, axis=-1, keepdims=True) + EPS) * w


def _norm_mod(x, w, mod):
    return _rmsnorm(x, w) * (1.0 + mod[1:2, :]) + mod[0:1, :]


def _params(sem):
    return pltpu.CompilerParams(dimension_semantics=sem, vmem_limit_bytes=VMEM_LIMIT)


def _const_spec(shape):
    nd = len(shape)
    return pl.BlockSpec(shape, lambda *_: (0,) * nd, pipeline_mode=pl.Buffered(1))


def _adaln_kernel(c_ref, w_ref, b_ref, o_ref):
    c = c_ref[...]
    ca = (c * _sigmoid(c)).astype(BF16)
    o_ref[...] = _dot(ca, w_ref[...].astype(BF16)) + b_ref[...]


def _adaln(c, w_ada, b_ada, layer):
    bsz, d = c.shape
    depth, _, n = w_ada.shape
    return pl.pallas_call(
        _adaln_kernel,
        grid=(n // d,),
        in_specs=[pl.BlockSpec((bsz, d), lambda j: (0, 0)),
                  pl.BlockSpec((None, d, d), lambda j: (layer, 0, j)),
                  pl.BlockSpec((None, 1, d), lambda j: (layer, 0, j))],
        out_specs=pl.BlockSpec((bsz, d), lambda j: (0, j)),
        out_shape=jax.ShapeDtypeStruct((bsz, n), F32),
        compiler_params=_params(("arbitrary",)),
        name="adaln",
    )(c, w_ada, b_ada.reshape(depth, 1, n))


def _ffn_kernel(x_ref, mod_ref, nw_ref, wg_ref, wu_ref, wd_ref, *rest, final_norm):
    if final_norm:
        nf_ref, o_ref, u_scr, a_scr = rest
    else:
        o_ref, u_scr, a_scr = rest
    mod = mod_ref[0, 0]
    u_scr[...] = _norm_mod(x_ref[...], nw_ref[...], mod).astype(BF16)
    d_ff = wg_ref.shape[1]
    for c in range(d_ff // FF_CHUNK):
        sl = slice(c * FF_CHUNK, (c + 1) * FF_CHUNK)
        g = _dot(u_scr[...], wg_ref[:, sl].astype(BF16))
        up = _dot(u_scr[...], wu_ref[:, sl].astype(BF16))
        a_scr[:, sl] = (g * _sigmoid(g) * up).astype(BF16)
    y = _dot(a_scr[...], wd_ref[...].astype(BF16))
    out = x_ref[...] + (FFN_RES * mod[2:3, :]) * y
    if final_norm:
        out = _rmsnorm(out, nf_ref[...])
    o_ref[...] = out


def _ffn(h, mod4, layer, norm_w, wg, wu, wd, seq, norm_final=None):
    t, d = h.shape
    d_ff = wg.shape[1]
    tm = FFN_TILE
    tiles_per_batch = seq // tm
    in_specs = [pl.BlockSpec((tm, d), lambda i: (i, 0)),
                pl.BlockSpec((1, 1, 3, d), lambda i: (i // tiles_per_batch, layer, 0, 0)),
                _const_spec((1, d)), _const_spec((d, d_ff)), _const_spec((d, d_ff)), _const_spec((d_ff, d))]
    args = [h, mod4, norm_w, wg, wu, wd]
    if norm_final is not None:
        in_specs.append(_const_spec((1, d)))
        args.append(norm_final)
    return pl.pallas_call(
        functools.partial(_ffn_kernel, final_norm=norm_final is not None),
        grid=(t // tm,),
        in_specs=in_specs,
        out_specs=pl.BlockSpec((tm, d), lambda i: (i, 0)),
        out_shape=jax.ShapeDtypeStruct((t, d), F32),
        scratch_shapes=[pltpu.VMEM((tm, d), BF16), pltpu.VMEM((tm, d_ff), BF16)],
        compiler_params=_params(("arbitrary",)),
        name="ffn_final" if norm_final is not None else "ffn",
    )(*args)


_P_GQ, _P_GK, _P_GV, _P_GOG, _P_MQ, _P_MK, _P_LR, _P_END = 0, 256, 512, 1024, 1536, 2048, 2560, 2688


def _inproj_kernel(h_ref, mod_ref, nw_ref, wp_ref, wvt_ref, wlr_ref, blr_ref, gnw_ref,
                   oa_ref, mq_ref, mk_ref, mvt_ref, kmean_ref,
                   u_scr, gq_scr, gk_scr, gv_scr, gog_scr, dec_scr, st_scr, tri_scr, lev_scr, b_scr, w_scr,
                   *, tiles_per_batch):
    i = pl.program_id(0)

    @pl.when(i == 0)
    def _():
        _gla_tables(tri_scr, lev_scr)

    @pl.when(lax.rem(i, jnp.int32(tiles_per_batch)) == 0)
    def _():
        st_scr[...] = jnp.zeros_like(st_scr)

    u_scr[...] = _norm_mod(h_ref[...], nw_ref[...], mod_ref[0, 0]).astype(BF16)

    def proj(lo, hi):
        return _dot(u_scr[...], wp_ref[:, lo:hi])

    glr = proj(_P_LR, _P_END).astype(BF16)
    z = _dot(glr, wlr_ref[...]) + blr_ref[...]
    g = (jnp.minimum(z, 0.0) - jnp.log1p(jnp.exp(-jnp.abs(z)))) * (LOG2E / GLA_TAU)
    g_hi = g.astype(BF16)
    dec_scr[:, :g.shape[1]] = g_hi
    dec_scr[:, g.shape[1]:] = (g - g_hi.astype(F32)).astype(BF16)
    cumsum, prepare, chunk = _gla_chunk_fns(gq_scr, gk_scr, gv_scr, gog_scr, dec_scr, gnw_ref, oa_ref,
                                            st_scr, tri_scr, lev_scr, b_scr, w_scr)
    n_chunks = h_ref.shape[0] // GLA_CHUNK
    for c in range(n_chunks):
        cumsum(c * GLA_CHUNK, c)
        prepare(c)

    gq_scr[...] = proj(_P_GQ, _P_GK) * 0.125
    gk_scr[...] = proj(_P_GK, _P_GV)
    gv_scr[...] = proj(_P_GV, _P_GOG).astype(BF16)
    gate = proj(_P_GOG, _P_MQ)
    gog_scr[...] = gate * _sigmoid(gate)
    mq_ref[...] = (proj(_P_MQ, _P_MK) * MOBA_QSCALE).astype(BF16)
    mk = proj(_P_MK, _P_LR)
    mk_ref[...] = mk.astype(BF16)
    kmean_ref[0] = jnp.mean(mk.reshape(-1, MOBA_BLOCK, mk.shape[1]), axis=1)
    mvt = _dot_nt(wvt_ref[...], u_scr[...])
    ones = jnp.ones((MOBA_VROWS - MOBA_DH, MOBA_BLOCK), BF16)
    for h in range(MOBA_HEADS):
        for j in range(mvt.shape[1] // MOBA_BLOCK):
            mvt_ref[0, h, j, :MOBA_DH, :] = mvt[h * MOBA_DH:(h + 1) * MOBA_DH,
                                                j * MOBA_BLOCK:(j + 1) * MOBA_BLOCK].astype(BF16)
            mvt_ref[0, h, j, MOBA_DH:, :] = ones

    for c in range(n_chunks):
        chunk(c * GLA_CHUNK, c)


def _inproj(h, mod4, norm_w, wp, wvt, wlr, blr, gla_norm, bsz, seq):
    t, d = h.shape
    tm = INPROJ_TILE
    tiles_per_batch = seq // tm
    nblk = tm // MOBA_BLOCK
    nb = seq // MOBA_BLOCK
    qk_w, v_w = _P_GK - _P_GQ, _P_GOG - _P_GV
    row = lambda i: (i, 0)
    out_shape = [jax.ShapeDtypeStruct((t, v_w), BF16), jax.ShapeDtypeStruct((t, 512), BF16),
                 jax.ShapeDtypeStruct((t, 512), BF16),
                 jax.ShapeDtypeStruct((bsz, MOBA_HEADS, nb, MOBA_VROWS, MOBA_BLOCK), BF16),
                 jax.ShapeDtypeStruct((t // tm, nblk, 512), F32)]
    out_specs = [pl.BlockSpec((tm, v_w), row), pl.BlockSpec((tm, 512), row), pl.BlockSpec((tm, 512), row),
                 pl.BlockSpec((1, MOBA_HEADS, nblk, MOBA_VROWS, MOBA_BLOCK),
                              lambda i: (i // tiles_per_batch, 0, i % tiles_per_batch, 0, 0)),
                 pl.BlockSpec((1, nblk, 512), lambda i: (i, 0, 0))]
    return pl.pallas_call(
        functools.partial(_inproj_kernel, tiles_per_batch=tiles_per_batch),
        grid=(t // tm,),
        in_specs=[pl.BlockSpec((tm, d), row),
                  pl.BlockSpec((1, 1, 3, d), lambda i: (i // tiles_per_batch, 1, 0, 0)),
                  _const_spec((1, d)), _const_spec(wp.shape), _const_spec(wvt.shape),
                  _const_spec(wlr.shape), _const_spec(blr.shape), _const_spec(gla_norm.shape)],
        out_specs=out_specs,
        out_shape=out_shape,
        scratch_shapes=[pltpu.VMEM((tm, d), BF16),
                        pltpu.VMEM((tm, qk_w), F32), pltpu.VMEM((tm, qk_w), F32),
                        pltpu.VMEM((tm, v_w), BF16), pltpu.VMEM((tm, v_w), F32),
                        pltpu.VMEM((tm, 2 * qk_w), BF16),
                        pltpu.VMEM((GLA_HEADS // 2, 2 * v_w // GLA_HEADS, 2 * qk_w // GLA_HEADS), F32),
                        pltpu.VMEM((GLA_CHUNK, GLA_CHUNK), BF16),
                        pltpu.VMEM((GLA_CHUNK, 2 * GLA_CHUNK), jnp.int32),
                        pltpu.VMEM((tm // GLA_CHUNK, GLA_CHUNK, qk_w), F32),
                        pltpu.VMEM((tm // GLA_CHUNK, len(_GLA_LEVELS) + 2, GLA_CHUNK, qk_w), F32)],
        compiler_params=_params(("arbitrary",)),
        name="inproj",
    )(h, mod4, norm_w, wp, wvt, wlr, blr, gla_norm)


_GLA_LEVELS = [1 << p for p in range(int(math.log2(GLA_CHUNK)))]


def _gla_tables(tri_scr, lev_scr):
    L = GLA_CHUNK
    tri_scr[...] = (lax.broadcasted_iota(jnp.int32, (L, L), 1)
                    <= lax.broadcasted_iota(jnp.int32, (L, L), 0)).astype(BF16)
    ti = lax.broadcasted_iota(jnp.int32, (L, 2 * L), 0)
    tj = lax.broadcasted_iota(jnp.int32, (L, 2 * L), 1) & (L - 1)
    xor = ti ^ tj
    lev = jnp.where(tj > ti, -1, 0)
    for p in range(len(_GLA_LEVELS)):
        lev = jnp.where((xor >= (1 << p)) & (tj < ti), p + 1, lev)
    lev_scr[...] = lev


def _gla_chunk_fns(q_ref, k_ref, v_ref, og_ref, g_ref, nw_ref, o_ref, st_scr, tri_scr, lev_scr, b_scr, w_scr):
    L = GLA_CHUNK
    hq = q_ref.shape[1] // GLA_HEADS
    hv = v_ref.shape[1] // GLA_HEADS
    wq = q_ref.shape[1]
    pairs = GLA_HEADS // 2
    assert 2 * hq == 128 and hv == 128 and L == 128
    levels = _GLA_LEVELS

    sub = lax.broadcasted_iota(jnp.int32, (L, wq), 0) & 7
    even_head = lax.broadcasted_iota(jnp.int32, (L, 2 * hq), 1) < hq
    stbd = ((lax.broadcasted_iota(jnp.int32, (2 * hv, 2 * hq), 0) < hv)
            == (lax.broadcasted_iota(jnp.int32, (2 * hv, 2 * hq), 1) < hq))

    def rows_bcast(slot, first, period):
        return jnp.concatenate([jnp.broadcast_to(b_scr[slot, r:r + 1, :], (period, wq))
                                for r in range(first, L, period)], axis=0)

    def midpoint(slot, s):
        if s >= 4:
            return rows_bcast(slot, s - 1, 2 * s)
        if s == 2:
            return jnp.where(sub < 4, rows_bcast(slot, 1, 8), rows_bcast(slot, 5, 8))
        return jnp.where(sub < 2, rows_bcast(slot, 0, 8),
                         jnp.where(sub < 4, rows_bcast(slot, 2, 8),
                                   jnp.where(sub < 6, rows_bcast(slot, 4, 8), rows_bcast(slot, 6, 8))))

    def cumsum(r0, slot):
        bb = _dot(tri_scr[...], g_ref[pl.ds(r0, L), :])
        b_scr[slot] = bb[:, :wq] + bb[:, wq:]

    def pair_scores(qx, kx, p):
        pl_ = slice(p * 2 * hq, (p + 1) * 2 * hq)
        kp = kx[:, pl_]
        kk = jnp.concatenate([jnp.where(even_head, kp, 0.0), jnp.where(even_head, 0.0, kp)], axis=0)
        return _dot_nt(qx[:, pl_].astype(BF16), kk.astype(BF16))

    def prepare(slot):
        b = b_scr[slot]
        for li, s in enumerate(levels):
            w_scr[slot, li] = jnp.exp2(-jnp.abs(b - midpoint(slot, s)))
        w_scr[slot, len(levels)] = jnp.exp2(b)
        w_scr[slot, len(levels) + 1] = jnp.exp2(b[L - 1:L, :] - b)

    def chunk(r0, slot):
        q = q_ref[pl.ds(r0, L), :]
        k = k_ref[pl.ds(r0, L), :]
        v = v_ref[pl.ds(r0, L), :]
        q_in = (q * w_scr[slot, len(levels)]).astype(BF16)
        k_out = (k * w_scr[slot, len(levels) + 1]).astype(BF16)
        dec = jnp.exp2(b_scr[slot, L - 1:L, :])

        att = [jnp.where(lev_scr[...] == 0, pair_scores(q, k, p), 0.0) for p in range(pairs)]
        for li in range(len(levels)):
            w = w_scr[slot, li]
            hit = lev_scr[...] == li + 1
            att = [jnp.where(hit, pair_scores(q * w, k * w, p), att[p]) for p in range(pairs)]

        for p in range(pairs):
            ql = slice(p * 2 * hq, (p + 1) * 2 * hq)
            vl = slice(p * 2 * hv, (p + 1) * 2 * hv)
            st = st_scr[p]
            vp = v[:, vl]
            ab = att[p].astype(BF16)
            o = _dot_nt(q_in[:, ql], st.astype(BF16)) + jnp.concatenate(
                [_dot(ab[:, :L], vp[:, :hv]), _dot(ab[:, L:], vp[:, hv:])], axis=1)
            st_scr[p] = st * dec[:, ql] + jnp.where(stbd, _dot_tn(vp, k_out[:, ql]), 0.0)
            for hh in range(2):
                sl = slice(vl.start + hh * hv, vl.start + (hh + 1) * hv)
                oh = _rmsnorm(o[:, hh * hv:(hh + 1) * hv], nw_ref[...])
                o_ref[pl.ds(r0, L), sl] = (oh * og_ref[pl.ds(r0, L), sl]).astype(BF16)

    return cumsum, prepare, chunk


def _rel_buckets(max_dist):
    n = np.arange(max_dist)
    nf = np.maximum(n, 1).astype(np.float64)
    large = MAX_EXACT + (np.log(nf / MAX_EXACT) / math.log(MAX_DISTANCE / MAX_EXACT)
                         * (NUM_BUCKETS - MAX_EXACT)).astype(np.int64)
    return np.where(n < MAX_EXACT, n, np.minimum(large, NUM_BUCKETS - 1))


_TBL_OWN, _TBL_PREV, _TBL_FAR = 0, 1, 2


def _bias_kernel(rb_ref, o_ref):
    h = pl.program_id(0)
    blk = MOBA_BLOCK
    buckets = _rel_buckets(2 * blk)
    assert (np.diff(buckets) >= 0).all()
    starts = {b: int(np.argmax(buckets == b)) for b in range(NUM_BUCKETS) if (buckets == b).any()}
    ki = lax.broadcasted_iota(jnp.int32, (blk, blk), 0)
    qi = lax.broadcasted_iota(jnp.int32, (blk, blk), 1)
    far = rb_ref[h, NUM_BUCKETS - 1]
    for tbl, base in ((_TBL_OWN, 0), (_TBL_PREV, blk)):
        dist = qi - ki + base
        val = jnp.full((blk, blk), rb_ref[h, 0], F32)
        for b in sorted(starts):
            if b > 0:
                val = jnp.where(dist >= starts[b], rb_ref[h, b], val)
        val = (val - far) * LOG2E
        if tbl == _TBL_OWN:
            val = jnp.where(dist >= 0, val, NEG)
        o_ref[0, tbl] = val
    o_ref[0, _TBL_FAR] = jnp.zeros((blk, blk), F32)


def _bias_tables(rel_bias):
    nh = rel_bias.shape[0]
    return pl.pallas_call(
        _bias_kernel,
        grid=(nh,),
        in_specs=[pl.BlockSpec(memory_space=pltpu.SMEM)],
        out_specs=pl.BlockSpec((1, 3, MOBA_BLOCK, MOBA_BLOCK), lambda h: (h, 0, 0, 0)),
        out_shape=jax.ShapeDtypeStruct((nh, 3, MOBA_BLOCK, MOBA_BLOCK), F32),
        compiler_params=_params(("arbitrary",)),
        name="moba_bias",
    )(rel_bias)


def _moba_kernel(q_ref, k_ref, vt_ref, km_ref, bias_ref, o_ref,
                 off_scr, m_scr, al_scr, acc_scr, p_scr, s_scr):
    n = pl.program_id(1)
    blk = MOBA_BLOCK
    nb = km_ref.shape[1]
    heads = range(q_ref.shape[0] * MOBA_HEADS)
    row = [h // MOBA_HEADS for h in heads]
    hsl = [slice((h % MOBA_HEADS) * MOBA_DH, (h % MOBA_HEADS + 1) * MOBA_DH) for h in heads]

    brow = lax.broadcasted_iota(jnp.int32, (nb, len(heads) * blk), 0).astype(F32)
    gate = jnp.concatenate([_dot_nt(km_ref[row[h], :, hsl[h]].astype(BF16), q_ref[row[h], :, hsl[h]])
                            for h in heads], axis=1)
    gate = jnp.where(brow < n.astype(F32), gate, NEG)
    off = jnp.full(gate.shape, NEG, F32)
    for _ in range(MOBA_TOPK):
        best = jnp.max(gate, axis=0, keepdims=True)
        first = jnp.min(jnp.where(gate == best, brow, float(nb)), axis=0, keepdims=True)
        pick = (brow == first) & (best > NEG)
        off = jnp.where(pick, 0.0, off)
        gate = jnp.where(pick, NEG, gate)
    for h in heads:
        off_scr[h] = off[:, h * blk:(h + 1) * blk]
        m_scr[h] = jnp.full((1, blk), NEG, F32)
        acc_scr[h] = jnp.zeros((MOBA_VROWS, blk), F32)
        al_scr[1, h] = jnp.ones((1, blk), F32)
        p_scr[1, h] = jnp.zeros((blk, blk), BF16)

    def qk(h, j, slot):
        r0 = pl.multiple_of(j * blk, blk)
        s_scr[slot, h] = _dot_nt(k_ref[row[h], pl.ds(r0, blk), hsl[h]], q_ref[row[h], :, hsl[h]])

    def pv(h, j, slot):
        acc_scr[h] = al_scr[slot, h] * acc_scr[h] + _dot(vt_ref[row[h], h % MOBA_HEADS, j], p_scr[slot, h])

    def step(slot, j, j_next, j_prev, table, live):
        for h in heads:
            pv(h, j_prev, 1 - slot)
            if j_next is not None:
                qk(h, j_next, 1 - slot)
        for h in heads:
            sh = s_scr[slot, h]
            if table is not None:
                sh = sh + bias_ref[h % MOBA_HEADS, table]
            top = jnp.max(sh, axis=0, keepdims=True)
            m_old = m_scr[h]
            if live is not None:
                off = jnp.where(live, off_scr[h, pl.ds(j, 1), :], NEG)
                m_new = jnp.maximum(m_old, top + off)
                m_sub = jnp.where(off < 0.0, -NEG, m_new)
            else:
                m_new = jnp.maximum(m_old, top)
                m_sub = m_new
            m_scr[h] = m_new
            al_scr[slot, h] = jnp.exp2(m_old - m_new)
            p_scr[slot, h] = jnp.exp2(sh - m_sub).astype(BF16)

    n_far = jnp.maximum(n - 1, 0)
    last_far = jnp.maximum(n_far - 1, 0)
    j_before = jnp.maximum(n - 1, 0)
    far_blk = lambda u: jnp.clip(u, 0, last_far)

    for h in heads:
        qk(h, jnp.where(n_far > 0, 0, j_before), 0)

    def far_steps(u, count):
        for c in range(count):
            nxt = far_blk(u + c + 1)
            if c == count - 1:
                nxt = jnp.where(u + count < n_far, nxt, j_before)
            step(c % 2, far_blk(u + c), nxt, far_blk(u + c - 1), None, u + c < n_far)

    done = 0
    for size in MOBA_BODIES:
        left = n_far - done
        trips = (left + 1) // 2 if size == 2 else left // size

        def body(i, carry, size=size, done=done):
            far_steps(done + size * i, size)
            return carry

        lax.fori_loop(0, trips, body, 0)
        done = done + trips * size
    step(0, j_before, n, last_far, _TBL_PREV, n >= 1)
    step(1, n, None, j_before, _TBL_OWN, None)
    for h in heads:
        pv(h, n, 1)
        acc = acc_scr[h]
        o_ref[row[h], :, hsl[h]] = (acc[:MOBA_DH] / acc[MOBA_DH:MOBA_DH + 1]).T.astype(BF16)


def _moba(mq, mk, mvt, kmean, bias, bsz, seq):
    t, w = mq.shape
    blk = MOBA_BLOCK
    nb = seq // blk
    g = MOBA_ROWS if bsz % MOBA_ROWS == 0 else 1
    units = g * MOBA_HEADS
    resident = lambda shape: pl.BlockSpec(shape, lambda b, n: (b,) + (0,) * (len(shape) - 1),
                                          pipeline_mode=pl.Buffered(1))
    out = pl.pallas_call(
        _moba_kernel,
        grid=(bsz // g, nb),
        in_specs=[pl.BlockSpec((g, blk, w), lambda b, n: (b, n, 0)),
                  resident((g, seq, w)),
                  resident((g, MOBA_HEADS, nb, MOBA_VROWS, blk)),
                  pl.BlockSpec((g, nb, w), lambda b, n: (b, 0, 0)),
                  _const_spec(bias.shape)],
        out_specs=pl.BlockSpec((g, blk, w), lambda b, n: (b, n, 0)),
        out_shape=jax.ShapeDtypeStruct((bsz, seq, w), BF16),
        scratch_shapes=[pltpu.VMEM((units, nb, blk), F32),
                        pltpu.VMEM((units, 1, blk), F32),
                        pltpu.VMEM((2, units, 1, blk), F32),
                        pltpu.VMEM((units, MOBA_VROWS, blk), F32),
                        pltpu.VMEM((2, units, blk, blk), BF16),
                        pltpu.VMEM((2, units, blk, blk), F32)],
        compiler_params=_params(("arbitrary", "arbitrary")),
        name="moba",
    )(mq.reshape(bsz, seq, w), mk.reshape(bsz, seq, w), mvt, kmean, bias)
    return out.reshape(t, w)


def _merge_kernel(h_ref, mod_ref, nw_ref, oa_ref, ob_ref, wga_ref, wgb_ref, wa_ref, wb_ref, wo_ref,
                  o_ref, u_scr):
    mod = mod_ref[0, 0]
    u_scr[...] = _norm_mod(h_ref[...], nw_ref[...], mod).astype(BF16)
    ya = _dot(oa_ref[...], wa_ref[...])
    yb = _dot(ob_ref[...], wb_ref[...])
    merged = _sigmoid(_dot(u_scr[...], wga_ref[...])) * ya + _sigmoid(_dot(u_scr[...], wgb_ref[...])) * yb
    o_ref[...] = h_ref[...] + mod[2:3, :] * _dot(merged.astype(BF16), wo_ref[...])


def _merge(h, mod4, norm_w, oa, ob, wga, wgb, wa, wb, wo, tiles_per_batch):
    t, d = h.shape
    tm = TOKEN_TILE
    row = lambda i: (i, 0)
    return pl.pallas_call(
        _merge_kernel,
        grid=(t // tm,),
        in_specs=[pl.BlockSpec((tm, d), row),
                  pl.BlockSpec((1, 1, 3, d), lambda i: (i // tiles_per_batch, 1, 0, 0)),
                  _const_spec((1, d)),
                  pl.BlockSpec((tm, oa.shape[1]), row), pl.BlockSpec((tm, ob.shape[1]), row),
                  _const_spec(wga.shape), _const_spec(wgb.shape), _const_spec(wa.shape),
                  _const_spec(wb.shape), _const_spec(wo.shape)],
        out_specs=pl.BlockSpec((tm, d), row),
        out_shape=jax.ShapeDtypeStruct((t, d), F32),
        scratch_shapes=[pltpu.VMEM((tm, d), BF16)],
        compiler_params=_params(("arbitrary",)),
        name="merge",
    )(h, mod4, norm_w, oa, ob, wga, wgb, wa, wb, wo)


def kernel(x, c, w_ada, b_ada, norm_ff1, w_ff1_gate, w_ff1_up, w_ff1_down, norm_mix, w_in, w_gla_lr,
           b_gla_lr, gla_norm, rel_bias, w_br_gla, w_br_moba, w_out, norm_ff2, w_ff2_gate, w_ff2_up,
           w_ff2_down, norm_final):
    bsz, seq, d = x.shape
    depth = w_ada.shape[0]
    t = bsz * seq
    tiles_per_batch = seq // TOKEN_TILE
    qk_w = GLA_HEADS * (d // 16)
    v_w = GLA_HEADS * (d // 8)
    m_w = MOBA_HEADS * MOBA_DH
    offs = np.cumsum([0, qk_w, qk_w, v_w, GLA_LOWRANK, v_w, m_w, m_w, m_w, d, d])
    bias = _bias_tables(rel_bias)
    h = x.reshape(t, d)
    c_act_in = c
    for l in range(depth):
        mod4 = _adaln(c_act_in, w_ada, b_ada, l).reshape(bsz, 3, 3, d)
        cast = lambda w: w.astype(BF16)
        h = _ffn(h, mod4, 0, norm_ff1[l].reshape(1, d), w_ff1_gate[l], w_ff1_up[l], w_ff1_down[l], seq)
        wi = w_in[l]
        seg = lambda i: wi[:, offs[i]:offs[i + 1]]
        lr_pad = jnp.pad(seg(3), ((0, 0), (0, 128 - GLA_LOWRANK)))
        wp = cast(jnp.concatenate([seg(0), seg(1), seg(2), seg(4), seg(5), seg(6), lr_pad], axis=1))
        wvt = cast(seg(7).T)
        wlr = cast(jnp.pad(w_gla_lr[l], ((0, 128 - GLA_LOWRANK), (0, 0))))
        o_a, mq, mk, mvt, kmean = _inproj(
            h, mod4, norm_mix[l].reshape(1, d), wp, wvt, wlr, b_gla_lr[l].reshape(1, qk_w),
            gla_norm[l].reshape(1, -1), bsz, seq)
        o_b = _moba(mq, mk, mvt, kmean.reshape(bsz, seq // MOBA_BLOCK, m_w), bias, bsz, seq)
        h = _merge(h, mod4, norm_mix[l].reshape(1, d), o_a, o_b, cast(seg(8)), cast(seg(9)),
                   cast(w_br_gla[l]), cast(w_br_moba[l]), cast(w_out[l]), tiles_per_batch)
        last = l == depth - 1
        h = _ffn(h, mod4, 2, norm_ff2[l].reshape(1, d), w_ff2_gate[l], w_ff2_up[l], w_ff2_down[l], seq,
                 norm_final.reshape(1, d) if last else None)
    return h.reshape(bsz, seq, d)
```

```python
import functools
import math

import numpy as np
import jax
import jax.numpy as jnp
from jax import lax
from jax.experimental import pallas as pl
from jax.experimental.pallas import tpu as pltpu

F32 = jnp.float32
BF16 = jnp.bfloat16

EPS = 1e-6
FFN_RES = 0.5
GLA_HEADS = 4
GLA_TAU = 16.0
GLA_LOWRANK = 16
GLA_CHUNK = 128
MOBA_HEADS = 4
MOBA_DH = 128
MOBA_BLOCK = 256
MOBA_TOPK = 3
MOBA_VROWS = MOBA_DH + 16
MOBA_ROWS = 2
MOBA_BODIES = (8, 4, 2)
NUM_BUCKETS = 32
MAX_DISTANCE = 128
MAX_EXACT = NUM_BUCKETS // 2
NEG = -1e30
LOG2E = 1.4426950408889634
MOBA_QSCALE = LOG2E * MOBA_DH ** -0.5
TOKEN_TILE = 1024
FFN_TILE = 1024
INPROJ_TILE = 1024
FF_CHUNK = 256
V7X_VMEM_BYTES = 64 * 1024 * 1024
VMEM_LIMIT = V7X_VMEM_BYTES - 3 * 1024 * 1024


def _dot(a, b):
    return jnp.dot(a, b, preferred_element_type=F32)


def _dot_nt(a, b):
    return lax.dot_general(a, b, (((1,), (1,)), ((), ())), preferred_element_type=F32)


def _dot_tn(a, b):
    return lax.dot_general(a, b, (((0,), (0,)), ((), ())), preferred_element_type=F32)


def _sigmoid(x):
    return 1.0 / (1.0 + jnp.exp(-x))


def _rmsnorm(x, w):
    return x * lax.rsqrt(jnp.mean(x * x, axis=-1, keepdims=True) + EPS) * w


def _norm_mod(x, w, mod):
    return _rmsnorm(x, w) * (1.0 + mod[1:2, :]) + mod[0:1, :]


def _params(sem):
    return pltpu.CompilerParams(dimension_semantics=sem, vmem_limit_bytes=VMEM_LIMIT)


def _const_spec(shape):
    nd = len(shape)
    return pl.BlockSpec(shape, lambda *_: (0,) * nd, pipeline_mode=pl.Buffered(1))


def _adaln_kernel(c_ref, w_ref, b_ref, o_ref):
    c = c_ref[...]
    ca = (c * _sigmoid(c)).astype(BF16)
    o_ref[...] = _dot(ca, w_ref[...].astype(BF16)) + b_ref[...]


def _adaln(c, w_ada, b_ada, layer):
    bsz, d = c.shape
    depth, _, n = w_ada.shape
    return pl.pallas_call(
        _adaln_kernel,
        grid=(n // d,),
        in_specs=[pl.BlockSpec((bsz, d), lambda j: (0, 0)),
                  pl.BlockSpec((None, d, d), lambda j: (layer, 0, j)),
                  pl.BlockSpec((None, 1, d), lambda j: (layer, 0, j))],
        out_specs=pl.BlockSpec((bsz, d), lambda j: (0, j)),
        out_shape=jax.ShapeDtypeStruct((bsz, n), F32),
        compiler_params=_params(("arbitrary",)),
        name="adaln",
    )(c, w_ada, b_ada.reshape(depth, 1, n))


def _ffn_kernel(x_ref, mod_ref, nw_ref, wg_ref, wu_ref, wd_ref, *rest, final_norm):
    if final_norm:
        nf_ref, o_ref, u_scr, a_scr = rest
    else:
        o_ref, u_scr, a_scr = rest
    mod = mod_ref[0, 0]
    u_scr[...] = _norm_mod(x_ref[...], nw_ref[...], mod).astype(BF16)
    d_ff = wg_ref.shape[1]
    for c in range(d_ff // FF_CHUNK):
        sl = slice(c * FF_CHUNK, (c + 1) * FF_CHUNK)
        g = _dot(u_scr[...], wg_ref[:, sl].astype(BF16))
        up = _dot(u_scr[...], wu_ref[:, sl].astype(BF16))
        a_scr[:, sl] = (g * _sigmoid(g) * up).astype(BF16)
    y = _dot(a_scr[...], wd_ref[...].astype(BF16))
    out = x_ref[...] + (FFN_RES * mod[2:3, :]) * y
    if final_norm:
        out = _rmsnorm(out, nf_ref[...])
    o_ref[...] = out


def _ffn(h, mod4, layer, norm_w, wg, wu, wd, seq, norm_final=None):
    t, d = h.shape
    d_ff = wg.shape[1]
    tm = FFN_TILE
    tiles_per_batch = seq // tm
    in_specs = [pl.BlockSpec((tm, d), lambda i: (i, 0)),
                pl.BlockSpec((1, 1, 3, d), lambda i: (i // tiles_per_batch, layer, 0, 0)),
                _const_spec((1, d)), _const_spec((d, d_ff)), _const_spec((d, d_ff)), _const_spec((d_ff, d))]
    args = [h, mod4, norm_w, wg, wu, wd]
    if norm_final is not None:
        in_specs.append(_const_spec((1, d)))
        args.append(norm_final)
    return pl.pallas_call(
        functools.partial(_ffn_kernel, final_norm=norm_final is not None),
        grid=(t // tm,),
        in_specs=in_specs,
        out_specs=pl.BlockSpec((tm, d), lambda i: (i, 0)),
        out_shape=jax.ShapeDtypeStruct((t, d), F32),
        scratch_shapes=[pltpu.VMEM((tm, d), BF16), pltpu.VMEM((tm, d_ff), BF16)],
        compiler_params=_params(("arbitrary",)),
        name="ffn_final" if norm_final is not None else "ffn",
    )(*args)


_P_GQ, _P_GK, _P_GV, _P_GOG, _P_MK, _P_LR, _P_END = 0, 256, 512, 1024, 1536, 2048, 2176


def _inproj_kernel(h_ref, mod_ref, nw_ref, wp_ref, wqvt_ref, wlr_ref, blr_ref, gnw_ref,
                   oa_ref, mqt_ref, mk_ref, mvt_ref, kmean_ref,
                   u_scr, gq_scr, gk_scr, gv_scr, gog_scr, dec_scr, st_scr, tri_scr, lev_scr, b_scr, w_scr,
                   *, tiles_per_batch):
    i = pl.program_id(0)

    @pl.when(i == 0)
    def _():
        _gla_tables(tri_scr, lev_scr)

    @pl.when(lax.rem(i, jnp.int32(tiles_per_batch)) == 0)
    def _():
        st_scr[...] = jnp.zeros_like(st_scr)

    u_scr[...] = _norm_mod(h_ref[...], nw_ref[...], mod_ref[0, 0]).astype(BF16)

    def proj(lo, hi):
        return _dot(u_scr[...], wp_ref[:, lo:hi])

    glr = proj(_P_LR, _P_END).astype(BF16)
    z = _dot(glr, wlr_ref[...]) + blr_ref[...]
    g = (jnp.minimum(z, 0.0) - jnp.log1p(jnp.exp(-jnp.abs(z)))) * (LOG2E / GLA_TAU)
    g_hi = g.astype(BF16)
    dec_scr[:, :g.shape[1]] = g_hi
    dec_scr[:, g.shape[1]:] = (g - g_hi.astype(F32)).astype(BF16)
    cumsum, prepare, chunk = _gla_chunk_fns(gq_scr, gk_scr, gv_scr, gog_scr, dec_scr, gnw_ref, oa_ref,
                                            st_scr, tri_scr, lev_scr, b_scr, w_scr)
    n_chunks = h_ref.shape[0] // GLA_CHUNK
    for c in range(n_chunks):
        cumsum(c * GLA_CHUNK, c)
        prepare(c)

    gq_scr[...] = proj(_P_GQ, _P_GK) * 0.125
    gk_scr[...] = proj(_P_GK, _P_GV)
    gv_scr[...] = proj(_P_GV, _P_GOG).astype(BF16)
    gate = proj(_P_GOG, _P_MK)
    gog_scr[...] = gate * _sigmoid(gate)
    mk = proj(_P_MK, _P_LR)
    mk_ref[...] = mk.astype(BF16)
    kmean_ref[0] = jnp.mean(mk.reshape(-1, MOBA_BLOCK, mk.shape[1]), axis=1)
    mqvt = _dot_nt(wqvt_ref[...], u_scr[...])
    m_w = MOBA_HEADS * MOBA_DH
    ones = jnp.ones((MOBA_VROWS - MOBA_DH, MOBA_BLOCK), BF16)
    for h in range(MOBA_HEADS):
        for j in range(mqvt.shape[1] // MOBA_BLOCK):
            cols = slice(j * MOBA_BLOCK, (j + 1) * MOBA_BLOCK)
            mqt_ref[0, h, j] = (mqvt[h * MOBA_DH:(h + 1) * MOBA_DH, cols] * MOBA_QSCALE).astype(BF16)
            mvt_ref[0, h, j, :MOBA_DH, :] = mqvt[m_w + h * MOBA_DH:m_w + (h + 1) * MOBA_DH, cols].astype(BF16)
            mvt_ref[0, h, j, MOBA_DH:, :] = ones

    for c in range(n_chunks):
        chunk(c * GLA_CHUNK, c)


def _inproj(h, mod4, norm_w, wp, wvt, wlr, blr, gla_norm, bsz, seq):
    t, d = h.shape
    tm = INPROJ_TILE
    tiles_per_batch = seq // tm
    nblk = tm // MOBA_BLOCK
    nb = seq // MOBA_BLOCK
    qk_w, v_w = _P_GK - _P_GQ, _P_GOG - _P_GV
    row = lambda i: (i, 0)
    blocks = lambda i: (i // tiles_per_batch, 0, i % tiles_per_batch, 0, 0)
    out_shape = [jax.ShapeDtypeStruct((t, v_w), BF16),
                 jax.ShapeDtypeStruct((bsz, MOBA_HEADS, nb, MOBA_DH, MOBA_BLOCK), BF16),
                 jax.ShapeDtypeStruct((t, 512), BF16),
                 jax.ShapeDtypeStruct((bsz, MOBA_HEADS, nb, MOBA_VROWS, MOBA_BLOCK), BF16),
                 jax.ShapeDtypeStruct((t // tm, nblk, 512), F32)]
    out_specs = [pl.BlockSpec((tm, v_w), row),
                 pl.BlockSpec((1, MOBA_HEADS, nblk, MOBA_DH, MOBA_BLOCK), blocks),
                 pl.BlockSpec((tm, 512), row),
                 pl.BlockSpec((1, MOBA_HEADS, nblk, MOBA_VROWS, MOBA_BLOCK), blocks),
                 pl.BlockSpec((1, nblk, 512), lambda i: (i, 0, 0))]
    return pl.pallas_call(
        functools.partial(_inproj_kernel, tiles_per_batch=tiles_per_batch),
        grid=(t // tm,),
        in_specs=[pl.BlockSpec((tm, d), row),
                  pl.BlockSpec((1, 1, 3, d), lambda i: (i // tiles_per_batch, 1, 0, 0)),
                  _const_spec((1, d)), _const_spec(wp.shape), _const_spec(wvt.shape),
                  _const_spec(wlr.shape), _const_spec(blr.shape), _const_spec(gla_norm.shape)],
        out_specs=out_specs,
        out_shape=out_shape,
        scratch_shapes=[pltpu.VMEM((tm, d), BF16),
                        pltpu.VMEM((tm, qk_w), F32), pltpu.VMEM((tm, qk_w), F32),
                        pltpu.VMEM((tm, v_w), BF16), pltpu.VMEM((tm, v_w), F32),
                        pltpu.VMEM((tm, 2 * qk_w), BF16),
                        pltpu.VMEM((GLA_HEADS // 2, 2 * v_w // GLA_HEADS, 2 * qk_w // GLA_HEADS), F32),
                        pltpu.VMEM((GLA_CHUNK, GLA_CHUNK), BF16),
                        pltpu.VMEM((GLA_CHUNK, 2 * GLA_CHUNK), jnp.int32),
                        pltpu.VMEM((tm // GLA_CHUNK, GLA_CHUNK, qk_w), F32),
                        pltpu.VMEM((tm // GLA_CHUNK, len(_GLA_LEVELS) + 2, GLA_CHUNK, qk_w), F32)],
        compiler_params=_params(("arbitrary",)),
        name="inproj",
    )(h, mod4, norm_w, wp, wvt, wlr, blr, gla_norm)


_GLA_LEVELS = [1 << p for p in range(int(math.log2(GLA_CHUNK)))]


def _gla_tables(tri_scr, lev_scr):
    L = GLA_CHUNK
    tri_scr[...] = (lax.broadcasted_iota(jnp.int32, (L, L), 1)
                    <= lax.broadcasted_iota(jnp.int32, (L, L), 0)).astype(BF16)
    ti = lax.broadcasted_iota(jnp.int32, (L, 2 * L), 0)
    tj = lax.broadcasted_iota(jnp.int32, (L, 2 * L), 1) & (L - 1)
    xor = ti ^ tj
    lev = jnp.where(tj > ti, -1, 0)
    for p in range(len(_GLA_LEVELS)):
        lev = jnp.where((xor >= (1 << p)) & (tj < ti), p + 1, lev)
    lev_scr[...] = lev


def _gla_chunk_fns(q_ref, k_ref, v_ref, og_ref, g_ref, nw_ref, o_ref, st_scr, tri_scr, lev_scr, b_scr, w_scr):
    L = GLA_CHUNK
    hq = q_ref.shape[1] // GLA_HEADS
    hv = v_ref.shape[1] // GLA_HEADS
    wq = q_ref.shape[1]
    pairs = GLA_HEADS // 2
    assert 2 * hq == 128 and hv == 128 and L == 128
    levels = _GLA_LEVELS

    sub = lax.broadcasted_iota(jnp.int32, (L, wq), 0) & 7
    even_head = lax.broadcasted_iota(jnp.int32, (L, 2 * hq), 1) < hq
    stbd = ((lax.broadcasted_iota(jnp.int32, (2 * hv, 2 * hq), 0) < hv)
            == (lax.broadcasted_iota(jnp.int32, (2 * hv, 2 * hq), 1) < hq))

    def rows_bcast(slot, first, period):
        return jnp.concatenate([jnp.broadcast_to(b_scr[slot, r:r + 1, :], (period, wq))
                                for r in range(first, L, period)], axis=0)

    def midpoint(slot, s):
        if s >= 4:
            return rows_bcast(slot, s - 1, 2 * s)
        if s == 2:
            return jnp.where(sub < 4, rows_bcast(slot, 1, 8), rows_bcast(slot, 5, 8))
        return jnp.where(sub < 2, rows_bcast(slot, 0, 8),
                         jnp.where(sub < 4, rows_bcast(slot, 2, 8),
                                   jnp.where(sub < 6, rows_bcast(slot, 4, 8), rows_bcast(slot, 6, 8))))

    def cumsum(r0, slot):
        bb = _dot(tri_scr[...], g_ref[pl.ds(r0, L), :])
        b_scr[slot] = bb[:, :wq] + bb[:, wq:]

    def pair_scores(qx, kx, p):
        pl_ = slice(p * 2 * hq, (p + 1) * 2 * hq)
        kp = kx[:, pl_]
        kk = jnp.concatenate([jnp.where(even_head, kp, 0.0), jnp.where(even_head, 0.0, kp)], axis=0)
        return _dot_nt(qx[:, pl_].astype(BF16), kk.astype(BF16))

    def prepare(slot):
        b = b_scr[slot]
        for li, s in enumerate(levels):
            w_scr[slot, li] = jnp.exp2(-jnp.abs(b - midpoint(slot, s)))
        w_scr[slot, len(levels)] = jnp.exp2(b)
        w_scr[slot, len(levels) + 1] = jnp.exp2(b[L - 1:L, :] - b)

    def chunk(r0, slot):
        q = q_ref[pl.ds(r0, L), :]
        k = k_ref[pl.ds(r0, L), :]
        v = v_ref[pl.ds(r0, L), :]
        q_in = (q * w_scr[slot, len(levels)]).astype(BF16)
        k_out = (k * w_scr[slot, len(levels) + 1]).astype(BF16)
        dec = jnp.exp2(b_scr[slot, L - 1:L, :])

        att = [jnp.where(lev_scr[...] == 0, pair_scores(q, k, p), 0.0) for p in range(pairs)]
        for li in range(len(levels)):
            w = w_scr[slot, li]
            hit = lev_scr[...] == li + 1
            att = [jnp.where(hit, pair_scores(q * w, k * w, p), att[p]) for p in range(pairs)]

        for p in range(pairs):
            ql = slice(p * 2 * hq, (p + 1) * 2 * hq)
            vl = slice(p * 2 * hv, (p + 1) * 2 * hv)
            st = st_scr[p]
            vp = v[:, vl]
            zero = jnp.zeros((L, hv), BF16)
            v_diag = jnp.concatenate([jnp.concatenate([vp[:, :hv], zero], axis=1),
                                      jnp.concatenate([zero, vp[:, hv:]], axis=1)], axis=0)
            o = _dot_nt(q_in[:, ql], st.astype(BF16)) + _dot(att[p].astype(BF16), v_diag)
            st_scr[p] = st * dec[:, ql] + jnp.where(stbd, _dot_tn(vp, k_out[:, ql]), 0.0)
            for hh in range(2):
                sl = slice(vl.start + hh * hv, vl.start + (hh + 1) * hv)
                oh = _rmsnorm(o[:, hh * hv:(hh + 1) * hv], nw_ref[...])
                o_ref[pl.ds(r0, L), sl] = (oh * og_ref[pl.ds(r0, L), sl]).astype(BF16)

    return cumsum, prepare, chunk


def _rel_buckets(max_dist):
    n = np.arange(max_dist)
    nf = np.maximum(n, 1).astype(np.float64)
    large = MAX_EXACT + (np.log(nf / MAX_EXACT) / math.log(MAX_DISTANCE / MAX_EXACT)
                         * (NUM_BUCKETS - MAX_EXACT)).astype(np.int64)
    return np.where(n < MAX_EXACT, n, np.minimum(large, NUM_BUCKETS - 1))


_TBL_OWN, _TBL_PREV, _TBL_FAR = 0, 1, 2


def _bias_kernel(rb_ref, o_ref):
    h = pl.program_id(0)
    blk = MOBA_BLOCK
    buckets = _rel_buckets(2 * blk)
    assert (np.diff(buckets) >= 0).all()
    starts = {b: int(np.argmax(buckets == b)) for b in range(NUM_BUCKETS) if (buckets == b).any()}
    ki = lax.broadcasted_iota(jnp.int32, (blk, blk), 0)
    qi = lax.broadcasted_iota(jnp.int32, (blk, blk), 1)
    far = rb_ref[h, NUM_BUCKETS - 1]
    for tbl, base in ((_TBL_OWN, 0), (_TBL_PREV, blk)):
        dist = qi - ki + base
        val = jnp.full((blk, blk), rb_ref[h, 0], F32)
        for b in sorted(starts):
            if b > 0:
                val = jnp.where(dist >= starts[b], rb_ref[h, b], val)
        val = (val - far) * LOG2E
        if tbl == _TBL_OWN:
            val = jnp.where(dist >= 0, val, NEG)
        o_ref[0, tbl] = val
    o_ref[0, _TBL_FAR] = jnp.zeros((blk, blk), F32)


def _bias_tables(rel_bias):
    nh = rel_bias.shape[0]
    return pl.pallas_call(
        _bias_kernel,
        grid=(nh,),
        in_specs=[pl.BlockSpec(memory_space=pltpu.SMEM)],
        out_specs=pl.BlockSpec((1, 3, MOBA_BLOCK, MOBA_BLOCK), lambda h: (h, 0, 0, 0)),
        out_shape=jax.ShapeDtypeStruct((nh, 3, MOBA_BLOCK, MOBA_BLOCK), F32),
        compiler_params=_params(("arbitrary",)),
        name="moba_bias",
    )(rel_bias)


def _moba_kernel(qt_ref, k_ref, vt_ref, km_ref, bias_ref, o_ref,
                 off_scr, m_scr, al_scr, acc_scr, p_scr, s_scr):
    n = pl.program_id(1)
    blk = MOBA_BLOCK
    nb = km_ref.shape[1]
    heads = range(qt_ref.shape[0] * MOBA_HEADS)
    row = [h // MOBA_HEADS for h in heads]
    hsl = [slice((h % MOBA_HEADS) * MOBA_DH, (h % MOBA_HEADS + 1) * MOBA_DH) for h in heads]
    qt = lambda h: qt_ref[row[h], h % MOBA_HEADS, 0]

    brow = lax.broadcasted_iota(jnp.int32, (nb, len(heads) * blk), 0).astype(F32)
    gate = jnp.concatenate([_dot(km_ref[row[h], :, hsl[h]].astype(BF16), qt(h)) for h in heads],
                           axis=1)
    gate = jnp.where(brow < n.astype(F32), gate, NEG)
    off = jnp.full(gate.shape, NEG, F32)
    for _ in range(MOBA_TOPK):
        best = jnp.max(gate, axis=0, keepdims=True)
        first = jnp.min(jnp.where(gate == best, brow, float(nb)), axis=0, keepdims=True)
        pick = (brow == first) & (best > NEG)
        off = jnp.where(pick, 0.0, off)
        gate = jnp.where(pick, NEG, gate)
    for h in heads:
        off_scr[h] = off[:, h * blk:(h + 1) * blk]
        m_scr[h] = jnp.full((1, blk), NEG, F32)
        acc_scr[h] = jnp.zeros((MOBA_VROWS, blk), F32)
        al_scr[1, h] = jnp.ones((1, blk), F32)
        p_scr[1, h] = jnp.zeros((blk, blk), BF16)

    def qk(h, j, slot):
        r0 = pl.multiple_of(j * blk, blk)
        s_scr[slot, h] = _dot(k_ref[row[h], pl.ds(r0, blk), hsl[h]], qt(h))

    def pv(h, j, slot):
        acc_scr[h] = al_scr[slot, h] * acc_scr[h] + _dot(vt_ref[row[h], h % MOBA_HEADS, j], p_scr[slot, h])

    def step(slot, j, j_next, j_prev, table, live):
        for h in heads:
            pv(h, j_prev, 1 - slot)
            if j_next is not None:
                qk(h, j_next, 1 - slot)
        for h in heads:
            sh = s_scr[slot, h]
            if table is not None:
                sh = sh + bias_ref[h % MOBA_HEADS, table]
            top = jnp.max(sh, axis=0, keepdims=True)
            m_old = m_scr[h]
            if live is not None:
                off = jnp.where(live, off_scr[h, pl.ds(j, 1), :], NEG)
                m_new = jnp.maximum(m_old, top + off)
                m_sub = jnp.where(off < 0.0, -NEG, m_new)
            else:
                m_new = jnp.maximum(m_old, top)
                m_sub = m_new
            m_scr[h] = m_new
            al_scr[slot, h] = jnp.exp2(m_old - m_new)
            p_scr[slot, h] = jnp.exp2(sh - m_sub).astype(BF16)

    n_far = jnp.maximum(n - 1, 0)
    last_far = jnp.maximum(n_far - 1, 0)
    j_before = jnp.maximum(n - 1, 0)
    far_blk = lambda u: jnp.clip(u, 0, last_far)

    for h in heads:
        qk(h, jnp.where(n_far > 0, 0, j_before), 0)

    def far_steps(u, count):
        for c in range(count):
            nxt = far_blk(u + c + 1)
            if c == count - 1:
                nxt = jnp.where(u + count < n_far, nxt, j_before)
            step(c % 2, far_blk(u + c), nxt, far_blk(u + c - 1), None, u + c < n_far)

    done = 0
    for size in MOBA_BODIES:
        left = n_far - done
        trips = (left + 1) // 2 if size == 2 else left // size

        def body(i, carry, size=size, done=done):
            far_steps(done + size * i, size)
            return carry

        lax.fori_loop(0, trips, body, 0)
        done = done + trips * size
    step(0, j_before, n, last_far, _TBL_PREV, n >= 1)
    step(1, n, None, j_before, _TBL_OWN, None)
    for h in heads:
        pv(h, n, 1)
        acc = acc_scr[h]
        o_ref[row[h], :, hsl[h]] = (acc[:MOBA_DH] / acc[MOBA_DH:MOBA_DH + 1]).T.astype(BF16)


def _moba(mqt, mk, mvt, kmean, bias, bsz, seq):
    t, w = mk.shape
    blk = MOBA_BLOCK
    nb = seq // blk
    g = MOBA_ROWS if bsz % MOBA_ROWS == 0 else 1
    units = g * MOBA_HEADS
    resident = lambda shape: pl.BlockSpec(shape, lambda b, n: (b,) + (0,) * (len(shape) - 1),
                                          pipeline_mode=pl.Buffered(1))
    out = pl.pallas_call(
        _moba_kernel,
        grid=(bsz // g, nb),
        in_specs=[pl.BlockSpec((g, MOBA_HEADS, 1, MOBA_DH, blk), lambda b, n: (b, 0, n, 0, 0)),
                  resident((g, seq, w)),
                  resident((g, MOBA_HEADS, nb, MOBA_VROWS, blk)),
                  pl.BlockSpec((g, nb, w), lambda b, n: (b, 0, 0)),
                  _const_spec(bias.shape)],
        out_specs=pl.BlockSpec((g, blk, w), lambda b, n: (b, n, 0)),
        out_shape=jax.ShapeDtypeStruct((bsz, seq, w), BF16),
        scratch_shapes=[pltpu.VMEM((units, nb, blk), F32),
                        pltpu.VMEM((units, 1, blk), F32),
                        pltpu.VMEM((2, units, 1, blk), F32),
                        pltpu.VMEM((units, MOBA_VROWS, blk), F32),
                        pltpu.VMEM((2, units, blk, blk), BF16),
                        pltpu.VMEM((2, units, blk, blk), F32)],
        compiler_params=_params(("arbitrary", "arbitrary")),
        name="moba",
    )(mqt, mk.reshape(bsz, seq, w), mvt, kmean, bias)
    return out.reshape(t, w)


def _merge_kernel(h_ref, mod_ref, nw_ref, oa_ref, ob_ref, wga_ref, wgb_ref, wa_ref, wb_ref, wo_ref,
                  o_ref, u_scr):
    mod = mod_ref[0, 0]
    u_scr[...] = _norm_mod(h_ref[...], nw_ref[...], mod).astype(BF16)
    ya = _dot(oa_ref[...], wa_ref[...])
    yb = _dot(ob_ref[...], wb_ref[...])
    merged = _sigmoid(_dot(u_scr[...], wga_ref[...])) * ya + _sigmoid(_dot(u_scr[...], wgb_ref[...])) * yb
    o_ref[...] = h_ref[...] + mod[2:3, :] * _dot(merged.astype(BF16), wo_ref[...])


def _merge(h, mod4, norm_w, oa, ob, wga, wgb, wa, wb, wo, tiles_per_batch):
    t, d = h.shape
    tm = TOKEN_TILE
    row = lambda i: (i, 0)
    return pl.pallas_call(
        _merge_kernel,
        grid=(t // tm,),
        in_specs=[pl.BlockSpec((tm, d), row),
                  pl.BlockSpec((1, 1, 3, d), lambda i: (i // tiles_per_batch, 1, 0, 0)),
                  _const_spec((1, d)),
                  pl.BlockSpec((tm, oa.shape[1]), row), pl.BlockSpec((tm, ob.shape[1]), row),
                  _const_spec(wga.shape), _const_spec(wgb.shape), _const_spec(wa.shape),
                  _const_spec(wb.shape), _const_spec(wo.shape)],
        out_specs=pl.BlockSpec((tm, d), row),
        out_shape=jax.ShapeDtypeStruct((t, d), F32),
        scratch_shapes=[pltpu.VMEM((tm, d), BF16)],
        compiler_params=_params(("arbitrary",)),
        name="merge",
    )(h, mod4, norm_w, oa, ob, wga, wgb, wa, wb, wo)


def kernel(x, c, w_ada, b_ada, norm_ff1, w_ff1_gate, w_ff1_up, w_ff1_down, norm_mix, w_in, w_gla_lr,
           b_gla_lr, gla_norm, rel_bias, w_br_gla, w_br_moba, w_out, norm_ff2, w_ff2_gate, w_ff2_up,
           w_ff2_down, norm_final):
    bsz, seq, d = x.shape
    depth = w_ada.shape[0]
    t = bsz * seq
    tiles_per_batch = seq // TOKEN_TILE
    qk_w = GLA_HEADS * (d // 16)
    v_w = GLA_HEADS * (d // 8)
    m_w = MOBA_HEADS * MOBA_DH
    offs = np.cumsum([0, qk_w, qk_w, v_w, GLA_LOWRANK, v_w, m_w, m_w, m_w, d, d])
    bias = _bias_tables(rel_bias)
    h = x.reshape(t, d)
    c_act_in = c
    for l in range(depth):
        mod4 = _adaln(c_act_in, w_ada, b_ada, l).reshape(bsz, 3, 3, d)
        cast = lambda w: w.astype(BF16)
        h = _ffn(h, mod4, 0, norm_ff1[l].reshape(1, d), w_ff1_gate[l], w_ff1_up[l], w_ff1_down[l], seq)
        wi = w_in[l]
        seg = lambda i: wi[:, offs[i]:offs[i + 1]]
        lr_pad = jnp.pad(seg(3), ((0, 0), (0, 128 - GLA_LOWRANK)))
        wp = cast(jnp.concatenate([seg(0), seg(1), seg(2), seg(4), seg(6), lr_pad], axis=1))
        wqvt = cast(jnp.concatenate([seg(5), seg(7)], axis=1).T)
        wlr = cast(jnp.pad(w_gla_lr[l], ((0, 128 - GLA_LOWRANK), (0, 0))))
        o_a, mqt, mk, mvt, kmean = _inproj(
            h, mod4, norm_mix[l].reshape(1, d), wp, wqvt, wlr, b_gla_lr[l].reshape(1, qk_w),
            gla_norm[l].reshape(1, -1), bsz, seq)
        o_b = _moba(mqt, mk, mvt, kmean.reshape(bsz, seq // MOBA_BLOCK, m_w), bias, bsz, seq)
        h = _merge(h, mod4, norm_mix[l].reshape(1, d), o_a, o_b, cast(seg(8)), cast(seg(9)),
                   cast(w_br_gla[l]), cast(w_br_moba[l]), cast(w_out[l]), tiles_per_batch)
        last = l == depth - 1
        h = _ffn(h, mod4, 2, norm_ff2[l].reshape(1, d), w_ff2_gate[l], w_ff2_up[l], w_ff2_down[l], seq,
                 norm_final.reshape(1, d) if last else None)
    return h.reshape(bsz, seq, d)
```

```python
import functools
import math

import numpy as np
import jax
import jax.numpy as jnp
from jax import lax
from jax.experimental import pallas as pl
from jax.experimental.pallas import tpu as pltpu

F32 = jnp.float32
BF16 = jnp.bfloat16

EPS = 1e-6
FFN_RES = 0.5
GLA_HEADS = 4
GLA_TAU = 16.0
GLA_LOWRANK = 16
GLA_CHUNK = 128
MOBA_HEADS = 4
MOBA_DH = 128
MOBA_BLOCK = 256
MOBA_TOPK = 3
MOBA_VROWS = MOBA_DH + 16
MOBA_ROWS = 2
MOBA_BODIES = (16, 8, 4, 2)
NUM_BUCKETS = 32
MAX_DISTANCE = 128
MAX_EXACT = NUM_BUCKETS // 2
NEG = -1e30
LOG2E = 1.4426950408889634
MOBA_QSCALE = LOG2E * MOBA_DH ** -0.5
TOKEN_TILE = 1024
FFN_TILE = 1024
INPROJ_TILE = 1024
FF_CHUNK = 256
V7X_VMEM_BYTES = 64 * 1024 * 1024
VMEM_LIMIT = V7X_VMEM_BYTES - 3 * 1024 * 1024


def _dot(a, b):
    return jnp.dot(a, b, preferred_element_type=F32)


def _dot_nt(a, b):
    return lax.dot_general(a, b, (((1,), (1,)), ((), ())), preferred_element_type=F32)


def _dot_tn(a, b):
    return lax.dot_general(a, b, (((0,), (0,)), ((), ())), preferred_element_type=F32)


def _sigmoid(x):
    return 1.0 / (1.0 + jnp.exp(-x))


def _rmsnorm(x, w):
    return x * lax.rsqrt(jnp.mean(x * x, axis=-1, keepdims=True) + EPS) * w


def _norm_mod(x, w, mod):
    return _rmsnorm(x, w) * (1.0 + mod[1:2, :]) + mod[0:1, :]


def _params(sem):
    return pltpu.CompilerParams(dimension_semantics=sem, vmem_limit_bytes=VMEM_LIMIT)


def _const_spec(shape):
    nd = len(shape)
    return pl.BlockSpec(shape, lambda *_: (0,) * nd, pipeline_mode=pl.Buffered(1))


def _adaln_kernel(c_ref, w_ref, b_ref, o_ref):
    c = c_ref[...]
    ca = (c * _sigmoid(c)).astype(BF16)
    o_ref[...] = _dot(ca, w_ref[...].astype(BF16)) + b_ref[...]


def _adaln(c, w_ada, b_ada, layer):
    bsz, d = c.shape
    depth, _, n = w_ada.shape
    return pl.pallas_call(
        _adaln_kernel,
        grid=(n // d,),
        in_specs=[pl.BlockSpec((bsz, d), lambda j: (0, 0)),
                  pl.BlockSpec((None, d, d), lambda j: (layer, 0, j)),
                  pl.BlockSpec((None, 1, d), lambda j: (layer, 0, j))],
        out_specs=pl.BlockSpec((bsz, d), lambda j: (0, j)),
        out_shape=jax.ShapeDtypeStruct((bsz, n), F32),
        compiler_params=_params(("arbitrary",)),
        name="adaln",
    )(c, w_ada, b_ada.reshape(depth, 1, n))


def _ffn_kernel(x_ref, mod_ref, nw_ref, wg_ref, wu_ref, wd_ref, *rest, final_norm):
    if final_norm:
        nf_ref, o_ref, u_scr, a_scr = rest
    else:
        o_ref, u_scr, a_scr = rest
    mod = mod_ref[0, 0]
    u_scr[...] = _norm_mod(x_ref[...], nw_ref[...], mod).astype(BF16)
    d_ff = wg_ref.shape[1]
    for c in range(d_ff // FF_CHUNK):
        sl = slice(c * FF_CHUNK, (c + 1) * FF_CHUNK)
        g = _dot(u_scr[...], wg_ref[:, sl].astype(BF16))
        up = _dot(u_scr[...], wu_ref[:, sl].astype(BF16))
        a_scr[:, sl] = (g * _sigmoid(g) * up).astype(BF16)
    y = _dot(a_scr[...], wd_ref[...].astype(BF16))
    out = x_ref[...] + (FFN_RES * mod[2:3, :]) * y
    if final_norm:
        out = _rmsnorm(out, nf_ref[...])
    o_ref[...] = out


def _ffn(h, mod4, layer, norm_w, wg, wu, wd, seq, norm_final=None):
    t, d = h.shape
    d_ff = wg.shape[1]
    tm = FFN_TILE
    tiles_per_batch = seq // tm
    in_specs = [pl.BlockSpec((tm, d), lambda i: (i, 0)),
                pl.BlockSpec((1, 1, 3, d), lambda i: (i // tiles_per_batch, layer, 0, 0)),
                _const_spec((1, d)), _const_spec((d, d_ff)), _const_spec((d, d_ff)), _const_spec((d_ff, d))]
    args = [h, mod4, norm_w, wg, wu, wd]
    if norm_final is not None:
        in_specs.append(_const_spec((1, d)))
        args.append(norm_final)
    return pl.pallas_call(
        functools.partial(_ffn_kernel, final_norm=norm_final is not None),
        grid=(t // tm,),
        in_specs=in_specs,
        out_specs=pl.BlockSpec((tm, d), lambda i: (i, 0)),
        out_shape=jax.ShapeDtypeStruct((t, d), F32),
        scratch_shapes=[pltpu.VMEM((tm, d), BF16), pltpu.VMEM((tm, d_ff), BF16)],
        compiler_params=_params(("arbitrary",)),
        name="ffn_final" if norm_final is not None else "ffn",
    )(*args)


_P_GQ, _P_GK, _P_GV, _P_GOG, _P_MK, _P_LR, _P_END = 0, 256, 512, 1024, 1536, 2048, 2176


def _inproj_kernel(h_ref, mod_ref, nw_ref, wp_ref, wqvt_ref, wlr_ref, blr_ref, gnw_ref,
                   oa_ref, mqt_ref, mk_ref, mvt_ref, kmean_ref,
                   u_scr, gq_scr, gk_scr, gv_scr, gog_scr, dec_scr, st_scr, tri_scr, lev_scr, b_scr, w_scr,
                   *, tiles_per_batch):
    i = pl.program_id(0)

    @pl.when(i == 0)
    def _():
        _gla_tables(tri_scr, lev_scr)

    @pl.when(lax.rem(i, jnp.int32(tiles_per_batch)) == 0)
    def _():
        st_scr[...] = jnp.zeros_like(st_scr)

    u_scr[...] = _norm_mod(h_ref[...], nw_ref[...], mod_ref[0, 0]).astype(BF16)

    def proj(lo, hi):
        return _dot(u_scr[...], wp_ref[:, lo:hi])

    glr = proj(_P_LR, _P_END).astype(BF16)
    z = _dot(glr, wlr_ref[...]) + blr_ref[...]
    g = (jnp.minimum(z, 0.0) - jnp.log1p(jnp.exp(-jnp.abs(z)))) * (LOG2E / GLA_TAU)
    g_hi = g.astype(BF16)
    dec_scr[:, :g.shape[1]] = g_hi
    dec_scr[:, g.shape[1]:] = (g - g_hi.astype(F32)).astype(BF16)
    cumsum, prepare, chunk = _gla_chunk_fns(gq_scr, gk_scr, gv_scr, gog_scr, dec_scr, gnw_ref, oa_ref,
                                            st_scr, tri_scr, lev_scr, b_scr, w_scr)
    n_chunks = h_ref.shape[0] // GLA_CHUNK
    for c in range(n_chunks):
        cumsum(c * GLA_CHUNK, c)
        prepare(c)

    gq_scr[...] = proj(_P_GQ, _P_GK) * ((_P_GK - _P_GQ) // GLA_HEADS) ** -0.5
    gk_scr[...] = proj(_P_GK, _P_GV)
    gv_scr[...] = proj(_P_GV, _P_GOG).astype(BF16)
    gate = proj(_P_GOG, _P_MK)
    gog_scr[...] = gate * _sigmoid(gate)
    mk = proj(_P_MK, _P_LR)
    mk_ref[...] = mk.astype(BF16)
    kmean_ref[0] = jnp.mean(mk.reshape(-1, MOBA_BLOCK, mk.shape[1]), axis=1)
    mqvt = _dot_nt(wqvt_ref[...], u_scr[...])
    m_w = MOBA_HEADS * MOBA_DH
    ones = jnp.ones((MOBA_VROWS - MOBA_DH, MOBA_BLOCK), BF16)
    for h in range(MOBA_HEADS):
        for j in range(mqvt.shape[1] // MOBA_BLOCK):
            cols = slice(j * MOBA_BLOCK, (j + 1) * MOBA_BLOCK)
            mqt_ref[0, h, j] = (mqvt[h * MOBA_DH:(h + 1) * MOBA_DH, cols] * MOBA_QSCALE).astype(BF16)
            mvt_ref[0, h, j, :MOBA_DH, :] = mqvt[m_w + h * MOBA_DH:m_w + (h + 1) * MOBA_DH, cols].astype(BF16)
            mvt_ref[0, h, j, MOBA_DH:, :] = ones

    for c in range(n_chunks):
        chunk(c * GLA_CHUNK, c)


def _inproj(h, mod4, norm_w, wp, wvt, wlr, blr, gla_norm, bsz, seq):
    t, d = h.shape
    tm = INPROJ_TILE
    tiles_per_batch = seq // tm
    nblk = tm // MOBA_BLOCK
    nb = seq // MOBA_BLOCK
    qk_w, v_w, m_w = _P_GK - _P_GQ, _P_GOG - _P_GV, _P_LR - _P_MK
    row = lambda i: (i, 0)
    blocks = lambda i: (i // tiles_per_batch, 0, i % tiles_per_batch, 0, 0)
    out_shape = [jax.ShapeDtypeStruct((t, v_w), BF16),
                 jax.ShapeDtypeStruct((bsz, MOBA_HEADS, nb, MOBA_DH, MOBA_BLOCK), BF16),
                 jax.ShapeDtypeStruct((t, m_w), BF16),
                 jax.ShapeDtypeStruct((bsz, MOBA_HEADS, nb, MOBA_VROWS, MOBA_BLOCK), BF16),
                 jax.ShapeDtypeStruct((t // tm, nblk, m_w), F32)]
    out_specs = [pl.BlockSpec((tm, v_w), row),
                 pl.BlockSpec((1, MOBA_HEADS, nblk, MOBA_DH, MOBA_BLOCK), blocks),
                 pl.BlockSpec((tm, m_w), row),
                 pl.BlockSpec((1, MOBA_HEADS, nblk, MOBA_VROWS, MOBA_BLOCK), blocks),
                 pl.BlockSpec((1, nblk, m_w), lambda i: (i, 0, 0))]
    return pl.pallas_call(
        functools.partial(_inproj_kernel, tiles_per_batch=tiles_per_batch),
        grid=(t // tm,),
        in_specs=[pl.BlockSpec((tm, d), row),
                  pl.BlockSpec((1, 1, 3, d), lambda i: (i // tiles_per_batch, 1, 0, 0)),
                  _const_spec((1, d)), _const_spec(wp.shape), _const_spec(wvt.shape),
                  _const_spec(wlr.shape), _const_spec(blr.shape), _const_spec(gla_norm.shape)],
        out_specs=out_specs,
        out_shape=out_shape,
        scratch_shapes=[pltpu.VMEM((tm, d), BF16),
                        pltpu.VMEM((tm, qk_w), F32), pltpu.VMEM((tm, qk_w), F32),
                        pltpu.VMEM((tm, v_w), BF16), pltpu.VMEM((tm, v_w), F32),
                        pltpu.VMEM((tm, 2 * qk_w), BF16),
                        pltpu.VMEM((GLA_HEADS // 2, 2 * v_w // GLA_HEADS, 2 * qk_w // GLA_HEADS), F32),
                        pltpu.VMEM((GLA_CHUNK, GLA_CHUNK), BF16),
                        pltpu.VMEM((GLA_CHUNK, 2 * GLA_CHUNK), jnp.int32),
                        pltpu.VMEM((tm // GLA_CHUNK, GLA_CHUNK, qk_w), F32),
                        pltpu.VMEM((tm // GLA_CHUNK, len(_GLA_LEVELS) + 2, GLA_CHUNK, qk_w), F32)],
        compiler_params=_params(("arbitrary",)),
        name="inproj",
    )(h, mod4, norm_w, wp, wvt, wlr, blr, gla_norm)


_GLA_LEVELS = [1 << p for p in range(int(math.log2(GLA_CHUNK)))]


def _gla_tables(tri_scr, lev_scr):
    L = GLA_CHUNK
    tri_scr[...] = (lax.broadcasted_iota(jnp.int32, (L, L), 1)
                    <= lax.broadcasted_iota(jnp.int32, (L, L), 0)).astype(BF16)
    ti = lax.broadcasted_iota(jnp.int32, (L, 2 * L), 0)
    tj = lax.broadcasted_iota(jnp.int32, (L, 2 * L), 1) & (L - 1)
    xor = ti ^ tj
    lev = jnp.where(tj > ti, -1, 0)
    for p in range(len(_GLA_LEVELS)):
        lev = jnp.where((xor >= (1 << p)) & (tj < ti), p + 1, lev)
    lev_scr[...] = lev


def _gla_chunk_fns(q_ref, k_ref, v_ref, og_ref, g_ref, nw_ref, o_ref, st_scr, tri_scr, lev_scr, b_scr, w_scr):
    L = GLA_CHUNK
    hq = q_ref.shape[1] // GLA_HEADS
    hv = v_ref.shape[1] // GLA_HEADS
    wq = q_ref.shape[1]
    pairs = GLA_HEADS // 2
    assert 2 * hq == 128 and hv == 128 and L == 128
    levels = _GLA_LEVELS

    sub = lax.broadcasted_iota(jnp.int32, (L, wq), 0) & 7
    even_head = lax.broadcasted_iota(jnp.int32, (L, 2 * hq), 1) < hq
    stbd = ((lax.broadcasted_iota(jnp.int32, (2 * hv, 2 * hq), 0) < hv)
            == (lax.broadcasted_iota(jnp.int32, (2 * hv, 2 * hq), 1) < hq))

    def rows_bcast(slot, first, period):
        return jnp.concatenate([jnp.broadcast_to(b_scr[slot, r:r + 1, :], (period, wq))
                                for r in range(first, L, period)], axis=0)

    def midpoint(slot, s):
        if s >= 4:
            return rows_bcast(slot, s - 1, 2 * s)
        if s == 2:
            return jnp.where(sub < 4, rows_bcast(slot, 1, 8), rows_bcast(slot, 5, 8))
        return jnp.where(sub < 2, rows_bcast(slot, 0, 8),
                         jnp.where(sub < 4, rows_bcast(slot, 2, 8),
                                   jnp.where(sub < 6, rows_bcast(slot, 4, 8), rows_bcast(slot, 6, 8))))

    def cumsum(r0, slot):
        bb = _dot(tri_scr[...], g_ref[pl.ds(r0, L), :])
        b_scr[slot] = bb[:, :wq] + bb[:, wq:]

    def pair_scores(qx, kx, p):
        pl_ = slice(p * 2 * hq, (p + 1) * 2 * hq)
        kp = kx[:, pl_]
        kk = jnp.concatenate([jnp.where(even_head, kp, 0.0), jnp.where(even_head, 0.0, kp)], axis=0)
        return _dot_nt(qx[:, pl_].astype(BF16), kk.astype(BF16))

    def prepare(slot):
        b = b_scr[slot]
        for li, s in enumerate(levels):
            w_scr[slot, li] = jnp.exp2(-jnp.abs(b - midpoint(slot, s)))
        w_scr[slot, len(levels)] = jnp.exp2(b)
        w_scr[slot, len(levels) + 1] = jnp.exp2(b[L - 1:L, :] - b)

    def chunk(r0, slot):
        q = q_ref[pl.ds(r0, L), :]
        k = k_ref[pl.ds(r0, L), :]
        v = v_ref[pl.ds(r0, L), :]
        q_in = (q * w_scr[slot, len(levels)]).astype(BF16)
        k_out = (k * w_scr[slot, len(levels) + 1]).astype(BF16)
        dec = jnp.exp2(b_scr[slot, L - 1:L, :])

        att = [jnp.where(lev_scr[...] == 0, pair_scores(q, k, p), 0.0) for p in range(pairs)]
        for li in range(len(levels)):
            w = w_scr[slot, li]
            hit = lev_scr[...] == li + 1
            att = [jnp.where(hit, pair_scores(q * w, k * w, p), att[p]) for p in range(pairs)]

        for p in range(pairs):
            ql = slice(p * 2 * hq, (p + 1) * 2 * hq)
            vl = slice(p * 2 * hv, (p + 1) * 2 * hv)
            st = st_scr[p]
            vp = v[:, vl]
            zero = jnp.zeros((L, hv), BF16)
            v_diag = jnp.concatenate([jnp.concatenate([vp[:, :hv], zero], axis=1),
                                      jnp.concatenate([zero, vp[:, hv:]], axis=1)], axis=0)
            o = _dot_nt(q_in[:, ql], st.astype(BF16)) + _dot(att[p].astype(BF16), v_diag)
            st_scr[p] = st * dec[:, ql] + jnp.where(stbd, _dot_tn(vp, k_out[:, ql]), 0.0)
            for hh in range(2):
                sl = slice(vl.start + hh * hv, vl.start + (hh + 1) * hv)
                oh = _rmsnorm(o[:, hh * hv:(hh + 1) * hv], nw_ref[...])
                o_ref[pl.ds(r0, L), sl] = (oh * og_ref[pl.ds(r0, L), sl]).astype(BF16)

    return cumsum, prepare, chunk


def _rel_buckets(max_dist):
    n = np.arange(max_dist)
    nf = np.maximum(n, 1).astype(np.float64)
    large = MAX_EXACT + (np.log(nf / MAX_EXACT) / math.log(MAX_DISTANCE / MAX_EXACT)
                         * (NUM_BUCKETS - MAX_EXACT)).astype(np.int64)
    return np.where(n < MAX_EXACT, n, np.minimum(large, NUM_BUCKETS - 1))


_TBL_OWN, _TBL_PREV, _TBL_FAR = 0, 1, 2


def _bias_kernel(rb_ref, o_ref):
    h = pl.program_id(0)
    blk = MOBA_BLOCK
    buckets = _rel_buckets(2 * blk)
    assert (np.diff(buckets) >= 0).all()
    starts = {b: int(np.argmax(buckets == b)) for b in range(NUM_BUCKETS) if (buckets == b).any()}
    ki = lax.broadcasted_iota(jnp.int32, (blk, blk), 0)
    qi = lax.broadcasted_iota(jnp.int32, (blk, blk), 1)
    far = rb_ref[h, NUM_BUCKETS - 1]
    for tbl, base in ((_TBL_OWN, 0), (_TBL_PREV, blk)):
        dist = qi - ki + base
        val = jnp.full((blk, blk), rb_ref[h, 0], F32)
        for b in sorted(starts):
            if b > 0:
                val = jnp.where(dist >= starts[b], rb_ref[h, b], val)
        val = (val - far) * LOG2E
        if tbl == _TBL_OWN:
            val = jnp.where(dist >= 0, val, NEG)
        o_ref[0, tbl] = val
    o_ref[0, _TBL_FAR] = jnp.zeros((blk, blk), F32)


def _bias_tables(rel_bias):
    nh = rel_bias.shape[0]
    return pl.pallas_call(
        _bias_kernel,
        grid=(nh,),
        in_specs=[pl.BlockSpec(memory_space=pltpu.SMEM)],
        out_specs=pl.BlockSpec((1, 3, MOBA_BLOCK, MOBA_BLOCK), lambda h: (h, 0, 0, 0)),
        out_shape=jax.ShapeDtypeStruct((nh, 3, MOBA_BLOCK, MOBA_BLOCK), F32),
        compiler_params=_params(("arbitrary",)),
        name="moba_bias",
    )(rel_bias)


def _moba_kernel(qt_ref, k_ref, vt_ref, km_ref, bias_ref, o_ref,
                 off_scr, m_scr, al_scr, acc_scr, p_scr, s_scr):
    n = pl.program_id(1)
    blk = MOBA_BLOCK
    nb = km_ref.shape[1]
    heads = range(qt_ref.shape[0] * MOBA_HEADS)
    row = [h // MOBA_HEADS for h in heads]
    hsl = [slice((h % MOBA_HEADS) * MOBA_DH, (h % MOBA_HEADS + 1) * MOBA_DH) for h in heads]
    qt = lambda h: qt_ref[row[h], h % MOBA_HEADS, 0]

    brow = lax.broadcasted_iota(jnp.int32, (nb, len(heads) * blk), 0).astype(F32)
    gate = jnp.concatenate([_dot(km_ref[row[h], :, hsl[h]].astype(BF16), qt(h)) for h in heads],
                           axis=1)
    gate = jnp.where(brow < n.astype(F32), gate, NEG)
    off = jnp.full(gate.shape, NEG, F32)
    for _ in range(MOBA_TOPK):
        best = jnp.max(gate, axis=0, keepdims=True)
        first = jnp.min(jnp.where(gate == best, brow, float(nb)), axis=0, keepdims=True)
        pick = (brow == first) & (best > NEG)
        off = jnp.where(pick, 0.0, off)
        gate = jnp.where(pick, NEG, gate)
    for h in heads:
        off_scr[h] = off[:, h * blk:(h + 1) * blk]
        m_scr[h] = jnp.full((1, blk), NEG, F32)
        acc_scr[h] = jnp.zeros((MOBA_VROWS, blk), F32)
        al_scr[1, h] = jnp.ones((1, blk), F32)
        p_scr[1, h] = jnp.zeros((blk, blk), BF16)

    def qk(h, j, slot):
        r0 = pl.multiple_of(j * blk, blk)
        s_scr[slot, h] = _dot(k_ref[row[h], pl.ds(r0, blk), hsl[h]], qt(h))

    def pv(h, j, slot):
        acc_scr[h] = al_scr[slot, h] * acc_scr[h] + _dot(vt_ref[row[h], h % MOBA_HEADS, j], p_scr[slot, h])

    def step(slot, j, j_next, j_prev, table, live):
        for h in heads:
            pv(h, j_prev, 1 - slot)
            if j_next is not None:
                qk(h, j_next, 1 - slot)
        for h in heads:
            sh = s_scr[slot, h]
            if table is not None:
                sh = sh + bias_ref[h % MOBA_HEADS, table]
            top = jnp.max(sh, axis=0, keepdims=True)
            m_old = m_scr[h]
            if live is not None:
                off = jnp.where(live, off_scr[h, pl.ds(j, 1), :], NEG)
                m_new = jnp.maximum(m_old, top + off)
                m_sub = jnp.where(off < 0.0, -NEG, m_new)
            else:
                m_new = jnp.maximum(m_old, top)
                m_sub = m_new
            m_scr[h] = m_new
            al_scr[slot, h] = jnp.exp2(m_old - m_new)
            p_scr[slot, h] = jnp.exp2(sh - m_sub).astype(BF16)

    n_far = jnp.maximum(n - 1, 0)
    last_far = jnp.maximum(n_far - 1, 0)
    j_before = jnp.maximum(n - 1, 0)
    far_blk = lambda u: jnp.clip(u, 0, last_far)

    for h in heads:
        qk(h, jnp.where(n_far > 0, 0, j_before), 0)

    def far_steps(u, count):
        for c in range(count):
            nxt = far_blk(u + c + 1)
            if c == count - 1:
                nxt = jnp.where(u + count < n_far, nxt, j_before)
            step(c % 2, far_blk(u + c), nxt, far_blk(u + c - 1), None, u + c < n_far)

    done = 0
    for size in MOBA_BODIES:
        left = n_far - done
        trips = (left + 1) // 2 if size == 2 else left // size

        def body(i, carry, size=size, done=done):
            far_steps(done + size * i, size)
            return carry

        lax.fori_loop(0, trips, body, 0)
        done = done + trips * size
    step(0, j_before, n, last_far, _TBL_PREV, n >= 1)
    step(1, n, None, j_before, _TBL_OWN, None)
    for h in heads:
        pv(h, n, 1)
        acc = acc_scr[h]
        o_ref[row[h], :, hsl[h]] = (acc[:MOBA_DH] / acc[MOBA_DH:MOBA_DH + 1]).T.astype(BF16)


def _moba(mqt, mk, mvt, kmean, bias, bsz, seq):
    t, w = mk.shape
    blk = MOBA_BLOCK
    nb = seq // blk
    g = MOBA_ROWS if bsz % MOBA_ROWS == 0 else 1
    units = g * MOBA_HEADS
    resident = lambda shape: pl.BlockSpec(shape, lambda b, n: (b,) + (0,) * (len(shape) - 1),
                                          pipeline_mode=pl.Buffered(1))
    out = pl.pallas_call(
        _moba_kernel,
        grid=(bsz // g, nb),
        in_specs=[pl.BlockSpec((g, MOBA_HEADS, 1, MOBA_DH, blk), lambda b, n: (b, 0, n, 0, 0)),
                  resident((g, seq, w)),
                  resident((g, MOBA_HEADS, nb, MOBA_VROWS, blk)),
                  pl.BlockSpec((g, nb, w), lambda b, n: (b, 0, 0)),
                  _const_spec(bias.shape)],
        out_specs=pl.BlockSpec((g, blk, w), lambda b, n: (b, n, 0)),
        out_shape=jax.ShapeDtypeStruct((bsz, seq, w), BF16),
        scratch_shapes=[pltpu.VMEM((units, nb, blk), F32),
                        pltpu.VMEM((units, 1, blk), F32),
                        pltpu.VMEM((2, units, 1, blk), F32),
                        pltpu.VMEM((units, MOBA_VROWS, blk), F32),
                        pltpu.VMEM((2, units, blk, blk), BF16),
                        pltpu.VMEM((2, units, blk, blk), F32)],
        compiler_params=_params(("arbitrary", "arbitrary")),
        name="moba",
    )(mqt, mk.reshape(bsz, seq, w), mvt, kmean, bias)
    return out.reshape(t, w)


def _merge_kernel(h_ref, mod_ref, nw_ref, oa_ref, ob_ref, wga_ref, wgb_ref, wa_ref, wb_ref, wo_ref,
                  o_ref, u_scr):
    mod = mod_ref[0, 0]
    u_scr[...] = _norm_mod(h_ref[...], nw_ref[...], mod).astype(BF16)
    ya = _dot(oa_ref[...], wa_ref[...])
    yb = _dot(ob_ref[...], wb_ref[...])
    merged = _sigmoid(_dot(u_scr[...], wga_ref[...])) * ya + _sigmoid(_dot(u_scr[...], wgb_ref[...])) * yb
    o_ref[...] = h_ref[...] + mod[2:3, :] * _dot(merged.astype(BF16), wo_ref[...])


def _merge(h, mod4, norm_w, oa, ob, wga, wgb, wa, wb, wo, tiles_per_batch):
    t, d = h.shape
    tm = TOKEN_TILE
    row = lambda i: (i, 0)
    return pl.pallas_call(
        _merge_kernel,
        grid=(t // tm,),
        in_specs=[pl.BlockSpec((tm, d), row),
                  pl.BlockSpec((1, 1, 3, d), lambda i: (i // tiles_per_batch, 1, 0, 0)),
                  _const_spec((1, d)),
                  pl.BlockSpec((tm, oa.shape[1]), row), pl.BlockSpec((tm, ob.shape[1]), row),
                  _const_spec(wga.shape), _const_spec(wgb.shape), _const_spec(wa.shape),
                  _const_spec(wb.shape), _const_spec(wo.shape)],
        out_specs=pl.BlockSpec((tm, d), row),
        out_shape=jax.ShapeDtypeStruct((t, d), F32),
        scratch_shapes=[pltpu.VMEM((tm, d), BF16)],
        compiler_params=_params(("arbitrary",)),
        name="merge",
    )(h, mod4, norm_w, oa, ob, wga, wgb, wa, wb, wo)


def kernel(x, c, w_ada, b_ada, norm_ff1, w_ff1_gate, w_ff1_up, w_ff1_down, norm_mix, w_in, w_gla_lr,
           b_gla_lr, gla_norm, rel_bias, w_br_gla, w_br_moba, w_out, norm_ff2, w_ff2_gate, w_ff2_up,
           w_ff2_down, norm_final):
    bsz, seq, d = x.shape
    depth = w_ada.shape[0]
    t = bsz * seq
    tiles_per_batch = seq // TOKEN_TILE
    qk_w = GLA_HEADS * (d // 16)
    v_w = GLA_HEADS * (d // 8)
    m_w = MOBA_HEADS * MOBA_DH
    offs = np.cumsum([0, qk_w, qk_w, v_w, GLA_LOWRANK, v_w, m_w, m_w, m_w, d, d])
    bias = _bias_tables(rel_bias)
    h = x.reshape(t, d)
    c_act_in = c
    for l in range(depth):
        mod4 = _adaln(c_act_in, w_ada, b_ada, l).reshape(bsz, 3, 3, d)
        cast = lambda w: w.astype(BF16)
        h = _ffn(h, mod4, 0, norm_ff1[l].reshape(1, d), w_ff1_gate[l], w_ff1_up[l], w_ff1_down[l], seq)
        wi = w_in[l]
        seg = lambda i: wi[:, offs[i]:offs[i + 1]]
        lr_w = _P_END - _P_LR
        lr_pad = jnp.pad(seg(3), ((0, 0), (0, lr_w - GLA_LOWRANK)))
        wp = cast(jnp.concatenate([seg(0), seg(1), seg(2), seg(4), seg(6), lr_pad], axis=1))
        wqvt = cast(jnp.concatenate([seg(5), seg(7)], axis=1).T)
        wlr = cast(jnp.pad(w_gla_lr[l], ((0, lr_w - GLA_LOWRANK), (0, 0))))
        o_a, mqt, mk, mvt, kmean = _inproj(
            h, mod4, norm_mix[l].reshape(1, d), wp, wqvt, wlr, b_gla_lr[l].reshape(1, qk_w),
            gla_norm[l].reshape(1, -1), bsz, seq)
        o_b = _moba(mqt, mk, mvt, kmean.reshape(bsz, seq // MOBA_BLOCK, m_w), bias, bsz, seq)
        h = _merge(h, mod4, norm_mix[l].reshape(1, d), o_a, o_b, cast(seg(8)), cast(seg(9)),
                   cast(w_br_gla[l]), cast(w_br_moba[l]), cast(w_out[l]), tiles_per_batch)
        last = l == depth - 1
        h = _ffn(h, mod4, 2, norm_ff2[l].reshape(1, d), w_ff2_gate[l], w_ff2_up[l], w_ff2_down[l], seq,
                 norm_final.reshape(1, d) if last else None)
    return h.reshape(bsz, seq, d)
```

```python
import functools
import math

import numpy as np
import jax
import jax.numpy as jnp
from jax import lax
from jax.experimental import pallas as pl
from jax.experimental.pallas import tpu as pltpu

F32 = jnp.float32
BF16 = jnp.bfloat16

EPS = 1e-6
FFN_RES = 0.5
GLA_HEADS = 4
GLA_TAU = 16.0
GLA_LOWRANK = 16
GLA_CHUNK = 128
MOBA_HEADS = 4
MOBA_DH = 128
MOBA_BLOCK = 256
MOBA_TOPK = 3
MOBA_VROWS = MOBA_DH + 16
MOBA_ROWS = 2
MOBA_BODIES = (16, 8, 4, 2)
NUM_BUCKETS = 32
MAX_DISTANCE = 128
MAX_EXACT = NUM_BUCKETS // 2
NEG = -1e30
LOG2E = 1.4426950408889634
MOBA_QSCALE = LOG2E * MOBA_DH ** -0.5
TOKEN_TILE = 1024
FFN_TILE = 1024
INPROJ_TILE = 1024
FF_CHUNK = 256
V7X_VMEM_BYTES = 64 * 1024 * 1024
VMEM_LIMIT = V7X_VMEM_BYTES - 3 * 1024 * 1024


def _dot(a, b):
    return jnp.dot(a, b, preferred_element_type=F32)


def _dot_nt(a, b):
    return lax.dot_general(a, b, (((1,), (1,)), ((), ())), preferred_element_type=F32)


def _dot_tn(a, b):
    return lax.dot_general(a, b, (((0,), (0,)), ((), ())), preferred_element_type=F32)


def _sigmoid(x):
    return 1.0 / (1.0 + jnp.exp(-x))


def _rmsnorm(x, w):
    return x * lax.rsqrt(jnp.mean(x * x, axis=-1, keepdims=True) + EPS) * w


def _norm_mod(x, w, mod):
    return _rmsnorm(x, w) * (1.0 + mod[1:2, :]) + mod[0:1, :]


def _params(sem):
    return pltpu.CompilerParams(dimension_semantics=sem, vmem_limit_bytes=VMEM_LIMIT)


def _const_spec(shape):
    nd = len(shape)
    return pl.BlockSpec(shape, lambda *_: (0,) * nd, pipeline_mode=pl.Buffered(1))


def _adaln_kernel(c_ref, w_ref, b_ref, o_ref):
    c = c_ref[...]
    ca = (c * _sigmoid(c)).astype(BF16)
    o_ref[...] = _dot(ca, w_ref[...].astype(BF16)) + b_ref[...]


def _adaln(c, w_ada, b_ada, layer):
    bsz, d = c.shape
    depth, _, n = w_ada.shape
    return pl.pallas_call(
        _adaln_kernel,
        grid=(n // d,),
        in_specs=[pl.BlockSpec((bsz, d), lambda j: (0, 0)),
                  pl.BlockSpec((None, d, d), lambda j: (layer, 0, j)),
                  pl.BlockSpec((None, 1, d), lambda j: (layer, 0, j))],
        out_specs=pl.BlockSpec((bsz, d), lambda j: (0, j)),
        out_shape=jax.ShapeDtypeStruct((bsz, n), F32),
        compiler_params=_params(("arbitrary",)),
        name="adaln",
    )(c, w_ada, b_ada.reshape(depth, 1, n))


def _ffn_kernel(x_ref, mod_ref, nw_ref, wg_hbm, wu_hbm, wd_hbm, *rest, final_norm):
    if final_norm:
        nf_ref, o_ref, u_scr, a_scr, wg_ref, wu_ref, wd_ref, sem = rest
    else:
        o_ref, u_scr, a_scr, wg_ref, wu_ref, wd_ref, sem = rest
    d_ff = wg_ref.shape[1]
    n_chunks = d_ff // FF_CHUNK
    chunk_cols = lambda c: slice(c * FF_CHUNK, (c + 1) * FF_CHUNK)

    def chunk_copies(c):
        sl = chunk_cols(c)
        return (pltpu.make_async_copy(wg_hbm.at[:, sl], wg_ref.at[:, sl], sem.at[0, c]),
                pltpu.make_async_copy(wu_hbm.at[:, sl], wu_ref.at[:, sl], sem.at[1, c]))

    down_copy = pltpu.make_async_copy(wd_hbm, wd_ref, sem.at[2, 0])

    def tile(wait):
        mod = mod_ref[0, 0]
        u_scr[...] = _norm_mod(x_ref[...], nw_ref[...], mod).astype(BF16)
        for c in range(n_chunks):
            sl = chunk_cols(c)
            if wait:
                for cp in chunk_copies(c):
                    cp.wait()
            g = _dot(u_scr[...], wg_ref[:, sl].astype(BF16))
            up = _dot(u_scr[...], wu_ref[:, sl].astype(BF16))
            a_scr[:, sl] = (g * _sigmoid(g) * up).astype(BF16)
        if wait:
            down_copy.wait()
        y = _dot(a_scr[...], wd_ref[...].astype(BF16))
        out = x_ref[...] + (FFN_RES * mod[2:3, :]) * y
        if final_norm:
            out = _rmsnorm(out, nf_ref[...])
        o_ref[...] = out

    first = pl.program_id(0) == 0

    @pl.when(first)
    def _():
        for c in range(n_chunks):
            for cp in chunk_copies(c):
                cp.start()
        down_copy.start()
        tile(True)

    @pl.when(jnp.logical_not(first))
    def _():
        tile(False)


def _ffn(h, mod4, layer, norm_w, wg, wu, wd, seq, norm_final=None):
    t, d = h.shape
    d_ff = wg.shape[1]
    tm = FFN_TILE
    tiles_per_batch = seq // tm
    in_hbm = pl.BlockSpec(memory_space=pl.ANY)
    in_specs = [pl.BlockSpec((tm, d), lambda i: (i, 0)),
                pl.BlockSpec((1, 1, 3, d), lambda i: (i // tiles_per_batch, layer, 0, 0)),
                _const_spec((1, d)), in_hbm, in_hbm, in_hbm]
    args = [h, mod4, norm_w, wg, wu, wd]
    if norm_final is not None:
        in_specs.append(_const_spec((1, d)))
        args.append(norm_final)
    return pl.pallas_call(
        functools.partial(_ffn_kernel, final_norm=norm_final is not None),
        grid=(t // tm,),
        in_specs=in_specs,
        out_specs=pl.BlockSpec((tm, d), lambda i: (i, 0)),
        out_shape=jax.ShapeDtypeStruct((t, d), F32),
        scratch_shapes=[pltpu.VMEM((tm, d), BF16), pltpu.VMEM((tm, d_ff), BF16),
                        pltpu.VMEM((d, d_ff), F32), pltpu.VMEM((d, d_ff), F32), pltpu.VMEM((d_ff, d), F32),
                        pltpu.SemaphoreType.DMA((3, d_ff // FF_CHUNK))],
        compiler_params=_params(("arbitrary",)),
        name="ffn_final" if norm_final is not None else "ffn",
    )(*args)


_P_GQ, _P_GK, _P_GV, _P_GOG, _P_MK, _P_LR, _P_END = 0, 256, 512, 1024, 1536, 2048, 2176


def _inproj_kernel(h_ref, mod_ref, nw_ref, wp_ref, wqvt_ref, wlr_ref, blr_ref, gnw_ref,
                   oa_ref, mqt_ref, mk_ref, mvt_ref, kmean_ref,
                   u_scr, gq_scr, gk_scr, gv_scr, gog_scr, dec_scr, st_scr, tri_scr, lev_scr, b_scr, w_scr,
                   *, tiles_per_batch):
    i = pl.program_id(0)

    @pl.when(i == 0)
    def _():
        _gla_tables(tri_scr, lev_scr)

    @pl.when(lax.rem(i, jnp.int32(tiles_per_batch)) == 0)
    def _():
        st_scr[...] = jnp.zeros_like(st_scr)

    u_scr[...] = _norm_mod(h_ref[...], nw_ref[...], mod_ref[0, 0]).astype(BF16)

    def proj(lo, hi):
        return _dot(u_scr[...], wp_ref[:, lo:hi])

    glr = proj(_P_LR, _P_END).astype(BF16)
    z = _dot(glr, wlr_ref[...]) + blr_ref[...]
    g = (jnp.minimum(z, 0.0) - jnp.log1p(jnp.exp(-jnp.abs(z)))) * (LOG2E / GLA_TAU)
    g_hi = g.astype(BF16)
    dec_scr[:, :g.shape[1]] = g_hi
    dec_scr[:, g.shape[1]:] = (g - g_hi.astype(F32)).astype(BF16)
    cumsum, prepare, chunk = _gla_chunk_fns(gq_scr, gk_scr, gv_scr, gog_scr, dec_scr, gnw_ref, oa_ref,
                                            st_scr, tri_scr, lev_scr, b_scr, w_scr)
    n_chunks = h_ref.shape[0] // GLA_CHUNK
    for c in range(n_chunks):
        cumsum(c * GLA_CHUNK, c)
        prepare(c)

    gq_scr[...] = proj(_P_GQ, _P_GK) * ((_P_GK - _P_GQ) // GLA_HEADS) ** -0.5
    gk_scr[...] = proj(_P_GK, _P_GV)
    gv_scr[...] = proj(_P_GV, _P_GOG).astype(BF16)
    gate = proj(_P_GOG, _P_MK)
    gog_scr[...] = gate * _sigmoid(gate)
    mk = proj(_P_MK, _P_LR)
    mk_ref[...] = mk.astype(BF16)
    kmean_ref[0] = jnp.mean(mk.reshape(-1, MOBA_BLOCK, mk.shape[1]), axis=1)
    mqvt = _dot_nt(wqvt_ref[...], u_scr[...])
    m_w = MOBA_HEADS * MOBA_DH
    ones = jnp.ones((MOBA_VROWS - MOBA_DH, MOBA_BLOCK), BF16)
    for h in range(MOBA_HEADS):
        for j in range(mqvt.shape[1] // MOBA_BLOCK):
            cols = slice(j * MOBA_BLOCK, (j + 1) * MOBA_BLOCK)
            mqt_ref[0, h, j] = (mqvt[h * MOBA_DH:(h + 1) * MOBA_DH, cols] * MOBA_QSCALE).astype(BF16)
            mvt_ref[0, h, j, :MOBA_DH, :] = mqvt[m_w + h * MOBA_DH:m_w + (h + 1) * MOBA_DH, cols].astype(BF16)
            mvt_ref[0, h, j, MOBA_DH:, :] = ones

    for c in range(n_chunks):
        chunk(c * GLA_CHUNK, c)


def _inproj(h, mod4, norm_w, wp, wvt, wlr, blr, gla_norm, bsz, seq):
    t, d = h.shape
    tm = INPROJ_TILE
    tiles_per_batch = seq // tm
    nblk = tm // MOBA_BLOCK
    nb = seq // MOBA_BLOCK
    qk_w, v_w, m_w = _P_GK - _P_GQ, _P_GOG - _P_GV, _P_LR - _P_MK
    row = lambda i: (i, 0)
    blocks = lambda i: (i // tiles_per_batch, 0, i % tiles_per_batch, 0, 0)
    out_shape = [jax.ShapeDtypeStruct((t, v_w), BF16),
                 jax.ShapeDtypeStruct((bsz, MOBA_HEADS, nb, MOBA_DH, MOBA_BLOCK), BF16),
                 jax.ShapeDtypeStruct((t, m_w), BF16),
                 jax.ShapeDtypeStruct((bsz, MOBA_HEADS, nb, MOBA_VROWS, MOBA_BLOCK), BF16),
                 jax.ShapeDtypeStruct((t // tm, nblk, m_w), F32)]
    out_specs = [pl.BlockSpec((tm, v_w), row),
                 pl.BlockSpec((1, MOBA_HEADS, nblk, MOBA_DH, MOBA_BLOCK), blocks),
                 pl.BlockSpec((tm, m_w), row),
                 pl.BlockSpec((1, MOBA_HEADS, nblk, MOBA_VROWS, MOBA_BLOCK), blocks),
                 pl.BlockSpec((1, nblk, m_w), lambda i: (i, 0, 0))]
    return pl.pallas_call(
        functools.partial(_inproj_kernel, tiles_per_batch=tiles_per_batch),
        grid=(t // tm,),
        in_specs=[pl.BlockSpec((tm, d), row),
                  pl.BlockSpec((1, 1, 3, d), lambda i: (i // tiles_per_batch, 1, 0, 0)),
                  _const_spec((1, d)), _const_spec(wp.shape), _const_spec(wvt.shape),
                  _const_spec(wlr.shape), _const_spec(blr.shape), _const_spec(gla_norm.shape)],
        out_specs=out_specs,
        out_shape=out_shape,
        scratch_shapes=[pltpu.VMEM((tm, d), BF16),
                        pltpu.VMEM((tm, qk_w), F32), pltpu.VMEM((tm, qk_w), F32),
                        pltpu.VMEM((tm, v_w), BF16), pltpu.VMEM((tm, v_w), F32),
                        pltpu.VMEM((tm, 2 * qk_w), BF16),
                        pltpu.VMEM((GLA_HEADS // 2, 2 * v_w // GLA_HEADS, 2 * qk_w // GLA_HEADS), F32),
                        pltpu.VMEM((GLA_CHUNK, GLA_CHUNK), BF16),
                        pltpu.VMEM((GLA_CHUNK, 2 * GLA_CHUNK), jnp.int32),
                        pltpu.VMEM((tm // GLA_CHUNK, GLA_CHUNK, qk_w), F32),
                        pltpu.VMEM((tm // GLA_CHUNK, len(_GLA_LEVELS) + 2, GLA_CHUNK, qk_w), F32)],
        compiler_params=_params(("arbitrary",)),
        name="inproj",
    )(h, mod4, norm_w, wp, wvt, wlr, blr, gla_norm)


_GLA_LEVELS = [1 << p for p in range(int(math.log2(GLA_CHUNK)))]


def _gla_tables(tri_scr, lev_scr):
    L = GLA_CHUNK
    tri_scr[...] = (lax.broadcasted_iota(jnp.int32, (L, L), 1)
                    <= lax.broadcasted_iota(jnp.int32, (L, L), 0)).astype(BF16)
    ti = lax.broadcasted_iota(jnp.int32, (L, 2 * L), 0)
    tj = lax.broadcasted_iota(jnp.int32, (L, 2 * L), 1) & (L - 1)
    xor = ti ^ tj
    lev = jnp.where(tj > ti, -1, 0)
    for p in range(len(_GLA_LEVELS)):
        lev = jnp.where((xor >= (1 << p)) & (tj < ti), p + 1, lev)
    lev_scr[...] = lev


def _gla_chunk_fns(q_ref, k_ref, v_ref, og_ref, g_ref, nw_ref, o_ref, st_scr, tri_scr, lev_scr, b_scr, w_scr):
    L = GLA_CHUNK
    hq = q_ref.shape[1] // GLA_HEADS
    hv = v_ref.shape[1] // GLA_HEADS
    wq = q_ref.shape[1]
    pairs = GLA_HEADS // 2
    assert 2 * hq == 128 and hv == 128 and L == 128
    levels = _GLA_LEVELS

    sub = lax.broadcasted_iota(jnp.int32, (L, wq), 0) & 7
    even_head = lax.broadcasted_iota(jnp.int32, (L, 2 * hq), 1) < hq
    stbd = ((lax.broadcasted_iota(jnp.int32, (2 * hv, 2 * hq), 0) < hv)
            == (lax.broadcasted_iota(jnp.int32, (2 * hv, 2 * hq), 1) < hq))

    def rows_bcast(slot, first, period):
        return jnp.concatenate([jnp.broadcast_to(b_scr[slot, r:r + 1, :], (period, wq))
                                for r in range(first, L, period)], axis=0)

    def midpoint(slot, s):
        if s >= 4:
            return rows_bcast(slot, s - 1, 2 * s)
        if s == 2:
            return jnp.where(sub < 4, rows_bcast(slot, 1, 8), rows_bcast(slot, 5, 8))
        return jnp.where(sub < 2, rows_bcast(slot, 0, 8),
                         jnp.where(sub < 4, rows_bcast(slot, 2, 8),
                                   jnp.where(sub < 6, rows_bcast(slot, 4, 8), rows_bcast(slot, 6, 8))))

    def cumsum(r0, slot):
        bb = _dot(tri_scr[...], g_ref[pl.ds(r0, L), :])
        b_scr[slot] = bb[:, :wq] + bb[:, wq:]

    def pair_scores(qx, kx, p):
        pl_ = slice(p * 2 * hq, (p + 1) * 2 * hq)
        kp = kx[:, pl_]
        kk = jnp.concatenate([jnp.where(even_head, kp, 0.0), jnp.where(even_head, 0.0, kp)], axis=0)
        return _dot_nt(qx[:, pl_].astype(BF16), kk.astype(BF16))

    def prepare(slot):
        b = b_scr[slot]
        for li, s in enumerate(levels):
            w_scr[slot, li] = jnp.exp2(-jnp.abs(b - midpoint(slot, s)))
        w_scr[slot, len(levels)] = jnp.exp2(b)
        w_scr[slot, len(levels) + 1] = jnp.exp2(b[L - 1:L, :] - b)

    def chunk(r0, slot):
        q = q_ref[pl.ds(r0, L), :]
        k = k_ref[pl.ds(r0, L), :]
        v = v_ref[pl.ds(r0, L), :]
        q_in = (q * w_scr[slot, len(levels)]).astype(BF16)
        k_out = (k * w_scr[slot, len(levels) + 1]).astype(BF16)
        dec = jnp.exp2(b_scr[slot, L - 1:L, :])

        att = [jnp.where(lev_scr[...] == 0, pair_scores(q, k, p), 0.0) for p in range(pairs)]
        for li in range(len(levels)):
            w = w_scr[slot, li]
            hit = lev_scr[...] == li + 1
            att = [jnp.where(hit, pair_scores(q * w, k * w, p), att[p]) for p in range(pairs)]

        for p in range(pairs):
            ql = slice(p * 2 * hq, (p + 1) * 2 * hq)
            vl = slice(p * 2 * hv, (p + 1) * 2 * hv)
            st = st_scr[p]
            vp = v[:, vl]
            zero = jnp.zeros((L, hv), BF16)
            v_diag = jnp.concatenate([jnp.concatenate([vp[:, :hv], zero], axis=1),
                                      jnp.concatenate([zero, vp[:, hv:]], axis=1)], axis=0)
            o = _dot_nt(q_in[:, ql], st.astype(BF16)) + _dot(att[p].astype(BF16), v_diag)
            st_scr[p] = st * dec[:, ql] + jnp.where(stbd, _dot_tn(vp, k_out[:, ql]), 0.0)
            for hh in range(2):
                sl = slice(vl.start + hh * hv, vl.start + (hh + 1) * hv)
                oh = _rmsnorm(o[:, hh * hv:(hh + 1) * hv], nw_ref[...])
                o_ref[pl.ds(r0, L), sl] = (oh * og_ref[pl.ds(r0, L), sl]).astype(BF16)

    return cumsum, prepare, chunk


def _rel_buckets(max_dist):
    n = np.arange(max_dist)
    nf = np.maximum(n, 1).astype(np.float64)
    large = MAX_EXACT + (np.log(nf / MAX_EXACT) / math.log(MAX_DISTANCE / MAX_EXACT)
                         * (NUM_BUCKETS - MAX_EXACT)).astype(np.int64)
    return np.where(n < MAX_EXACT, n, np.minimum(large, NUM_BUCKETS - 1))


_TBL_OWN, _TBL_PREV, _TBL_FAR = 0, 1, 2


def _bias_kernel(rb_ref, o_ref):
    h = pl.program_id(0)
    blk = MOBA_BLOCK
    buckets = _rel_buckets(2 * blk)
    assert (np.diff(buckets) >= 0).all()
    starts = {b: int(np.argmax(buckets == b)) for b in range(NUM_BUCKETS) if (buckets == b).any()}
    ki = lax.broadcasted_iota(jnp.int32, (blk, blk), 0)
    qi = lax.broadcasted_iota(jnp.int32, (blk, blk), 1)
    far = rb_ref[h, NUM_BUCKETS - 1]
    for tbl, base in ((_TBL_OWN, 0), (_TBL_PREV, blk)):
        dist = qi - ki + base
        val = jnp.full((blk, blk), rb_ref[h, 0], F32)
        for b in sorted(starts):
            if b > 0:
                val = jnp.where(dist >= starts[b], rb_ref[h, b], val)
        val = (val - far) * LOG2E
        if tbl == _TBL_OWN:
            val = jnp.where(dist >= 0, val, NEG)
        o_ref[0, tbl] = val
    o_ref[0, _TBL_FAR] = jnp.zeros((blk, blk), F32)


def _bias_tables(rel_bias):
    nh = rel_bias.shape[0]
    return pl.pallas_call(
        _bias_kernel,
        grid=(nh,),
        in_specs=[pl.BlockSpec(memory_space=pltpu.SMEM)],
        out_specs=pl.BlockSpec((1, 3, MOBA_BLOCK, MOBA_BLOCK), lambda h: (h, 0, 0, 0)),
        out_shape=jax.ShapeDtypeStruct((nh, 3, MOBA_BLOCK, MOBA_BLOCK), F32),
        compiler_params=_params(("arbitrary",)),
        name="moba_bias",
    )(rel_bias)


def _moba_kernel(qt_ref, k_ref, vt_ref, km_ref, bias_ref, o_ref,
                 off_scr, m_scr, al_scr, acc_scr, p_scr, s_scr):
    n = pl.program_id(1)
    blk = MOBA_BLOCK
    nb = km_ref.shape[1]
    heads = range(qt_ref.shape[0] * MOBA_HEADS)
    row = [h // MOBA_HEADS for h in heads]
    hsl = [slice((h % MOBA_HEADS) * MOBA_DH, (h % MOBA_HEADS + 1) * MOBA_DH) for h in heads]
    qt = lambda h: qt_ref[row[h], h % MOBA_HEADS, 0]

    brow = lax.broadcasted_iota(jnp.int32, (nb, len(heads) * blk), 0).astype(F32)
    gate = jnp.concatenate([_dot(km_ref[row[h], :, hsl[h]].astype(BF16), qt(h)) for h in heads],
                           axis=1)
    gate = jnp.where(brow < n.astype(F32), gate, NEG)
    off = jnp.full(gate.shape, NEG, F32)
    for _ in range(MOBA_TOPK):
        best = jnp.max(gate, axis=0, keepdims=True)
        first = jnp.min(jnp.where(gate == best, brow, float(nb)), axis=0, keepdims=True)
        pick = (brow == first) & (best > NEG)
        off = jnp.where(pick, 0.0, off)
        gate = jnp.where(pick, NEG, gate)
    for h in heads:
        off_scr[h] = off[:, h * blk:(h + 1) * blk]
        m_scr[h] = jnp.full((1, blk), NEG, F32)
        acc_scr[h] = jnp.zeros((MOBA_VROWS, blk), F32)
        al_scr[1, h] = jnp.ones((1, blk), F32)
        p_scr[1, h] = jnp.zeros((blk, blk), BF16)

    def qk(h, j, slot):
        r0 = pl.multiple_of(j * blk, blk)
        s_scr[slot, h] = _dot(k_ref[row[h], pl.ds(r0, blk), hsl[h]], qt(h))

    def pv(h, j, slot):
        acc_scr[h] = al_scr[slot, h] * acc_scr[h] + _dot(vt_ref[row[h], h % MOBA_HEADS, j], p_scr[slot, h])

    def step(slot, j, j_next, j_prev, table, live):
        for h in heads:
            pv(h, j_prev, 1 - slot)
            if j_next is not None:
                qk(h, j_next, 1 - slot)
        for h in heads:
            sh = s_scr[slot, h]
            if table is not None:
                sh = sh + bias_ref[h % MOBA_HEADS, table]
            top = jnp.max(sh, axis=0, keepdims=True)
            m_old = m_scr[h]
            if live is not None:
                off = jnp.where(live, off_scr[h, pl.ds(j, 1), :], NEG)
                m_new = jnp.maximum(m_old, top + off)
                m_sub = jnp.where(off < 0.0, -NEG, m_new)
            else:
                m_new = jnp.maximum(m_old, top)
                m_sub = m_new
            m_scr[h] = m_new
            al_scr[slot, h] = jnp.exp2(m_old - m_new)
            p_scr[slot, h] = jnp.exp2(sh - m_sub).astype(BF16)

    n_far = jnp.maximum(n - 1, 0)
    last_far = jnp.maximum(n_far - 1, 0)
    j_before = jnp.maximum(n - 1, 0)
    far_blk = lambda u: jnp.clip(u, 0, last_far)

    for h in heads:
        qk(h, jnp.where(n_far > 0, 0, j_before), 0)

    def far_steps(u, count):
        for c in range(count):
            nxt = far_blk(u + c + 1)
            if c == count - 1:
                nxt = jnp.where(u + count < n_far, nxt, j_before)
            step(c % 2, far_blk(u + c), nxt, far_blk(u + c - 1), None, u + c < n_far)

    done = 0
    for size in MOBA_BODIES:
        left = n_far - done
        trips = (left + 1) // 2 if size == 2 else left // size

        def body(i, carry, size=size, done=done):
            far_steps(done + size * i, size)
            return carry

        lax.fori_loop(0, trips, body, 0)
        done = done + trips * size
    step(0, j_before, n, last_far, _TBL_PREV, n >= 1)
    step(1, n, None, j_before, _TBL_OWN, None)
    for h in heads:
        pv(h, n, 1)
        acc = acc_scr[h]
        o_ref[row[h], :, hsl[h]] = (acc[:MOBA_DH] / acc[MOBA_DH:MOBA_DH + 1]).T.astype(BF16)


def _moba(mqt, mk, mvt, kmean, bias, bsz, seq):
    t, w = mk.shape
    blk = MOBA_BLOCK
    nb = seq // blk
    g = MOBA_ROWS if bsz % MOBA_ROWS == 0 else 1
    units = g * MOBA_HEADS
    resident = lambda shape: pl.BlockSpec(shape, lambda b, n: (b,) + (0,) * (len(shape) - 1),
                                          pipeline_mode=pl.Buffered(1))
    out = pl.pallas_call(
        _moba_kernel,
        grid=(bsz // g, nb),
        in_specs=[pl.BlockSpec((g, MOBA_HEADS, 1, MOBA_DH, blk), lambda b, n: (b, 0, n, 0, 0)),
                  resident((g, seq, w)),
                  resident((g, MOBA_HEADS, nb, MOBA_VROWS, blk)),
                  pl.BlockSpec((g, nb, w), lambda b, n: (b, 0, 0)),
                  _const_spec(bias.shape)],
        out_specs=pl.BlockSpec((g, blk, w), lambda b, n: (b, n, 0)),
        out_shape=jax.ShapeDtypeStruct((bsz, seq, w), BF16),
        scratch_shapes=[pltpu.VMEM((units, nb, blk), F32),
                        pltpu.VMEM((units, 1, blk), F32),
                        pltpu.VMEM((2, units, 1, blk), F32),
                        pltpu.VMEM((units, MOBA_VROWS, blk), F32),
                        pltpu.VMEM((2, units, blk, blk), BF16),
                        pltpu.VMEM((2, units, blk, blk), F32)],
        compiler_params=_params(("arbitrary", "arbitrary")),
        name="moba",
    )(mqt, mk.reshape(bsz, seq, w), mvt, kmean, bias)
    return out.reshape(t, w)


def _merge_kernel(h_ref, mod_ref, nw_ref, oa_ref, ob_ref, wga_ref, wgb_ref, wa_ref, wb_ref, wo_ref,
                  o_ref, u_scr):
    mod = mod_ref[0, 0]
    u_scr[...] = _norm_mod(h_ref[...], nw_ref[...], mod).astype(BF16)
    ya = _dot(oa_ref[...], wa_ref[...])
    yb = _dot(ob_ref[...], wb_ref[...])
    merged = _sigmoid(_dot(u_scr[...], wga_ref[...])) * ya + _sigmoid(_dot(u_scr[...], wgb_ref[...])) * yb
    o_ref[...] = h_ref[...] + mod[2:3, :] * _dot(merged.astype(BF16), wo_ref[...])


def _merge(h, mod4, norm_w, oa, ob, wga, wgb, wa, wb, wo, tiles_per_batch):
    t, d = h.shape
    tm = TOKEN_TILE
    row = lambda i: (i, 0)
    return pl.pallas_call(
        _merge_kernel,
        grid=(t // tm,),
        in_specs=[pl.BlockSpec((tm, d), row),
                  pl.BlockSpec((1, 1, 3, d), lambda i: (i // tiles_per_batch, 1, 0, 0)),
                  _const_spec((1, d)),
                  pl.BlockSpec((tm, oa.shape[1]), row), pl.BlockSpec((tm, ob.shape[1]), row),
                  _const_spec(wga.shape), _const_spec(wgb.shape), _const_spec(wa.shape),
                  _const_spec(wb.shape), _const_spec(wo.shape)],
        out_specs=pl.BlockSpec((tm, d), row),
        out_shape=jax.ShapeDtypeStruct((t, d), F32),
        scratch_shapes=[pltpu.VMEM((tm, d), BF16)],
        compiler_params=_params(("arbitrary",)),
        name="merge",
    )(h, mod4, norm_w, oa, ob, wga, wgb, wa, wb, wo)


def kernel(x, c, w_ada, b_ada, norm_ff1, w_ff1_gate, w_ff1_up, w_ff1_down, norm_mix, w_in, w_gla_lr,
           b_gla_lr, gla_norm, rel_bias, w_br_gla, w_br_moba, w_out, norm_ff2, w_ff2_gate, w_ff2_up,
           w_ff2_down, norm_final):
    bsz, seq, d = x.shape
    depth = w_ada.shape[0]
    t = bsz * seq
    tiles_per_batch = seq // TOKEN_TILE
    qk_w = GLA_HEADS * (d // 16)
    v_w = GLA_HEADS * (d // 8)
    m_w = MOBA_HEADS * MOBA_DH
    offs = np.cumsum([0, qk_w, qk_w, v_w, GLA_LOWRANK, v_w, m_w, m_w, m_w, d, d])
    bias = _bias_tables(rel_bias)
    h = x.reshape(t, d)
    c_act_in = c
    for l in range(depth):
        mod4 = _adaln(c_act_in, w_ada, b_ada, l).reshape(bsz, 3, 3, d)
        cast = lambda w: w.astype(BF16)
        h = _ffn(h, mod4, 0, norm_ff1[l].reshape(1, d), w_ff1_gate[l], w_ff1_up[l], w_ff1_down[l], seq)
        wi = w_in[l]
        seg = lambda i: wi[:, offs[i]:offs[i + 1]]
        lr_w = _P_END - _P_LR
        lr_pad = jnp.pad(seg(3), ((0, 0), (0, lr_w - GLA_LOWRANK)))
        wp = cast(jnp.concatenate([seg(0), seg(1), seg(2), seg(4), seg(6), lr_pad], axis=1))
        wqvt = cast(jnp.concatenate([seg(5), seg(7)], axis=1).T)
        wlr = cast(jnp.pad(w_gla_lr[l], ((0, lr_w - GLA_LOWRANK), (0, 0))))
        o_a, mqt, mk, mvt, kmean = _inproj(
            h, mod4, norm_mix[l].reshape(1, d), wp, wqvt, wlr, b_gla_lr[l].reshape(1, qk_w),
            gla_norm[l].reshape(1, -1), bsz, seq)
        o_b = _moba(mqt, mk, mvt, kmean.reshape(bsz, seq // MOBA_BLOCK, m_w), bias, bsz, seq)
        h = _merge(h, mod4, norm_mix[l].reshape(1, d), o_a, o_b, cast(seg(8)), cast(seg(9)),
                   cast(w_br_gla[l]), cast(w_br_moba[l]), cast(w_out[l]), tiles_per_batch)
        last = l == depth - 1
        h = _ffn(h, mod4, 2, norm_ff2[l].reshape(1, d), w_ff2_gate[l], w_ff2_up[l], w_ff2_down[l], seq,
                 norm_final.reshape(1, d) if last else None)
    return h.reshape(bsz, seq, d)
```

```python
import functools
import math

import numpy as np
import jax
import jax.numpy as jnp
from jax import lax
from jax.experimental import pallas as pl
from jax.experimental.pallas import tpu as pltpu

F32 = jnp.float32
BF16 = jnp.bfloat16

EPS = 1e-6
FFN_RES = 0.5
GLA_HEADS = 4
GLA_TAU = 16.0
GLA_LOWRANK = 16
GLA_CHUNK = 128
MOBA_HEADS = 4
MOBA_DH = 128
MOBA_BLOCK = 256
MOBA_TOPK = 3
MOBA_VROWS = MOBA_DH + 16
MOBA_ROWS = 2
MOBA_BODIES = (16, 8, 4, 2)
NUM_BUCKETS = 32
MAX_DISTANCE = 128
MAX_EXACT = NUM_BUCKETS // 2
NEG = -1e30
LOG2E = 1.4426950408889634
MOBA_QSCALE = LOG2E * MOBA_DH ** -0.5
TOKEN_TILE = 1024
FFN_TILE = 1024
INPROJ_TILE = 1024
FF_CHUNK = 256
V7X_VMEM_BYTES = 64 * 1024 * 1024
VMEM_LIMIT = V7X_VMEM_BYTES - 3 * 1024 * 1024


def _dot(a, b):
    return jnp.dot(a, b, preferred_element_type=F32)


def _dot_nt(a, b):
    return lax.dot_general(a, b, (((1,), (1,)), ((), ())), preferred_element_type=F32)


def _dot_tn(a, b):
    return lax.dot_general(a, b, (((0,), (0,)), ((), ())), preferred_element_type=F32)


def _sigmoid(x):
    return 1.0 / (1.0 + jnp.exp(-x))


def _rmsnorm(x, w):
    return x * lax.rsqrt(jnp.mean(x * x, axis=-1, keepdims=True) + EPS) * w


def _norm_mod(x, w, mod):
    return _rmsnorm(x, w) * (1.0 + mod[1:2, :]) + mod[0:1, :]


def _params(sem):
    return pltpu.CompilerParams(dimension_semantics=sem, vmem_limit_bytes=VMEM_LIMIT)


def _const_spec(shape):
    nd = len(shape)
    return pl.BlockSpec(shape, lambda *_: (0,) * nd, pipeline_mode=pl.Buffered(1))


def _adaln_kernel(c_ref, w_ref, b_ref, o_ref):
    c = c_ref[...]
    ca = (c * _sigmoid(c)).astype(BF16)
    o_ref[...] = _dot(ca, w_ref[...].astype(BF16)) + b_ref[...]


def _adaln(c, w_ada, b_ada, layer):
    bsz, d = c.shape
    depth, _, n = w_ada.shape
    return pl.pallas_call(
        _adaln_kernel,
        grid=(n // d,),
        in_specs=[pl.BlockSpec((bsz, d), lambda j: (0, 0)),
                  pl.BlockSpec((None, d, d), lambda j: (layer, 0, j)),
                  pl.BlockSpec((None, 1, d), lambda j: (layer, 0, j))],
        out_specs=pl.BlockSpec((bsz, d), lambda j: (0, j)),
        out_shape=jax.ShapeDtypeStruct((bsz, n), F32),
        compiler_params=_params(("arbitrary",)),
        name="adaln",
    )(c, w_ada, b_ada.reshape(depth, 1, n))


def _ffn_kernel(x_ref, mod_ref, nw_ref, wg_hbm, wu_hbm, wd_hbm, *rest, final_norm):
    if final_norm:
        nf_ref, o_ref, u_scr, a_scr, wg_ref, wu_ref, wd_ref, sem = rest
    else:
        o_ref, u_scr, a_scr, wg_ref, wu_ref, wd_ref, sem = rest
    d_ff = wg_ref.shape[1]
    n_chunks = d_ff // FF_CHUNK
    chunk_cols = lambda c: slice(c * FF_CHUNK, (c + 1) * FF_CHUNK)

    def chunk_copies(c):
        sl = chunk_cols(c)
        return (pltpu.make_async_copy(wg_hbm.at[:, sl], wg_ref.at[:, sl], sem.at[0, c]),
                pltpu.make_async_copy(wu_hbm.at[:, sl], wu_ref.at[:, sl], sem.at[1, c]))

    down_copy = pltpu.make_async_copy(wd_hbm, wd_ref, sem.at[2, 0])

    def tile(wait):
        mod = mod_ref[0, 0]
        u_scr[...] = _norm_mod(x_ref[...], nw_ref[...], mod).astype(BF16)
        for c in range(n_chunks):
            sl = chunk_cols(c)
            if wait:
                for cp in chunk_copies(c):
                    cp.wait()
            g = _dot(u_scr[...], wg_ref[:, sl].astype(BF16))
            up = _dot(u_scr[...], wu_ref[:, sl].astype(BF16))
            a_scr[:, sl] = (g * _sigmoid(g) * up).astype(BF16)
        if wait:
            down_copy.wait()
        y = _dot(a_scr[...], wd_ref[...].astype(BF16))
        out = x_ref[...] + (FFN_RES * mod[2:3, :]) * y
        if final_norm:
            out = _rmsnorm(out, nf_ref[...])
        o_ref[...] = out

    first = pl.program_id(0) == 0

    @pl.when(first)
    def _():
        for c in range(n_chunks):
            for cp in chunk_copies(c):
                cp.start()
        down_copy.start()
        tile(True)

    @pl.when(jnp.logical_not(first))
    def _():
        tile(False)


def _ffn(h, mod4, layer, norm_w, wg, wu, wd, seq, norm_final=None):
    t, d = h.shape
    d_ff = wg.shape[1]
    tm = FFN_TILE
    tiles_per_batch = seq // tm
    in_hbm = pl.BlockSpec(memory_space=pl.ANY)
    in_specs = [pl.BlockSpec((tm, d), lambda i: (i, 0)),
                pl.BlockSpec((1, 1, 3, d), lambda i: (i // tiles_per_batch, layer, 0, 0)),
                _const_spec((1, d)), in_hbm, in_hbm, in_hbm]
    args = [h, mod4, norm_w, wg, wu, wd]
    if norm_final is not None:
        in_specs.append(_const_spec((1, d)))
        args.append(norm_final)
    return pl.pallas_call(
        functools.partial(_ffn_kernel, final_norm=norm_final is not None),
        grid=(t // tm,),
        in_specs=in_specs,
        out_specs=pl.BlockSpec((tm, d), lambda i: (i, 0)),
        out_shape=jax.ShapeDtypeStruct((t, d), F32),
        scratch_shapes=[pltpu.VMEM((tm, d), BF16), pltpu.VMEM((tm, d_ff), BF16),
                        pltpu.VMEM((d, d_ff), F32), pltpu.VMEM((d, d_ff), F32), pltpu.VMEM((d_ff, d), F32),
                        pltpu.SemaphoreType.DMA((3, d_ff // FF_CHUNK))],
        compiler_params=_params(("arbitrary",)),
        name="ffn_final" if norm_final is not None else "ffn",
    )(*args)


_P_GQ, _P_GK, _P_GV, _P_GOG, _P_MK, _P_LR, _P_END = 0, 256, 512, 1024, 1536, 2048, 2176


def _inproj_kernel(h_ref, mod_ref, nw_ref, wp_ref, wqvt_ref, wlr_ref, blr_ref, gnw_ref,
                   oa_ref, mqt_ref, mk_ref, mvt_ref, kmean_ref,
                   u_scr, gq_scr, gk_scr, gv_scr, gog_scr, dec_scr, st_scr, tri_scr, lev_scr, b_scr, w_scr,
                   *, tiles_per_batch):
    i = pl.program_id(0)

    @pl.when(i == 0)
    def _():
        _gla_tables(tri_scr, lev_scr)

    @pl.when(lax.rem(i, jnp.int32(tiles_per_batch)) == 0)
    def _():
        st_scr[...] = jnp.zeros_like(st_scr)

    u_scr[...] = _norm_mod(h_ref[...], nw_ref[...], mod_ref[0, 0]).astype(BF16)

    def proj(lo, hi):
        return _dot(u_scr[...], wp_ref[:, lo:hi])

    glr = proj(_P_LR, _P_END).astype(BF16)
    z = _dot(glr, wlr_ref[...]) + blr_ref[...]
    g = (jnp.minimum(z, 0.0) - jnp.log1p(jnp.exp(-jnp.abs(z)))) * (LOG2E / GLA_TAU)
    g_hi = g.astype(BF16)
    dec_scr[:, :g.shape[1]] = g_hi
    dec_scr[:, g.shape[1]:] = (g - g_hi.astype(F32)).astype(BF16)
    cumsum, prepare, chunk = _gla_chunk_fns(gq_scr, gk_scr, gv_scr, gog_scr, dec_scr, gnw_ref, oa_ref,
                                            st_scr, tri_scr, lev_scr, b_scr, w_scr)
    n_chunks = h_ref.shape[0] // GLA_CHUNK
    for c in range(n_chunks):
        cumsum(c * GLA_CHUNK, c)
        prepare(c)

    gq_scr[...] = proj(_P_GQ, _P_GK) * ((_P_GK - _P_GQ) // GLA_HEADS) ** -0.5
    gk_scr[...] = proj(_P_GK, _P_GV)
    gv_scr[...] = proj(_P_GV, _P_GOG).astype(BF16)
    gate = proj(_P_GOG, _P_MK)
    gog_scr[...] = gate * _sigmoid(gate)
    mk = proj(_P_MK, _P_LR)
    mk_ref[...] = mk.astype(BF16)
    kmean_ref[0] = jnp.mean(mk.reshape(-1, MOBA_BLOCK, mk.shape[1]), axis=1)
    mqvt = _dot_nt(wqvt_ref[...], u_scr[...])
    m_w = MOBA_HEADS * MOBA_DH
    ones = jnp.ones((MOBA_VROWS - MOBA_DH, MOBA_BLOCK), BF16)
    for h in range(MOBA_HEADS):
        for j in range(mqvt.shape[1] // MOBA_BLOCK):
            cols = slice(j * MOBA_BLOCK, (j + 1) * MOBA_BLOCK)
            mqt_ref[0, h, j] = (mqvt[h * MOBA_DH:(h + 1) * MOBA_DH, cols] * MOBA_QSCALE).astype(BF16)
            mvt_ref[0, h, j, :MOBA_DH, :] = mqvt[m_w + h * MOBA_DH:m_w + (h + 1) * MOBA_DH, cols].astype(BF16)
            mvt_ref[0, h, j, MOBA_DH:, :] = ones

    for c in range(n_chunks):
        chunk(c * GLA_CHUNK, c)


def _inproj(h, mod4, norm_w, wp, wvt, wlr, blr, gla_norm, bsz, seq):
    t, d = h.shape
    tm = INPROJ_TILE
    tiles_per_batch = seq // tm
    nblk = tm // MOBA_BLOCK
    nb = seq // MOBA_BLOCK
    qk_w, v_w, m_w = _P_GK - _P_GQ, _P_GOG - _P_GV, _P_LR - _P_MK
    row = lambda i: (i, 0)
    blocks = lambda i: (i // tiles_per_batch, 0, i % tiles_per_batch, 0, 0)
    out_shape = [jax.ShapeDtypeStruct((t, v_w), BF16),
                 jax.ShapeDtypeStruct((bsz, MOBA_HEADS, nb, MOBA_DH, MOBA_BLOCK), BF16),
                 jax.ShapeDtypeStruct((t, m_w), BF16),
                 jax.ShapeDtypeStruct((bsz, MOBA_HEADS, nb, MOBA_VROWS, MOBA_BLOCK), BF16),
                 jax.ShapeDtypeStruct((t // tm, nblk, m_w), F32)]
    out_specs = [pl.BlockSpec((tm, v_w), row),
                 pl.BlockSpec((1, MOBA_HEADS, nblk, MOBA_DH, MOBA_BLOCK), blocks),
                 pl.BlockSpec((tm, m_w), row),
                 pl.BlockSpec((1, MOBA_HEADS, nblk, MOBA_VROWS, MOBA_BLOCK), blocks),
                 pl.BlockSpec((1, nblk, m_w), lambda i: (i, 0, 0))]
    return pl.pallas_call(
        functools.partial(_inproj_kernel, tiles_per_batch=tiles_per_batch),
        grid=(t // tm,),
        in_specs=[pl.BlockSpec((tm, d), row),
                  pl.BlockSpec((1, 1, 3, d), lambda i: (i // tiles_per_batch, 1, 0, 0)),
                  _const_spec((1, d)), _const_spec(wp.shape), _const_spec(wvt.shape),
                  _const_spec(wlr.shape), _const_spec(blr.shape), _const_spec(gla_norm.shape)],
        out_specs=out_specs,
        out_shape=out_shape,
        scratch_shapes=[pltpu.VMEM((tm, d), BF16),
                        pltpu.VMEM((tm, qk_w), F32), pltpu.VMEM((tm, qk_w), F32),
                        pltpu.VMEM((tm, v_w), BF16), pltpu.VMEM((tm, v_w), F32),
                        pltpu.VMEM((tm, 2 * qk_w), BF16),
                        pltpu.VMEM((GLA_HEADS // 2, 2 * v_w // GLA_HEADS, 2 * qk_w // GLA_HEADS), F32),
                        pltpu.VMEM((GLA_CHUNK, GLA_CHUNK), BF16),
                        pltpu.VMEM((GLA_CHUNK, 2 * GLA_CHUNK), jnp.int32),
                        pltpu.VMEM((tm // GLA_CHUNK, GLA_CHUNK, qk_w), F32),
                        pltpu.VMEM((tm // GLA_CHUNK, len(_GLA_LEVELS) + 2, GLA_CHUNK, qk_w), F32)],
        compiler_params=_params(("arbitrary",)),
        name="inproj",
    )(h, mod4, norm_w, wp, wvt, wlr, blr, gla_norm)


_GLA_LEVELS = [1 << p for p in range(int(math.log2(GLA_CHUNK)))]


def _gla_tables(tri_scr, lev_scr):
    L = GLA_CHUNK
    tri_scr[...] = (lax.broadcasted_iota(jnp.int32, (L, L), 1)
                    <= lax.broadcasted_iota(jnp.int32, (L, L), 0)).astype(BF16)
    ti = lax.broadcasted_iota(jnp.int32, (L, 2 * L), 0)
    tj = lax.broadcasted_iota(jnp.int32, (L, 2 * L), 1) & (L - 1)
    xor = ti ^ tj
    lev = jnp.where(tj > ti, -1, 0)
    for p in range(len(_GLA_LEVELS)):
        lev = jnp.where((xor >= (1 << p)) & (tj < ti), p + 1, lev)
    lev_scr[...] = lev


def _gla_chunk_fns(q_ref, k_ref, v_ref, og_ref, g_ref, nw_ref, o_ref, st_scr, tri_scr, lev_scr, b_scr, w_scr):
    L = GLA_CHUNK
    hq = q_ref.shape[1] // GLA_HEADS
    hv = v_ref.shape[1] // GLA_HEADS
    wq = q_ref.shape[1]
    pairs = GLA_HEADS // 2
    assert 2 * hq == 128 and hv == 128 and L == 128
    levels = _GLA_LEVELS

    sub = lax.broadcasted_iota(jnp.int32, (L, wq), 0) & 7
    even_head = lax.broadcasted_iota(jnp.int32, (L, 2 * hq), 1) < hq
    stbd = ((lax.broadcasted_iota(jnp.int32, (2 * hv, 2 * hq), 0) < hv)
            == (lax.broadcasted_iota(jnp.int32, (2 * hv, 2 * hq), 1) < hq))

    def rows_bcast(slot, first, period):
        return jnp.concatenate([jnp.broadcast_to(b_scr[slot, r:r + 1, :], (period, wq))
                                for r in range(first, L, period)], axis=0)

    def midpoint(slot, s):
        if s >= 4:
            return rows_bcast(slot, s - 1, 2 * s)
        if s == 2:
            return jnp.where(sub < 4, rows_bcast(slot, 1, 8), rows_bcast(slot, 5, 8))
        return jnp.where(sub < 2, rows_bcast(slot, 0, 8),
                         jnp.where(sub < 4, rows_bcast(slot, 2, 8),
                                   jnp.where(sub < 6, rows_bcast(slot, 4, 8), rows_bcast(slot, 6, 8))))

    def cumsum(r0, slot):
        bb = _dot(tri_scr[...], g_ref[pl.ds(r0, L), :])
        b_scr[slot] = bb[:, :wq] + bb[:, wq:]

    def pair_scores(qx, kx, p):
        pl_ = slice(p * 2 * hq, (p + 1) * 2 * hq)
        kp = kx[:, pl_]
        kk = jnp.concatenate([jnp.where(even_head, kp, 0.0), jnp.where(even_head, 0.0, kp)], axis=0)
        return _dot_nt(qx[:, pl_].astype(BF16), kk.astype(BF16))

    def prepare(slot):
        b = b_scr[slot]
        for li, s in enumerate(levels):
            w_scr[slot, li] = jnp.exp2(-jnp.abs(b - midpoint(slot, s)))
        w_scr[slot, len(levels)] = jnp.exp2(b)
        w_scr[slot, len(levels) + 1] = jnp.exp2(b[L - 1:L, :] - b)

    def chunk(r0, slot):
        q = q_ref[pl.ds(r0, L), :]
        k = k_ref[pl.ds(r0, L), :]
        v = v_ref[pl.ds(r0, L), :]
        q_in = (q * w_scr[slot, len(levels)]).astype(BF16)
        k_out = (k * w_scr[slot, len(levels) + 1]).astype(BF16)
        dec = jnp.exp2(b_scr[slot, L - 1:L, :])

        att = [jnp.where(lev_scr[...] == 0, pair_scores(q, k, p), 0.0) for p in range(pairs)]
        for li in range(len(levels)):
            w = w_scr[slot, li]
            hit = lev_scr[...] == li + 1
            att = [jnp.where(hit, pair_scores(q * w, k * w, p), att[p]) for p in range(pairs)]

        for p in range(pairs):
            ql = slice(p * 2 * hq, (p + 1) * 2 * hq)
            vl = slice(p * 2 * hv, (p + 1) * 2 * hv)
            st = st_scr[p]
            vp = v[:, vl]
            zero = jnp.zeros((L, hv), BF16)
            v_diag = jnp.concatenate([jnp.concatenate([vp[:, :hv], zero], axis=1),
                                      jnp.concatenate([zero, vp[:, hv:]], axis=1)], axis=0)
            o = _dot_nt(q_in[:, ql], st.astype(BF16)) + _dot(att[p].astype(BF16), v_diag)
            st_scr[p] = st * dec[:, ql] + jnp.where(stbd, _dot_tn(vp, k_out[:, ql]), 0.0)
            for hh in range(2):
                sl = slice(vl.start + hh * hv, vl.start + (hh + 1) * hv)
                oh = _rmsnorm(o[:, hh * hv:(hh + 1) * hv], nw_ref[...])
                o_ref[pl.ds(r0, L), sl] = (oh * og_ref[pl.ds(r0, L), sl]).astype(BF16)

    return cumsum, prepare, chunk


def _rel_buckets(max_dist):
    n = np.arange(max_dist)
    nf = np.maximum(n, 1).astype(np.float64)
    large = MAX_EXACT + (np.log(nf / MAX_EXACT) / math.log(MAX_DISTANCE / MAX_EXACT)
                         * (NUM_BUCKETS - MAX_EXACT)).astype(np.int64)
    return np.where(n < MAX_EXACT, n, np.minimum(large, NUM_BUCKETS - 1))


_TBL_OWN, _TBL_PREV, _TBL_FAR = 0, 1, 2


def _bias_kernel(rb_ref, o_ref):
    h = pl.program_id(0)
    blk = MOBA_BLOCK
    buckets = _rel_buckets(2 * blk)
    assert (np.diff(buckets) >= 0).all()
    starts = {b: int(np.argmax(buckets == b)) for b in range(NUM_BUCKETS) if (buckets == b).any()}
    ki = lax.broadcasted_iota(jnp.int32, (blk, blk), 0)
    qi = lax.broadcasted_iota(jnp.int32, (blk, blk), 1)
    far = rb_ref[h, NUM_BUCKETS - 1]
    for tbl, base in ((_TBL_OWN, 0), (_TBL_PREV, blk)):
        dist = qi - ki + base
        val = jnp.full((blk, blk), rb_ref[h, 0], F32)
        for b in sorted(starts):
            if b > 0:
                val = jnp.where(dist >= starts[b], rb_ref[h, b], val)
        val = (val - far) * LOG2E
        if tbl == _TBL_OWN:
            val = jnp.where(dist >= 0, val, NEG)
        o_ref[0, tbl] = val
    o_ref[0, _TBL_FAR] = jnp.zeros((blk, blk), F32)


def _bias_tables(rel_bias):
    nh = rel_bias.shape[0]
    return pl.pallas_call(
        _bias_kernel,
        grid=(nh,),
        in_specs=[pl.BlockSpec(memory_space=pltpu.SMEM)],
        out_specs=pl.BlockSpec((1, 3, MOBA_BLOCK, MOBA_BLOCK), lambda h: (h, 0, 0, 0)),
        out_shape=jax.ShapeDtypeStruct((nh, 3, MOBA_BLOCK, MOBA_BLOCK), F32),
        compiler_params=_params(("arbitrary",)),
        name="moba_bias",
    )(rel_bias)


def _moba_kernel(qt_ref, k_hbm, vt_hbm, km_ref, bias_ref, o_ref,
                 off_scr, m_scr, al_scr, acc_scr, p_scr, s_scr, k_ref, vt_ref, sem):
    n = pl.program_id(1)
    blk = MOBA_BLOCK
    nb = km_ref.shape[1]

    rows = pl.ds(pl.program_id(0) * qt_ref.shape[0], qt_ref.shape[0])

    def block_copies(j):
        keys = pl.ds(j * blk if isinstance(j, int) else pl.multiple_of(j * blk, blk), blk)
        return (pltpu.make_async_copy(k_hbm.at[rows, keys, :], k_ref.at[:, keys, :], sem.at[0, j]),
                pltpu.make_async_copy(vt_hbm.at[rows, :, pl.ds(j, 1)], vt_ref.at[:, :, pl.ds(j, 1)], sem.at[1, j]))

    @pl.when(n == 0)
    def _():
        for j in range(nb):
            for cp in block_copies(j):
                cp.start()

    for cp in block_copies(n):
        cp.wait()

    heads = range(qt_ref.shape[0] * MOBA_HEADS)
    row = [h // MOBA_HEADS for h in heads]
    hsl = [slice((h % MOBA_HEADS) * MOBA_DH, (h % MOBA_HEADS + 1) * MOBA_DH) for h in heads]
    qt = lambda h: qt_ref[row[h], h % MOBA_HEADS, 0]

    brow = lax.broadcasted_iota(jnp.int32, (nb, len(heads) * blk), 0).astype(F32)
    gate = jnp.concatenate([_dot(km_ref[row[h], :, hsl[h]].astype(BF16), qt(h)) for h in heads],
                           axis=1)
    gate = jnp.where(brow < n.astype(F32), gate, NEG)
    off = jnp.full(gate.shape, NEG, F32)
    for _ in range(MOBA_TOPK):
        best = jnp.max(gate, axis=0, keepdims=True)
        first = jnp.min(jnp.where(gate == best, brow, float(nb)), axis=0, keepdims=True)
        pick = (brow == first) & (best > NEG)
        off = jnp.where(pick, 0.0, off)
        gate = jnp.where(pick, NEG, gate)
    for h in heads:
        off_scr[h] = off[:, h * blk:(h + 1) * blk]
        m_scr[h] = jnp.full((1, blk), NEG, F32)
        acc_scr[h] = jnp.zeros((MOBA_VROWS, blk), F32)
        al_scr[1, h] = jnp.ones((1, blk), F32)
        p_scr[1, h] = jnp.zeros((blk, blk), BF16)

    def qk(h, j, slot):
        r0 = pl.multiple_of(j * blk, blk)
        s_scr[slot, h] = _dot(k_ref[row[h], pl.ds(r0, blk), hsl[h]], qt(h))

    def pv(h, j, slot):
        acc_scr[h] = al_scr[slot, h] * acc_scr[h] + _dot(vt_ref[row[h], h % MOBA_HEADS, j], p_scr[slot, h])

    def step(slot, j, j_next, j_prev, table, live):
        for h in heads:
            pv(h, j_prev, 1 - slot)
            if j_next is not None:
                qk(h, j_next, 1 - slot)
        for h in heads:
            sh = s_scr[slot, h]
            if table is not None:
                sh = sh + bias_ref[h % MOBA_HEADS, table]
            top = jnp.max(sh, axis=0, keepdims=True)
            m_old = m_scr[h]
            if live is not None:
                off = jnp.where(live, off_scr[h, pl.ds(j, 1), :], NEG)
                m_new = jnp.maximum(m_old, top + off)
                m_sub = jnp.where(off < 0.0, -NEG, m_new)
            else:
                m_new = jnp.maximum(m_old, top)
                m_sub = m_new
            m_scr[h] = m_new
            al_scr[slot, h] = jnp.exp2(m_old - m_new)
            p_scr[slot, h] = jnp.exp2(sh - m_sub).astype(BF16)

    n_far = jnp.maximum(n - 1, 0)
    last_far = jnp.maximum(n_far - 1, 0)
    j_before = jnp.maximum(n - 1, 0)
    far_blk = lambda u: jnp.clip(u, 0, last_far)

    for h in heads:
        qk(h, jnp.where(n_far > 0, 0, j_before), 0)

    def far_steps(u, count):
        for c in range(count):
            nxt = far_blk(u + c + 1)
            if c == count - 1:
                nxt = jnp.where(u + count < n_far, nxt, j_before)
            step(c % 2, far_blk(u + c), nxt, far_blk(u + c - 1), None, u + c < n_far)

    done = 0
    for size in MOBA_BODIES:
        left = n_far - done
        trips = (left + 1) // 2 if size == 2 else left // size

        def body(i, carry, size=size, done=done):
            far_steps(done + size * i, size)
            return carry

        lax.fori_loop(0, trips, body, 0)
        done = done + trips * size
    step(0, j_before, n, last_far, _TBL_PREV, n >= 1)
    step(1, n, None, j_before, _TBL_OWN, None)
    for h in heads:
        pv(h, n, 1)
        acc = acc_scr[h]
        o_ref[row[h], :, hsl[h]] = (acc[:MOBA_DH] / acc[MOBA_DH:MOBA_DH + 1]).T.astype(BF16)


def _moba(mqt, mk, mvt, kmean, bias, bsz, seq):
    t, w = mk.shape
    blk = MOBA_BLOCK
    nb = seq // blk
    g = MOBA_ROWS if bsz % MOBA_ROWS == 0 else 1
    units = g * MOBA_HEADS
    in_hbm = pl.BlockSpec(memory_space=pl.ANY)
    out = pl.pallas_call(
        _moba_kernel,
        grid=(bsz // g, nb),
        in_specs=[pl.BlockSpec((g, MOBA_HEADS, 1, MOBA_DH, blk), lambda b, n: (b, 0, n, 0, 0)),
                  in_hbm, in_hbm,
                  pl.BlockSpec((g, nb, w), lambda b, n: (b, 0, 0)),
                  _const_spec(bias.shape)],
        out_specs=pl.BlockSpec((g, blk, w), lambda b, n: (b, n, 0)),
        out_shape=jax.ShapeDtypeStruct((bsz, seq, w), BF16),
        scratch_shapes=[pltpu.VMEM((units, nb, blk), F32),
                        pltpu.VMEM((units, 1, blk), F32),
                        pltpu.VMEM((2, units, 1, blk), F32),
                        pltpu.VMEM((units, MOBA_VROWS, blk), F32),
                        pltpu.VMEM((2, units, blk, blk), BF16),
                        pltpu.VMEM((2, units, blk, blk), F32),
                        pltpu.VMEM((g, seq, w), BF16),
                        pltpu.VMEM((g, MOBA_HEADS, nb, MOBA_VROWS, blk), BF16),
                        pltpu.SemaphoreType.DMA((2, nb))],
        compiler_params=_params(("arbitrary", "arbitrary")),
        name="moba",
    )(mqt, mk.reshape(bsz, seq, w), mvt, kmean, bias)
    return out.reshape(t, w)


def _merge_kernel(h_ref, mod_ref, nw_ref, oa_ref, ob_ref, wga_ref, wgb_ref, wa_ref, wb_ref, wo_ref,
                  o_ref, u_scr):
    mod = mod_ref[0, 0]
    u_scr[...] = _norm_mod(h_ref[...], nw_ref[...], mod).astype(BF16)
    ya = _dot(oa_ref[...], wa_ref[...])
    yb = _dot(ob_ref[...], wb_ref[...])
    merged = _sigmoid(_dot(u_scr[...], wga_ref[...])) * ya + _sigmoid(_dot(u_scr[...], wgb_ref[...])) * yb
    o_ref[...] = h_ref[...] + mod[2:3, :] * _dot(merged.astype(BF16), wo_ref[...])


def _merge(h, mod4, norm_w, oa, ob, wga, wgb, wa, wb, wo, tiles_per_batch):
    t, d = h.shape
    tm = TOKEN_TILE
    row = lambda i: (i, 0)
    return pl.pallas_call(
        _merge_kernel,
        grid=(t // tm,),
        in_specs=[pl.BlockSpec((tm, d), row),
                  pl.BlockSpec((1, 1, 3, d), lambda i: (i // tiles_per_batch, 1, 0, 0)),
                  _const_spec((1, d)),
                  pl.BlockSpec((tm, oa.shape[1]), row), pl.BlockSpec((tm, ob.shape[1]), row),
                  _const_spec(wga.shape), _const_spec(wgb.shape), _const_spec(wa.shape),
                  _const_spec(wb.shape), _const_spec(wo.shape)],
        out_specs=pl.BlockSpec((tm, d), row),
        out_shape=jax.ShapeDtypeStruct((t, d), F32),
        scratch_shapes=[pltpu.VMEM((tm, d), BF16)],
        compiler_params=_params(("arbitrary",)),
        name="merge",
    )(h, mod4, norm_w, oa, ob, wga, wgb, wa, wb, wo)


def kernel(x, c, w_ada, b_ada, norm_ff1, w_ff1_gate, w_ff1_up, w_ff1_down, norm_mix, w_in, w_gla_lr,
           b_gla_lr, gla_norm, rel_bias, w_br_gla, w_br_moba, w_out, norm_ff2, w_ff2_gate, w_ff2_up,
           w_ff2_down, norm_final):
    bsz, seq, d = x.shape
    depth = w_ada.shape[0]
    t = bsz * seq
    tiles_per_batch = seq // TOKEN_TILE
    qk_w = GLA_HEADS * (d // 16)
    v_w = GLA_HEADS * (d // 8)
    m_w = MOBA_HEADS * MOBA_DH
    offs = np.cumsum([0, qk_w, qk_w, v_w, GLA_LOWRANK, v_w, m_w, m_w, m_w, d, d])
    bias = _bias_tables(rel_bias)
    h = x.reshape(t, d)
    c_act_in = c
    for l in range(depth):
        mod4 = _adaln(c_act_in, w_ada, b_ada, l).reshape(bsz, 3, 3, d)
        cast = lambda w: w.astype(BF16)
        h = _ffn(h, mod4, 0, norm_ff1[l].reshape(1, d), w_ff1_gate[l], w_ff1_up[l], w_ff1_down[l], seq)
        wi = w_in[l]
        seg = lambda i: wi[:, offs[i]:offs[i + 1]]
        lr_w = _P_END - _P_LR
        lr_pad = jnp.pad(seg(3), ((0, 0), (0, lr_w - GLA_LOWRANK)))
        wp = cast(jnp.concatenate([seg(0), seg(1), seg(2), seg(4), seg(6), lr_pad], axis=1))
        wqvt = cast(jnp.concatenate([seg(5), seg(7)], axis=1).T)
        wlr = cast(jnp.pad(w_gla_lr[l], ((0, lr_w - GLA_LOWRANK), (0, 0))))
        o_a, mqt, mk, mvt, kmean = _inproj(
            h, mod4, norm_mix[l].reshape(1, d), wp, wqvt, wlr, b_gla_lr[l].reshape(1, qk_w),
            gla_norm[l].reshape(1, -1), bsz, seq)
        o_b = _moba(mqt, mk, mvt, kmean.reshape(bsz, seq // MOBA_BLOCK, m_w), bias, bsz, seq)
        h = _merge(h, mod4, norm_mix[l].reshape(1, d), o_a, o_b, cast(seg(8)), cast(seg(9)),
                   cast(w_br_gla[l]), cast(w_br_moba[l]), cast(w_out[l]), tiles_per_batch)
        last = l == depth - 1
        h = _ffn(h, mod4, 2, norm_ff2[l].reshape(1, d), w_ff2_gate[l], w_ff2_up[l], w_ff2_down[l], seq,
                 norm_final.reshape(1, d) if last else None)
    return h.reshape(bsz, seq, d)
```
